```python
import math
import jax
import jax.numpy as jnp
from jax import lax
import numpy as np

D_MODEL = 1024
BATCH = 8
SEQ = 2048
DEPTH = 2

GRID_W = 64
CTX_LEN = 256
N_ATTN_LAYERS = (DEPTH + 1) // 2
N_REC_LAYERS = DEPTH // 2
EPS = 1e-6
Q_BLOCK = 128
ROPE_BASE = 10000.0

DA_HEADS = 4
DA_QK = 64
DA_V = 2 * DA_QK
DA_SCALE = DA_QK ** -0.5

MLA_HEADS = 8
MLA_NOPE = 64
MLA_ROPE = 32
MLA_V = 64
MLA_Q_RANK = 384
MLA_KV_RANK = 256
MLA_SCALE = (MLA_NOPE + MLA_ROPE) ** -0.5

ATT_SPLITS = (DA_HEADS * 2 * DA_QK, DA_HEADS * 2 * DA_QK, DA_HEADS * DA_V,
              MLA_Q_RANK, MLA_KV_RANK, MLA_ROPE)
ATT_IN = 3 * DA_HEADS * 2 * DA_QK + MLA_Q_RANK + MLA_KV_RANK + MLA_ROPE
ATT_OUT = DA_HEADS * DA_V + MLA_HEADS * MLA_V

GLA_HEADS = 4
GLA_DK = 64
GLA_DV = 128
GLA_GATE_RANK = 16
GLA_GATE_NORM = 16.0

HG_HEADS = 4
HG_DF = 128
HG_DV = 128

CHUNK = 64
REC_SPLITS = (GLA_HEADS * GLA_DK, GLA_HEADS * GLA_DK, GLA_HEADS * GLA_DV, GLA_HEADS * GLA_DV,
              2 * GLA_GATE_RANK, HG_HEADS * HG_DF, 2 * HG_HEADS * HG_DF, HG_HEADS * HG_DV, HG_HEADS * HG_DV)
REC_IN = (2 * GLA_HEADS * GLA_DK + 2 * GLA_HEADS * GLA_DV + 2 * GLA_GATE_RANK
          + 3 * HG_HEADS * HG_DF + 2 * HG_HEADS * HG_DV)
REC_OUT = GLA_HEADS * GLA_DV + HG_HEADS * HG_DV

N_EXPERTS = 16
EXPERT_FF = 2816
EC_CAPACITY_FACTOR = 2

kernel_name = 'hybrid_diffusion_diffattn_mla_gla_hgrn2_ecmoe'


def _split(t, sizes):
    return jnp.split(t, np.cumsum(sizes)[:-1].tolist(), axis=-1)


def rms_norm(x, gain):
    xf = x.astype(jnp.float32)
    y = xf * lax.rsqrt(jnp.mean(xf * xf, axis=-1, keepdims=True) + EPS)
    return (y * gain.astype(jnp.float32)).astype(x.dtype)


def modulate(x, gain, shift, scale):
    return rms_norm(x, gain) * (1 + scale) + shift


def axial_rope(rows, rot_dim):
    row = jnp.repeat(jnp.arange(rows, dtype=jnp.float32), GRID_W)
    col = jnp.tile(jnp.arange(GRID_W, dtype=jnp.float32), rows)
    n_freq = rot_dim // 4
    inv = ROPE_BASE ** (-jnp.arange(n_freq, dtype=jnp.float32) / n_freq)
    ang = jnp.concatenate([row[:, None] * inv, col[:, None] * inv], axis=-1)
    return jnp.cos(ang), jnp.sin(ang)


def apply_rope(x, cos, sin):
    shape = (1, x.shape[1]) + (1,) * (x.ndim - 3) + (cos.shape[-1],)
    cs = cos.reshape(shape).astype(x.dtype)
    sn = sin.reshape(shape).astype(x.dtype)
    x1, x2 = jnp.split(x, 2, axis=-1)
    return jnp.concatenate([x1 * cs - x2 * sn, x2 * cs + x1 * sn], axis=-1)


def over_query_blocks(fn, *qs):
    bsz, n = qs[0].shape[:2]
    nb = n // Q_BLOCK
    blocks = tuple(jnp.moveaxis(q.reshape((bsz, nb, Q_BLOCK) + q.shape[2:]), 1, 0) for q in qs)
    out = lax.map(lambda blk: fn(*blk), blocks)
    return jnp.moveaxis(out, 0, 1).reshape(bsz, n, out.shape[-1])


def attn_project(h, w_in, q_norm_g, q_up, kv_norm_g, kv_up, rope):
    bsz, n, _ = h.shape
    qa, ka, va, cq, ckv, kr = _split(h @ w_in, ATT_SPLITS)
    qa = qa.reshape(bsz, n, DA_HEADS, 2, DA_QK)
    ka = ka.reshape(bsz, n, DA_HEADS, 2, DA_QK)
    va = va.reshape(bsz, n, DA_HEADS, DA_V)
    qm = (rms_norm(cq, q_norm_g) @ q_up).reshape(bsz, n, MLA_HEADS, MLA_NOPE + MLA_ROPE)
    kvm = (rms_norm(ckv, kv_norm_g) @ kv_up).reshape(bsz, n, MLA_HEADS, MLA_NOPE + MLA_V)
    q_nope, q_rope = qm[..., :MLA_NOPE], qm[..., MLA_NOPE:]
    k_nope, vm = kvm[..., :MLA_NOPE], kvm[..., MLA_NOPE:]
    k_rope = kr[:, :, None, :]
    if rope is not None:
        cos_da, sin_da, cos_r, sin_r = rope
        qa = apply_rope(qa, cos_da, sin_da)
        ka = apply_rope(ka, cos_da, sin_da)
        q_rope = apply_rope(q_rope, cos_r, sin_r)
        k_rope = apply_rope(k_rope, cos_r, sin_r)
    qm = jnp.concatenate([q_nope, q_rope], axis=-1)
    km = jnp.concatenate([k_nope, jnp.broadcast_to(k_rope, (bsz, n, MLA_HEADS, MLA_ROPE))], axis=-1)
    queries = (qa[..., 0, :], qa[..., 1, :], qm)
    keys = (ka[..., 0, :], ka[..., 1, :], va, km, vm)
    return queries, keys


def attn_heads(q1, q2, qm, keys, lam, lam_init, subln_g):
    k1, k2, va, km, vm = keys
    bsz, nq = q1.shape[:2]
    s1 = jnp.einsum('bqhd,bkhd->bhqk', q1, k1).astype(jnp.float32) * DA_SCALE
    s2 = jnp.einsum('bqhd,bkhd->bhqk', q2, k2).astype(jnp.float32) * DA_SCALE
    p = jax.nn.softmax(s1, axis=-1) - lam * jax.nn.softmax(s2, axis=-1)
    oa = jnp.einsum('bhqk,bkhd->bqhd', p.astype(va.dtype), va)
    oa = rms_norm(oa, subln_g) * (1.0 - lam_init)
    sm = jnp.einsum('bqhd,bkhd->bhqk', qm, km).astype(jnp.float32) * MLA_SCALE
    om = jnp.einsum('bhqk,bkhd->bqhd', jax.nn.softmax(sm, axis=-1).astype(vm.dtype), vm)
    return jnp.concatenate([oa.reshape(bsz, nq, -1), om.reshape(bsz, nq, -1)], axis=-1)


def attn_mixer(hl, hc, rope, layer_idx, w_in, q_norm_g, q_up, kv_norm_g, kv_up,
               lam_q1, lam_k1, lam_q2, lam_k2, subln_g, w_out, need_ctx):
    lam_init = 0.8 - 0.6 * math.exp(-0.3 * layer_idx)
    f32 = jnp.float32
    lam = (jnp.exp(jnp.sum(lam_q1.astype(f32) * lam_k1.astype(f32)))
           - jnp.exp(jnp.sum(lam_q2.astype(f32) * lam_k2.astype(f32))) + lam_init)
    ql, kl = attn_project(hl, w_in, q_norm_g, q_up, kv_norm_g, kv_up, rope)
    qc, kc = attn_project(hc, w_in, q_norm_g, q_up, kv_norm_g, kv_up, None)
    k_all = tuple(jnp.concatenate([a, b], axis=1) for a, b in zip(kc, kl))
    out_l = over_query_blocks(
        lambda q1, q2, qm: attn_heads(q1, q2, qm, k_all, lam, lam_init, subln_g), *ql) @ w_out
    out_c = attn_heads(qc[0], qc[1], qc[2], kc, lam, lam_init, subln_g) @ w_out if need_ctx else None
    return out_l, out_c


def chunk_gla(q, k, v, log_a, s0, readout=True):
    bsz, t, nh, _ = k.shape
    dv = v.shape[-1]
    nc = t // CHUNK
    f32 = jnp.float32

    def chunks(a):
        return a.astype(f32).reshape(bsz, nc, CHUNK, nh, a.shape[-1])

    kc, vc = chunks(k), chunks(v)
    b = jnp.cumsum(chunks(log_a), axis=2)
    b_last = b[:, :, -1:]
    kv = jnp.einsum('bnlhk,bnlhv->nbhkv', kc * jnp.exp(b_last - b), vc)
    decay = jnp.moveaxis(jnp.exp(b_last[:, :, 0]), 1, 0)

    def step(s, inp):
        dec, upd = inp
        return dec[..., None] * s + upd, s

    s_final, s_start = lax.scan(step, s0, (decay, kv))
    if not readout:
        return None, s_final
    qc = chunks(q)
    o = jnp.einsum('bnlhk,nbhkv->bnlhv', qc * jnp.exp(b), s_start)
    b_ref = b[:, :, CHUNK // 2:CHUNK // 2 + 1]
    scores = jnp.einsum('bnihk,bnjhk->bnhij', qc * jnp.exp(b - b_ref), kc * jnp.exp(b_ref - b))
    scores = jnp.where(jnp.tril(jnp.ones((CHUNK, CHUNK), dtype=bool)), scores, 0.0)
    o = o + jnp.einsum('bnhij,bnjhv->bnihv', scores, vc)
    return o.reshape(bsz, t, nh, dv).astype(v.dtype), s_final


def bidir_scan(fc, fl, name, need_ctx):
    q_l, v_l = fl[name + '_q'], fl[name + '_v']
    bsz, _, nh, dk = q_l.shape
    s0 = jnp.zeros((bsz, nh, dk, v_l.shape[-1]), jnp.float32)
    outs_l, outs_c = [], []
    for d in range(2):
        rev = (lambda t: jnp.flip(t, axis=1)) if d == 1 else (lambda t: t)
        oc, s_ctx = chunk_gla(rev(fc[name + '_q']), rev(fc[name + '_k'][d]), rev(fc[name + '_v']),
                              rev(fc[name + '_la'][d]), s0, readout=need_ctx)
        ol, _ = chunk_gla(rev(q_l), rev(fl[name + '_k'][d]), rev(v_l), rev(fl[name + '_la'][d]), s_ctx)
        outs_l.append(rev(ol))
        if need_ctx:
            outs_c.append(rev(oc))
    o_l = outs_l[0] + outs_l[1]
    o_c = outs_c[0] + outs_c[1] if need_ctx else None
    return o_l, o_c


def rec_features(h, w_in, gk_up, gk_bias, lb):
    bsz, n, _ = h.shape

    def heads(t, nh):
        return t.reshape(bsz, n, nh, -1)

    gq, gk, gv, gg, gdown, hq, hf, hi, hg = _split(h @ w_in, REC_SPLITS)
    gla_la = [heads(jax.nn.log_sigmoid((gd @ gk_up[d] + gk_bias[d]).astype(jnp.float32)) / GLA_GATE_NORM,
                    GLA_HEADS) for d, gd in enumerate(jnp.split(gdown, 2, axis=-1))]
    gla_k = heads(gk, GLA_HEADS)
    hg_f = [lb[d] + (1.0 - lb[d]) * jax.nn.sigmoid(fd.astype(jnp.float32))
            for d, fd in enumerate(jnp.split(hf, 2, axis=-1))]
    return {
        'gla_q': heads(gq, GLA_HEADS) * GLA_DK ** -0.5,
        'gla_k': [gla_k, gla_k],
        'gla_v': heads(gv, GLA_HEADS),
        'gla_la': gla_la,
        'gla_gate': gg,
        'hg_q': heads(jax.nn.silu(hq), HG_HEADS),
        'hg_k': [heads(1.0 - f, HG_HEADS) for f in hg_f],
        'hg_v': heads(hi, HG_HEADS),
        'hg_la': [heads(jnp.log(f), HG_HEADS) for f in hg_f],
        'hg_gate': hg,
    }


def rec_mixer(hl, hc, w_in, gk_up, gk_bias, gla_norm_g, lb, hg_norm_g, w_out, need_ctx):
    fl = rec_features(hl, w_in, gk_up, gk_bias, lb)
    fc = rec_features(hc, w_in, gk_up, gk_bias, lb)
    gla_l, gla_c = bidir_scan(fc, fl, 'gla', need_ctx)
    hg_l, hg_c = bidir_scan(fc, fl, 'hg', need_ctx)

    def readout(o_gla, o_hg, f):
        bsz, n = o_gla.shape[:2]
        a = rms_norm(o_gla, gla_norm_g).reshape(bsz, n, -1) * jax.nn.silu(f['gla_gate'])
        b = rms_norm(o_hg, hg_norm_g).reshape(bsz, n, -1) * jax.nn.silu(f['hg_gate'])
        return jnp.concatenate([a, b], axis=-1) @ w_out

    out_l = readout(gla_l, hg_l, fl)
    out_c = readout(gla_c, hg_c, fc) if need_ctx else None
    return out_l, out_c


def expert_choice_ffn(h, w_router, w_gate, w_up, w_down):
    bsz, n, d = h.shape
    cap = EC_CAPACITY_FACTOR * n // N_EXPERTS
    aff = jax.nn.softmax((h @ w_router).astype(jnp.float32), axis=-1)
    gate, idx = lax.top_k(jnp.swapaxes(aff, 1, 2), cap)
    xe = jax.vmap(lambda hb, ib: hb[ib])(h, idx)
    hid = jax.nn.silu(jnp.einsum('becd,edf->becf', xe, w_gate)) * jnp.einsum('becd,edf->becf', xe, w_up)
    ye = jnp.einsum('becf,efd->becd', hid, w_down) * gate[..., None].astype(h.dtype)
    return jax.vmap(lambda yb, ib: jnp.zeros((n, d), yb.dtype).at[ib.reshape(-1)].add(yb.reshape(-1, d)))(ye, idx)


def setup_inputs(seed: int = 0) -> dict:
    key = jax.random.key(seed)
    ks = iter(jax.random.split(key, 64))

    def nrm(shape, scale):
        return jax.random.normal(next(ks), shape, jnp.float32) * scale

    def gain(shape):
        return 1.0 + nrm(shape, 0.02)

    D, NA, NR, E, F = D_MODEL, N_ATTN_LAYERS, N_REC_LAYERS, N_EXPERTS, EXPERT_FF
    return {
        'x': nrm((BATCH, SEQ, D), 1.0),
        'c': nrm((BATCH, D), 1.0),
        'ctx': nrm((BATCH, CTX_LEN, D), 1.0),
        'c_ctx': nrm((D,), 1.0),
        'ada_w': nrm((DEPTH, D, 6 * D), 0.5 * D ** -0.5),
        'ada_b': nrm((DEPTH, 6 * D), 0.02),
        'norm1_g': gain((DEPTH, D)),
        'norm2_g': gain((DEPTH, D)),
        'att_w_in': nrm((NA, D, ATT_IN), D ** -0.5),
        'mla_q_norm_g': gain((NA, MLA_Q_RANK)),
        'mla_q_up': nrm((NA, MLA_Q_RANK, MLA_HEADS * (MLA_NOPE + MLA_ROPE)), MLA_Q_RANK ** -0.5),
        'mla_kv_norm_g': gain((NA, MLA_KV_RANK)),
        'mla_kv_up': nrm((NA, MLA_KV_RANK, MLA_HEADS * (MLA_NOPE + MLA_V)), MLA_KV_RANK ** -0.5),
        'da_lam_q1': nrm((NA, DA_QK), 0.1),
        'da_lam_k1': nrm((NA, DA_QK), 0.1),
        'da_lam_q2': nrm((NA, DA_QK), 0.1),
        'da_lam_k2': nrm((NA, DA_QK), 0.1),
        'da_subln_g': gain((NA, DA_V)),
        'att_w_out': nrm((NA, ATT_OUT, D), ATT_OUT ** -0.5),
        'rec_w_in': nrm((NR, D, REC_IN), D ** -0.5),
        'gla_gk_up': nrm((NR, 2, GLA_GATE_RANK, GLA_HEADS * GLA_DK), GLA_GATE_RANK ** -0.5),
        'gla_gk_bias': nrm((NR, 2, GLA_HEADS * GLA_DK), 0.02),
        'gla_norm_g': gain((NR, GLA_DV)),
        'hg_lb_logits': nrm((DEPTH, 2, HG_HEADS * HG_DF), 0.1),
        'hg_norm_g': gain((NR, HG_DV)),
        'rec_w_out': nrm((NR, REC_OUT, D), REC_OUT ** -0.5),
        'moe_router': nrm((DEPTH, D, E), D ** -0.5),
        'moe_w_gate': nrm((DEPTH, E, D, F), D ** -0.5),
        'moe_w_up': nrm((DEPTH, E, D, F), D ** -0.5),
        'moe_w_down': nrm((DEPTH, E, F, D), F ** -0.5),
        'final_norm_g': gain((D,)),
    }


def reference(x, c, ctx, c_ctx, ada_w, ada_b, norm1_g, norm2_g, att_w_in, mla_q_norm_g, mla_q_up,
              mla_kv_norm_g, mla_kv_up, da_lam_q1, da_lam_k1, da_lam_q2, da_lam_k2, da_subln_g, att_w_out,
              rec_w_in, gla_gk_up, gla_gk_bias, gla_norm_g, hg_lb_logits, hg_norm_g, rec_w_out,
              moe_router, moe_w_gate, moe_w_up, moe_w_down, final_norm_g):
    rows = x.shape[1] // GRID_W
    cos_da, sin_da = axial_rope(rows, DA_QK)
    cos_r, sin_r = axial_rope(rows, MLA_ROPE)
    rope = (cos_da, sin_da, cos_r, sin_r)
    lb = jax.nn.softmax(hg_lb_logits.astype(jnp.float32), axis=0)
    lb = jnp.cumsum(lb, axis=0) - lb[0]
    sc_l = jax.nn.silu(c)
    sc_c = jax.nn.silu(c_ctx)
    xl, xc = x, ctx
    for i in range(DEPTH):
        last = i == DEPTH - 1
        j = i // 2
        mod_l = jnp.split((sc_l @ ada_w[i] + ada_b[i])[:, None, :], 6, axis=-1)
        mod_c = jnp.split(sc_c @ ada_w[i] + ada_b[i], 6, axis=-1)
        hl = modulate(xl, norm1_g[i], mod_l[0], mod_l[1])
        hc = modulate(xc, norm1_g[i], mod_c[0], mod_c[1])
        if i % 2 == 0:
            ml, mc = attn_mixer(hl, hc, rope, i, att_w_in[j], mla_q_norm_g[j], mla_q_up[j], mla_kv_norm_g[j],
                                mla_kv_up[j], da_lam_q1[j], da_lam_k1[j], da_lam_q2[j], da_lam_k2[j],
                                da_subln_g[j], att_w_out[j], not last)
        else:
            ml, mc = rec_mixer(hl, hc, rec_w_in[j], gla_gk_up[j], gla_gk_bias[j], gla_norm_g[j], lb[i],
                               hg_norm_g[j], rec_w_out[j], not last)
        xl = xl + mod_l[2] * ml
        hl = modulate(xl, norm2_g[i], mod_l[3], mod_l[4])
        xl = xl + mod_l[5] * expert_choice_ffn(hl, moe_router[i], moe_w_gate[i], moe_w_up[i], moe_w_down[i])
        if not last:
            xc = xc + mod_c[2] * mc
            hc = modulate(xc, norm2_g[i], mod_c[3], mod_c[4])
            xc = xc + mod_c[5] * expert_choice_ffn(hc, moe_router[i], moe_w_gate[i], moe_w_up[i], moe_w_down[i])
    return rms_norm(xl, final_norm_g)
```

```python
import functools
import math

import jax
import jax.numpy as jnp
from jax import lax
from jax.experimental import pallas as pl
from jax.experimental.pallas import tpu as pltpu

F32 = jnp.float32
BF16 = jnp.bfloat16

EPS = 1e-6
GRID_W = 64
ROPE_BASE = 10000.0

DA_HEADS = 4
DA_QK = 64
DA_V = 2 * DA_QK
DA_SCALE = DA_QK ** -0.5
MLA_HEADS = 8
MLA_NOPE = 64
MLA_ROPE = 32
MLA_V = 64
MLA_Q_RANK = 384
MLA_KV_RANK = 256
MLA_SCALE = (MLA_NOPE + MLA_ROPE) ** -0.5

GLA_HEADS = 4
GLA_DK = 64
GLA_DV = 128
GLA_GATE_RANK = 16
GLA_GATE_NORM = 16.0
HG_HEADS = 4
HG_DF = 128
HG_DV = 128
CHUNK = 64

N_EXPERTS = 16
EC_CAPACITY_FACTOR = 2

LANES = 128
ROW_TILE = 256
VMEM_LIMIT = 56 * 1024 * 1024


def _params(*sem):
    return pltpu.CompilerParams(dimension_semantics=sem, vmem_limit_bytes=VMEM_LIMIT)


def _dot(a, b):
    return jnp.dot(a, b, preferred_element_type=F32)


def _dot_nt(a, b):
    return lax.dot_general(a, b, (((1,), (1,)), ((), ())), preferred_element_type=F32)


def _dot_tn(a, b):
    return lax.dot_general(a, b, (((0,), (0,)), ((), ())), preferred_element_type=F32)


def _split2(a):
    hi = a.astype(BF16)
    lo = (a - hi.astype(F32)).astype(BF16)
    return hi, lo


def _split3(a):
    hi = a.astype(BF16)
    r = a - hi.astype(F32)
    mid = r.astype(BF16)
    lo = (r - mid.astype(F32)).astype(BF16)
    return hi, mid, lo


def _dot3(a, b, dot=_dot):
    ah, al = _split2(a)
    bh, bl = _split2(b)
    return dot(ah, bh) + (dot(ah, bl) + dot(al, bh))


def _rms(x):
    return x * lax.rsqrt(jnp.mean(x * x, axis=-1, keepdims=True) + EPS)


def _silu(x):
    return x * jax.nn.sigmoid(x)


def _modulate(x, g, shift, scale):
    return _rms(x) * g * (1.0 + scale) + shift


def _ada_kernel(c_ref, w_ref, b_ref, o_ref):
    o_ref[...] = _dot3(_silu(c_ref[...]), w_ref[...]) + b_ref[...]


def _ada_vectors(cvec, ada_w, ada_b):
    depth, d, n6 = ada_w.shape
    tn = n6 // 4
    rows = cvec.shape[0]
    return pl.pallas_call(
        _ada_kernel,
        grid=(depth, n6 // tn),
        in_specs=[pl.BlockSpec((rows, d), lambda l, j: (0, 0)),
                  pl.BlockSpec((None, d, tn), lambda l, j: (l, 0, j)),
                  pl.BlockSpec((None, 1, tn), lambda l, j: (l, 0, j))],
        out_specs=pl.BlockSpec((None, rows, tn), lambda l, j: (l, 0, j)),
        out_shape=jax.ShapeDtypeStruct((depth, rows, n6), F32),
        compiler_params=_params("arbitrary", "arbitrary"),
        name="ada_vectors",
    )(cvec, ada_w, ada_b.reshape(depth, 1, n6))


def _axial_angles(rows, rot_dim):
    row = jnp.repeat(jnp.arange(rows, dtype=F32), GRID_W)
    col = jnp.tile(jnp.arange(GRID_W, dtype=F32), rows)
    n_freq = rot_dim // 4
    inv = ROPE_BASE ** (-jnp.arange(n_freq, dtype=F32) / n_freq)
    ang = jnp.concatenate([row[:, None] * inv, col[:, None] * inv], axis=-1)
    return jnp.cos(ang), jnp.sin(ang)


def _rope_tables(n_ctx, seq, width, start, group):
    half = width // 2
    cos, sin = _axial_angles(seq // GRID_W, width)
    lane = jnp.arange(LANES) % group - start
    inside = (lane >= 0) & (lane < width)
    idx = jnp.clip(lane, 0, width - 1) % half
    second = inside & (lane >= half)
    firsth = inside & (lane < half)
    c = jnp.where(inside[None, :], cos[:, idx], 1.0)
    s_prev = jnp.where(second[None, :], sin[:, idx], 0.0)
    s_next = jnp.where(firsth[None, :], -sin[:, idx], 0.0)
    ident = jnp.ones((n_ctx, LANES), F32)
    zero = jnp.zeros((n_ctx, LANES), F32)
    return (jnp.concatenate([ident, c], 0), jnp.concatenate([zero, s_prev], 0),
            jnp.concatenate([zero, s_next], 0))


def _rope(x, cos, s_prev, s_next, half):
    return x * cos + pltpu.roll(x, half, 1) * s_prev + pltpu.roll(x, LANES - half, 1) * s_next


def _attn_proj_kernel(x_ref, mod_ref, g_ref, w_ref, qn_ref, qup_ref, kvn_ref, kvup_ref,
                      cda_ref, pda_ref, nda_ref, cm_ref, pm_ref, nm_ref,
                      qda_ref, kda_ref, va_ref, qm_ref, km_ref, vm_ref):
    mod = mod_ref[0, 0]
    h = _modulate(x_ref[0], g_ref[...], mod[0:1], mod[1:2]).astype(BF16)
    proj = _dot(h, w_ref[...])
    cda, pda, nda = cda_ref[...], pda_ref[...], nda_ref[...]
    cm, pm, nm = cm_ref[...], pm_ref[...], nm_ref[...]
    n_da = DA_HEADS * LANES
    for hh in range(DA_HEADS):
        sl = slice(LANES * hh, LANES * (hh + 1))
        q = proj[:, LANES * hh:LANES * (hh + 1)]
        k = proj[:, n_da + LANES * hh:n_da + LANES * (hh + 1)]
        qda_ref[0, :, sl] = (_rope(q, cda, pda, nda, DA_QK // 2) * DA_SCALE).astype(BF16)
        kda_ref[0, :, sl] = _rope(k, cda, pda, nda, DA_QK // 2).astype(BF16)
    va_ref[0] = proj[:, 2 * n_da:3 * n_da].astype(BF16)
    o = 3 * n_da
    cq = proj[:, o:o + MLA_Q_RANK]
    ckv = proj[:, o + MLA_Q_RANK:o + MLA_Q_RANK + MLA_KV_RANK]
    kr = proj[:, o + MLA_Q_RANK + MLA_KV_RANK:o + MLA_Q_RANK + MLA_KV_RANK + LANES]
    qm = _dot((_rms(cq) * qn_ref[...]).astype(BF16), qup_ref[...])
    kv = _dot((_rms(ckv) * kvn_ref[...]).astype(BF16), kvup_ref[...])
    rep = kr + pltpu.roll(kr, 32, 1) + pltpu.roll(kr, 64, 1) + pltpu.roll(kr, 96, 1)
    lane = lax.broadcasted_iota(jnp.int32, (1, LANES), 1)
    rep = jnp.where((lane >= MLA_NOPE) & (lane < MLA_NOPE + MLA_ROPE), rep, 0.0)
    rep = _rope(rep, cm, pm, nm, MLA_ROPE // 2)
    for hh in range(MLA_HEADS):
        sl = slice(LANES * hh, LANES * (hh + 1))
        qm_ref[0, :, sl] = (_rope(qm[:, sl], cm, pm, nm, MLA_ROPE // 2) * MLA_SCALE).astype(BF16)
        km_ref[0, :, sl] = (kv[:, sl] + rep).astype(BF16)
    vm_ref[0] = kv[:, MLA_HEADS * LANES:].astype(BF16)


def _attn_project(xall, mod, g, w_in, qn_g, q_up, kvn_g, kv_up, tabs_da, tabs_m, n_ctx):
    bsz, t, d = xall.shape
    tm = ROW_TILE
    ncb = n_ctx // tm
    n_da = DA_HEADS * LANES
    pad = (-w_in.shape[1]) % LANES
    w = jnp.pad(w_in, ((0, 0), (0, pad))).astype(BF16)
    qup = jnp.pad(q_up.reshape(MLA_Q_RANK, MLA_HEADS, MLA_NOPE + MLA_ROPE),
                  ((0, 0), (0, 0), (0, LANES - MLA_NOPE - MLA_ROPE))).reshape(MLA_Q_RANK, -1).astype(BF16)
    kv3 = kv_up.reshape(MLA_KV_RANK, MLA_HEADS, MLA_NOPE + MLA_V)
    kvup = jnp.concatenate([
        jnp.pad(kv3[:, :, :MLA_NOPE], ((0, 0), (0, 0), (0, LANES - MLA_NOPE))).reshape(MLA_KV_RANK, -1),
        kv3[:, :, MLA_NOPE:].reshape(MLA_KV_RANK, -1)], axis=1).astype(BF16)
    row = lambda b, i: (b, i, 0)
    const = lambda b, i: (0, 0)
    tab = pl.BlockSpec((tm, LANES), lambda b, i: (i, 0))
    nm = MLA_HEADS * LANES
    nv = MLA_HEADS * MLA_V
    outs = [(n_da,), (n_da,), (n_da,), (nm,), (nm,), (nv,)]
    return pl.pallas_call(
        _attn_proj_kernel,
        grid=(bsz, t // tm),
        in_specs=[pl.BlockSpec((1, tm, d), row),
                  pl.BlockSpec((1, 1, 8, d), lambda b, i: (b, jnp.minimum(i // ncb, 1), 0, 0)),
                  pl.BlockSpec((1, d), const),
                  pl.BlockSpec(w.shape, const),
                  pl.BlockSpec((1, MLA_Q_RANK), const),
                  pl.BlockSpec(qup.shape, const),
                  pl.BlockSpec((1, MLA_KV_RANK), const),
                  pl.BlockSpec(kvup.shape, const),
                  tab, tab, tab, tab, tab, tab],
        out_specs=[pl.BlockSpec((1, tm, n), row) for (n,) in outs],
        out_shape=[jax.ShapeDtypeStruct((bsz, t, n), BF16) for (n,) in outs],
        compiler_params=_params("arbitrary", "arbitrary"),
        name="attn_project",
    )(xall, mod, g.reshape(1, d), w, qn_g.reshape(1, -1), qup, kvn_g.reshape(1, -1), kvup,
      *tabs_da, *tabs_m)


def _softmax_parts(s):
    e = jnp.exp(s - jnp.max(s, axis=-1, keepdims=True))
    return e, 1.0 / jnp.sum(e, axis=-1, keepdims=True)


def _attn_kernel(qda_ref, qm_ref, kda_ref, va_ref, km_ref, vm_ref, lam_ref, subg_ref, wout_ref,
                 x_ref, mod_ref, o_ref, *, n_ctx, n_tok, tq, lam_init):
    qi = pl.program_id(1)
    lamv = lam_ref[...]
    lam = (jnp.exp(jnp.sum(lamv[0:1] * lamv[1:2], axis=1, keepdims=True))
           - jnp.exp(jnp.sum(lamv[2:3] * lamv[3:4], axis=1, keepdims=True)) + lam_init)
    lane = lax.broadcasted_iota(jnp.int32, (1, LANES), 1)
    first = lane < LANES // 2

    def run(nk):
        outs = []
        for h in range(DA_HEADS):
            sl = slice(LANES * h, LANES * (h + 1))
            q = qda_ref[0, :, sl].astype(F32)
            k = kda_ref[0, 0:nk, sl]
            e1, r1 = _softmax_parts(_dot_nt(jnp.where(first, q, 0.0).astype(BF16), k))
            e2, r2 = _softmax_parts(_dot_nt(jnp.where(first, 0.0, q).astype(BF16), k))
            p = e1 * r1 - e2 * (lam * r2)
            oa = _dot(p.astype(BF16), va_ref[0, 0:nk, sl])
            outs.append(_rms(oa) * subg_ref[...] * (1.0 - lam_init))
        for j in range(MLA_HEADS // 2):
            vpair = vm_ref[0, 0:nk, LANES * j:LANES * (j + 1)]
            res = []
            for u in range(2):
                sl = slice(LANES * (2 * j + u), LANES * (2 * j + u + 1))
                e, r = _softmax_parts(_dot_nt(qm_ref[0, :, sl], km_ref[0, 0:nk, sl]))
                res.append(_dot((e * r).astype(BF16), vpair))
            outs.append(jnp.where(first, res[0], res[1]))
        att = jnp.concatenate(outs, axis=1).astype(BF16)
        o_ref[0] = x_ref[0] + mod_ref[0, 0][2:3] * _dot(att, wout_ref[...])

    @pl.when(qi < n_ctx // tq)
    def _():
        run(n_ctx)

    @pl.when(qi >= n_ctx // tq)
    def _():
        run(n_tok)


def _attention(qda, kda, va, qm, km, vm, lamv, subln_g, w_out, xall, mod, n_ctx, lam_init):
    bsz, t, d = xall.shape
    tq = 128
    ncb = n_ctx // tq
    row = lambda b, i: (b, i, 0)
    full = lambda b, i: (b, 0, 0)
    const = lambda b, i: (0, 0)
    wout = w_out.astype(BF16)
    kern = functools.partial(_attn_kernel, n_ctx=n_ctx, n_tok=t, tq=tq, lam_init=lam_init)
    return pl.pallas_call(
        kern,
        grid=(bsz, t // tq),
        in_specs=[pl.BlockSpec((1, tq, qda.shape[2]), row),
                  pl.BlockSpec((1, tq, qm.shape[2]), row),
                  pl.BlockSpec((1, t, kda.shape[2]), full),
                  pl.BlockSpec((1, t, va.shape[2]), full),
                  pl.BlockSpec((1, t, km.shape[2]), full),
                  pl.BlockSpec((1, t, vm.shape[2]), full),
                  pl.BlockSpec(lamv.shape, const),
                  pl.BlockSpec((1, DA_V), const),
                  pl.BlockSpec(wout.shape, const),
                  pl.BlockSpec((1, tq, d), row),
                  pl.BlockSpec((1, 1, 8, d), lambda b, i: (b, jnp.minimum(i // ncb, 1), 0, 0))],
        out_specs=pl.BlockSpec((1, tq, d), row),
        out_shape=jax.ShapeDtypeStruct((bsz, t, d), F32),
        compiler_params=_params("arbitrary", "arbitrary"),
        name="attention",
    )(qda, qm, kda, va, km, vm, lamv, subln_g.reshape(1, -1), wout, xall, mod)


def _router_kernel(x_ref, mod_ref, g_ref, wrt_ref, h_ref, lg_ref):
    mod = mod_ref[0, 0]
    h = _modulate(x_ref[0], g_ref[...], mod[3:4], mod[4:5])
    h_ref[0] = h.astype(BF16)
    lg_ref[0] = _dot3(wrt_ref[...], h, _dot_nt)


def _router(x, mod, g, w_router, n_ctx):
    bsz, t, d = x.shape
    tm = ROW_TILE
    ncb = max(n_ctx // tm, 1)
    has_ctx = n_ctx > 0
    ne = w_router.shape[1]
    grp = (lambda b, i: (b, jnp.minimum(i // ncb, 1), 0, 0)) if has_ctx else (lambda b, i: (b, 1, 0, 0))
    return pl.pallas_call(
        _router_kernel,
        grid=(bsz, t // tm),
        in_specs=[pl.BlockSpec((1, tm, d), lambda b, i: (b, i, 0)),
                  pl.BlockSpec((1, 1, 8, d), grp),
                  pl.BlockSpec((1, d), lambda b, i: (0, 0)),
                  pl.BlockSpec((ne, d), lambda b, i: (0, 0))],
        out_specs=[pl.BlockSpec((1, tm, d), lambda b, i: (b, i, 0)),
                   pl.BlockSpec((1, ne, tm), lambda b, i: (b, 0, i))],
        out_shape=[jax.ShapeDtypeStruct((bsz, t, d), BF16),
                   jax.ShapeDtypeStruct((bsz, ne, t), F32)],
        compiler_params=_params("arbitrary", "arbitrary"),
        name="moe_router",
    )(x, mod, g.reshape(1, d), w_router.T)


def _kth_largest(a, k):
    rows = a.shape[0]
    lo0 = jnp.full((rows, 1), -1.0, F32)
    hi0 = jnp.max(a, axis=1, keepdims=True)

    def cond(c):
        return c[2] > 0

    def body(c):
        lo, hi, _ = c
        mid = 0.5 * (lo + hi)
        open_ = (mid > lo) & (mid < hi)
        cnt = jnp.sum(jnp.where(a > mid, 1.0, 0.0), axis=1, keepdims=True)
        up = open_ & (cnt >= k)
        down = open_ & (cnt < k)
        lo = jnp.where(up, mid, lo)
        hi = jnp.where(down, mid, hi)
        return lo, hi, jnp.sum(jnp.where(open_, 1.0, 0.0))

    _, hi, _ = lax.while_loop(cond, body, (lo0, hi0, jnp.float32(1.0)))
    return hi


def _select_kernel(lg_ref, slot_ref, aff_ref, tri_ref, *, segs):
    nmax = tri_ref.shape[0]

    @pl.when(pl.program_id(0) == 0)
    def _():
        rt = 256
        for r in range(0, nmax, rt):
            ri = lax.broadcasted_iota(jnp.int32, (rt, nmax), 0) + r
            ci = lax.broadcasted_iota(jnp.int32, (rt, nmax), 1)
            tri_ref[r:r + rt, :] = jnp.where(ri < ci, 1.0, 0.0).astype(BF16)

    lg = lg_ref[0]
    e = jnp.exp(lg - jnp.max(lg, axis=0, keepdims=True))
    aff = e / jnp.sum(e, axis=0, keepdims=True)
    aff_ref[0] = aff
    for (start, n, cap) in segs:
        a = aff[:, start:start + n]
        thr = _kth_largest(a, cap)
        gt = a > thr
        eq = a == thr
        need = cap - jnp.sum(jnp.where(gt, 1.0, 0.0), axis=1, keepdims=True)
        tri = tri_ref[0:n, 0:n]
        before = _dot(jnp.where(eq, 1.0, 0.0).astype(BF16), tri)
        sel = gt | (eq & (before < need))
        pos = _dot(jnp.where(sel, 1.0, 0.0).astype(BF16), tri)
        slot_ref[0, :, start:start + n] = jnp.where(sel, pos, -1.0).astype(jnp.int32)


def _select(logits, segs):
    bsz, ne, t = logits.shape
    nmax = max(n for (_, n, _) in segs)
    blk = pl.BlockSpec((1, ne, t), lambda b: (b, 0, 0))
    return pl.pallas_call(
        functools.partial(_select_kernel, segs=segs),
        grid=(bsz,),
        in_specs=[blk],
        out_specs=[blk, blk],
        out_shape=[jax.ShapeDtypeStruct((bsz, ne, t), jnp.int32),
                   jax.ShapeDtypeStruct((bsz, ne, t), F32)],
        scratch_shapes=[pltpu.VMEM((nmax, nmax), BF16)],
        compiler_params=_params("arbitrary"),
        name="moe_select",
    )(logits)


def _gather_kernel(h_ref, slot_ref, *out_refs, segs):
    for (start, n, cap), o_ref in zip(segs, out_refs):
        sl = slot_ref[0, 0, :, start:start + n]
        pi = lax.broadcasted_iota(jnp.int32, (cap, n), 0)
        onehot = jnp.where(sl == pi, 1.0, 0.0).astype(BF16)
        o_ref[0] = _dot(onehot, h_ref[0, start:start + n, :]).astype(BF16)


def _gather(h, slot, segs):
    bsz, t, d = h.shape
    ne = slot.shape[1]
    return pl.pallas_call(
        functools.partial(_gather_kernel, segs=segs),
        grid=(bsz, ne),
        in_specs=[pl.BlockSpec((1, t, d), lambda b, e: (b, 0, 0)),
                  pl.BlockSpec((1, 1, 1, t), lambda b, e: (b, e, 0, 0))],
        out_specs=[pl.BlockSpec((1, cap, d), lambda b, e: (e, b, 0)) for (_, _, cap) in segs],
        out_shape=[jax.ShapeDtypeStruct((ne, bsz * cap, d), BF16) for (_, _, cap) in segs],
        compiler_params=_params("arbitrary", "arbitrary"),
        name="moe_gather",
    )(h, slot.reshape(bsz, ne, 1, t))


def _ffn_kernel(*refs, nseg, rows_per_pass):
    x_refs = refs[:nseg]
    wg_ref, wu_ref, wd_ref = refs[nseg:nseg + 3]
    y_refs = refs[nseg + 3:2 * nseg + 3]
    acc_refs = refs[2 * nseg + 3:]
    f = pl.program_id(1)
    wg = wg_ref[0].astype(BF16)
    wu = wu_ref[0].astype(BF16)
    wd = wd_ref[0].astype(BF16)
    for x_ref, y_ref, acc_ref in zip(x_refs, y_refs, acc_refs):
        m = x_ref.shape[1]
        step = min(rows_per_pass, m)
        for r in range(0, m, step):
            x = x_ref[0, r:r + step, :]
            hid = (_silu(_dot(x, wg)) * _dot(x, wu)).astype(BF16)
            part = _dot(hid, wd)

            @pl.when(f == 0)
            def _():
                acc_ref[r:r + step, :] = part

            @pl.when(f > 0)
            def _():
                acc_ref[r:r + step, :] += part

        @pl.when(f == pl.num_programs(1) - 1)
        def _():
            y_ref[0] = acc_ref[...].astype(BF16)


def _ffn(xes, w_gate, w_up, w_down):
    ne, d, ff = w_gate.shape
    tf = 256
    nseg = len(xes)
    return pl.pallas_call(
        functools.partial(_ffn_kernel, nseg=nseg, rows_per_pass=512),
        grid=(ne, ff // tf),
        in_specs=[pl.BlockSpec((1, xe.shape[1], d), lambda e, f: (e, 0, 0)) for xe in xes]
        + [pl.BlockSpec((1, d, tf), lambda e, f: (e, 0, f)),
           pl.BlockSpec((1, d, tf), lambda e, f: (e, 0, f)),
           pl.BlockSpec((1, tf, d), lambda e, f: (e, f, 0))],
        out_specs=[pl.BlockSpec((1, xe.shape[1], d), lambda e, f: (e, 0, 0)) for xe in xes],
        out_shape=[jax.ShapeDtypeStruct(xe.shape, BF16) for xe in xes],
        scratch_shapes=[pltpu.VMEM(xe.shape[1:], F32) for xe in xes],
        compiler_params=_params("arbitrary", "arbitrary"),
        name="moe_ffn",
    )(*xes, w_gate, w_up, w_down)


def _combine_kernel(*refs, segs, tm, final_norm):
    nseg = len(segs)
    slot_ref, aff_ref = refs[0], refs[1]
    ye_refs = refs[2:2 + nseg]
    x_ref, mod_ref = refs[2 + nseg], refs[3 + nseg]
    rest = refs[4 + nseg:]
    if final_norm:
        fg_ref, o_ref = rest
    else:
        (o_ref,) = rest
    i = pl.program_id(1)

    def run(ye_ref, cap):
        acc = jnp.zeros(x_ref.shape[1:], F32)
        pi = lax.broadcasted_iota(jnp.int32, (1, cap), 1)
        for e in range(ye_ref.shape[0]):
            onehot = jnp.where(slot_ref[0, :, e:e + 1] == pi, 1.0, 0.0).astype(BF16)
            acc = acc + aff_ref[0, :, e:e + 1] * _dot(onehot, ye_ref[e])
        y = x_ref[0] + mod_ref[0, 0][5:6] * acc
        if final_norm:
            y = _rms(y) * fg_ref[...]
        o_ref[0] = y

    for (start, n, cap), ye_ref in zip(segs, ye_refs):
        if nseg == 1:
            run(ye_ref, cap)
        else:
            @pl.when((i >= start // tm) & (i < (start + n) // tm))
            def _():
                run(ye_ref, cap)


def _combine(slot_t, aff_t, yes, x, mod, segs, n_ctx, final_g):
    bsz, t, d = x.shape
    ne = slot_t.shape[2]
    tm = ROW_TILE
    ncb = max(n_ctx // tm, 1)
    grp = (lambda b, i: (b, jnp.minimum(i // ncb, 1), 0, 0)) if n_ctx > 0 else (lambda b, i: (b, 1, 0, 0))
    final_norm = final_g is not None
    in_specs = ([pl.BlockSpec((1, tm, ne), lambda b, i: (b, i, 0)),
                 pl.BlockSpec((1, tm, ne), lambda b, i: (b, i, 0))]
                + [pl.BlockSpec((ne, cap, d), lambda b, i: (0, b, 0)) for (_, _, cap) in segs]
                + [pl.BlockSpec((1, tm, d), lambda b, i: (b, i, 0)),
                   pl.BlockSpec((1, 1, 8, d), grp)])
    args = [slot_t, aff_t, *yes, x, mod]
    if final_norm:
        in_specs.append(pl.BlockSpec((1, d), lambda b, i: (0, 0)))
        args.append(final_g.reshape(1, d))
    return pl.pallas_call(
        functools.partial(_combine_kernel, segs=segs, tm=tm, final_norm=final_norm),
        grid=(bsz, t // tm),
        in_specs=in_specs,
        out_specs=pl.BlockSpec((1, tm, d), lambda b, i: (b, i, 0)),
        out_shape=jax.ShapeDtypeStruct((bsz, t, d), F32),
        compiler_params=_params("arbitrary", "arbitrary"),
        name="moe_combine",
    )(*args)


def _moe(x, mod, g, w_router, w_gate, w_up, w_down, n_ctx, final_g=None):
    bsz, t, d = x.shape
    ne = w_router.shape[1]
    segs = []
    if n_ctx > 0:
        segs.append((0, n_ctx, EC_CAPACITY_FACTOR * n_ctx // ne))
    segs.append((n_ctx, t - n_ctx, EC_CAPACITY_FACTOR * (t - n_ctx) // ne))
    segs = tuple(segs)
    h, logits = _router(x, mod, g, w_router, n_ctx)
    slot, aff = _select(logits, segs)
    xes = _gather(h, slot, segs)
    yes = _ffn(xes, w_gate, w_up, w_down)
    return _combine(jnp.swapaxes(slot, 1, 2), jnp.swapaxes(aff, 1, 2), yes, x, mod, segs, n_ctx, final_g)


def _rec_proj_kernel(x_ref, mod_ref, g_ref, w_ref, gkup_ref, gkb_ref, lbl_ref,
                     gq_ref, gk_ref, gv_ref, gla_ref, gg_ref, hq_ref, hf_ref, hv_ref, hg_ref, *, layer):
    mod = mod_ref[0, 0]
    h = _modulate(x_ref[0], g_ref[...], mod[0:1], mod[1:2]).astype(BF16)
    proj = _dot(h, w_ref[...])
    n = GLA_HEADS * LANES
    gq_ref[0] = proj[:, 0:n] * GLA_DK ** -0.5
    gk_ref[0] = proj[:, n:2 * n]
    gv_ref[0] = proj[:, 2 * n:3 * n].astype(BF16)
    gg_ref[0] = proj[:, 3 * n:4 * n]
    hq_ref[0] = _silu(proj[:, 4 * n:5 * n])
    hv_ref[0] = proj[:, 7 * n:8 * n].astype(BF16)
    hg_ref[0] = proj[:, 8 * n:9 * n]
    gdown = proj[:, 9 * n:9 * n + LANES]
    gate = _dot(gdown.astype(BF16), gkup_ref[...]) + gkb_ref[...]
    lbl = lbl_ref[...]
    e = jnp.exp(lbl - jnp.max(lbl, axis=0, keepdims=True))
    sm = e / jnp.sum(e, axis=0, keepdims=True)
    lb = jnp.sum(sm[0:layer + 1], axis=0, keepdims=True) - sm[0:1]
    f = lb + (1.0 - lb) * jax.nn.sigmoid(proj[:, 5 * n:7 * n])
    for dd in range(2):
        gla_ref[dd, 0] = jax.nn.log_sigmoid(gate[:, dd * n:(dd + 1) * n]) / GLA_GATE_NORM
        hf_ref[dd, 0] = f[:, dd * n:(dd + 1) * n]


def _pad_heads(w, heads, dk):
    lead = w.shape[:-1]
    w3 = w.reshape(lead + (heads, dk))
    return jnp.pad(w3, [(0, 0)] * len(lead) + [(0, 0), (0, LANES - dk)]).reshape(lead + (heads * LANES,))


def _rec_project(xall, mod, g, w_in, gk_up, gk_bias, lb_logits, n_ctx, layer):
    bsz, t, d = xall.shape
    tm = ROW_TILE
    ncb = n_ctx // tm
    hk = GLA_HEADS * GLA_DK
    hv = GLA_HEADS * GLA_DV
    hf = HG_HEADS * HG_DF
    sizes = (hk, hk, hv, hv, 2 * GLA_GATE_RANK, hf, 2 * hf, HG_HEADS * HG_DV, HG_HEADS * HG_DV)
    offs = [0]
    for s in sizes:
        offs.append(offs[-1] + s)
    gq, gk, gv, gg, gdown, hq, hff, hi, hgt = [w_in[:, offs[i]:offs[i + 1]] for i in range(9)]
    w = jnp.concatenate([_pad_heads(gq, GLA_HEADS, GLA_DK), _pad_heads(gk, GLA_HEADS, GLA_DK), gv, gg,
                         hq, hff, hi, hgt, jnp.pad(gdown, ((0, 0), (0, LANES - 2 * GLA_GATE_RANK)))],
                        axis=1).astype(BF16)
    n = GLA_HEADS * LANES
    gkup = jnp.zeros((LANES, 2 * n), F32)
    for dd in range(2):
        gkup = gkup.at[dd * GLA_GATE_RANK:(dd + 1) * GLA_GATE_RANK, dd * n:(dd + 1) * n].set(
            _pad_heads(gk_up[dd], GLA_HEADS, GLA_DK))
    gkup = gkup.astype(BF16)
    gkb = jnp.concatenate([_pad_heads(gk_bias[dd], GLA_HEADS, GLA_DK) for dd in range(2)]).reshape(1, 2 * n)
    row = lambda b, i: (b, i, 0)
    drow = lambda b, i: (0, b, i, 0)
    const = lambda b, i: (0, 0)
    one = lambda dt: (pl.BlockSpec((1, tm, n), row), jax.ShapeDtypeStruct((bsz, t, n), dt))
    two = lambda dt: (pl.BlockSpec((2, 1, tm, n), drow), jax.ShapeDtypeStruct((2, bsz, t, n), dt))
    outs = [one(F32), one(F32), one(BF16), two(F32), one(F32), one(F32), two(F32), one(BF16), one(F32)]
    return pl.pallas_call(
        functools.partial(_rec_proj_kernel, layer=layer),
        grid=(bsz, t // tm),
        in_specs=[pl.BlockSpec((1, tm, d), row),
                  pl.BlockSpec((1, 1, 8, d), lambda b, i: (b, jnp.minimum(i // ncb, 1), 0, 0)),
                  pl.BlockSpec((1, d), const),
                  pl.BlockSpec(w.shape, const),
                  pl.BlockSpec(gkup.shape, const),
                  pl.BlockSpec(gkb.shape, const),
                  pl.BlockSpec((lb_logits.shape[0], 2 * hf), const)],
        out_specs=[o[0] for o in outs],
        out_shape=[o[1] for o in outs],
        compiler_params=_params("arbitrary", "arbitrary"),
        name="rec_project",
    )(xall, mod, g.reshape(1, d), w, gkup, gkb, lb_logits.reshape(lb_logits.shape[0], 2 * hf))


def _scan_kernel(q_ref, k_ref, v_ref, la_ref, o_ref, st_ref, *, heads, from_forget):
    dirn = pl.program_id(1)
    fwd = dirn == 0

    @pl.when(pl.program_id(2) == 0)
    def _():
        st_ref[...] = jnp.zeros_like(st_ref)

    ri = lax.broadcasted_iota(jnp.int32, (CHUNK, CHUNK), 0)
    ci = lax.broadcasted_iota(jnp.int32, (CHUNK, CHUNK), 1)
    causal = (ci - ri) * jnp.where(fwd, 1, -1) <= 0
    cum = jnp.where(causal, 1.0, 0.0).astype(BF16)
    n_chunks = q_ref.shape[1] // CHUNK

    def chunk(step, carry):
        c = jnp.where(fwd, step, n_chunks - 1 - step)
        rows = pl.ds(pl.multiple_of(c * CHUNK, CHUNK), CHUNK)
        q = q_ref[0, rows, :]
        if from_forget:
            fg = la_ref[0, 0, rows, :]
            k = 1.0 - fg
            la = jnp.log(fg)
        else:
            k = k_ref[0, rows, :]
            la = la_ref[0, 0, rows, :]
        l1, l2, l3 = _split3(la)
        b = _dot(cum, l1) + (_dot(cum, l2) + _dot(cum, l3))
        b_tot = jnp.where(fwd, b[CHUNK - 1:CHUNK], b[0:1])
        b_mid = jnp.where(fwd, b[CHUNK // 2:CHUNK // 2 + 1], b[CHUNK // 2 - 1:CHUNK // 2])
        q_in = (q * jnp.exp(b)).astype(BF16)
        q_mid = (q * jnp.exp(b - b_mid)).astype(BF16)
        k_mid = (k * jnp.exp(b_mid - b)).astype(BF16)
        k_out = (k * jnp.exp(b_tot - b)).astype(BF16)
        decay = jnp.exp(b_tot)
        for h in range(heads):
            sl = slice(LANES * h, LANES * (h + 1))
            v = v_ref[0, rows, sl]
            st = st_ref[h]
            scores = jnp.where(causal, _dot_nt(q_mid[:, sl], k_mid[:, sl]), 0.0)
            o_ref[0, 0, rows, sl] = _dot_nt(q_in[:, sl], st.astype(BF16)) + _dot(scores.astype(BF16), v)
            st_ref[h] = st * decay[:, sl] + _dot_tn(v, k_out[:, sl])
        return carry

    lax.fori_loop(0, n_chunks, chunk, 0)


def _scan(q, k, v, la, n_ctx, from_forget):
    bsz, t, n = q.shape
    heads = n // LANES
    tm = ROW_TILE
    nb = t // tm
    ncb = n_ctx // tm

    def blk(d, s):
        back = jnp.where(s < ncb, ncb - 1 - s, nb - 1 - (s - ncb))
        return jnp.where(d == 0, s, back)

    row = lambda b, d, s: (b, blk(d, s), 0)
    drow = lambda b, d, s: (d, b, blk(d, s), 0)
    if k is None:
        k = q
    return pl.pallas_call(
        functools.partial(_scan_kernel, heads=heads, from_forget=from_forget),
        grid=(bsz, 2, nb),
        in_specs=[pl.BlockSpec((1, tm, n), row),
                  pl.BlockSpec((1, tm, n), row),
                  pl.BlockSpec((1, tm, n), row),
                  pl.BlockSpec((1, 1, tm, n), drow)],
        out_specs=pl.BlockSpec((1, 1, tm, n), drow),
        out_shape=jax.ShapeDtypeStruct((2, bsz, t, n), F32),
        scratch_shapes=[pltpu.VMEM((heads, LANES, LANES), F32)],
        compiler_params=_params("arbitrary", "arbitrary", "arbitrary"),
        name="rec_scan",
    )(q, k, v, la)


def _rec_out_kernel(og_ref, oh_ref, gg_ref, hg_ref, gn_ref, hn_ref, wout_ref, x_ref, mod_ref, o_ref):
    parts = []
    for o2_ref, gate_ref, n_ref in ((og_ref, gg_ref, gn_ref), (oh_ref, hg_ref, hn_ref)):
        for h in range(o2_ref.shape[3] // LANES):
            sl = slice(LANES * h, LANES * (h + 1))
            o = o2_ref[0, 0, :, sl] + o2_ref[1, 0, :, sl]
            parts.append(_rms(o) * n_ref[...] * _silu(gate_ref[0, :, sl]))
    a = jnp.concatenate(parts, axis=1).astype(BF16)
    o_ref[0] = x_ref[0] + mod_ref[0, 0][2:3] * _dot(a, wout_ref[...])


def _rec_readout(og, oh, gg, hg, gn, hn, w_out, xall, mod, n_ctx):
    bsz, t, d = xall.shape
    tm = ROW_TILE
    ncb = n_ctx // tm
    n = og.shape[3]
    row = lambda b, i: (b, i + ncb, 0)
    drow = lambda b, i: (0, b, i + ncb, 0)
    const = lambda b, i: (0, 0)
    wout = w_out.astype(BF16)
    return pl.pallas_call(
        _rec_out_kernel,
        grid=(bsz, (t - n_ctx) // tm),
        in_specs=[pl.BlockSpec((2, 1, tm, n), drow),
                  pl.BlockSpec((2, 1, tm, n), drow),
                  pl.BlockSpec((1, tm, n), row),
                  pl.BlockSpec((1, tm, n), row),
                  pl.BlockSpec((1, LANES), const),
                  pl.BlockSpec((1, LANES), const),
                  pl.BlockSpec(wout.shape, const),
                  pl.BlockSpec((1, tm, d), row),
                  pl.BlockSpec((1, 1, 8, d), lambda b, i: (b, 1, 0, 0))],
        out_specs=pl.BlockSpec((1, tm, d), lambda b, i: (b, i, 0)),
        out_shape=jax.ShapeDtypeStruct((bsz, t - n_ctx, d), F32),
        compiler_params=_params("arbitrary", "arbitrary"),
        name="rec_readout",
    )(og, oh, gg, hg, gn.reshape(1, -1), hn.reshape(1, -1), wout, xall, mod)


def kernel(x, c, ctx, c_ctx, ada_w, ada_b, norm1_g, norm2_g, att_w_in, mla_q_norm_g, mla_q_up, mla_kv_norm_g, mla_kv_up, da_lam_q1, da_lam_k1, da_lam_q2, da_lam_k2, da_subln_g, att_w_out, rec_w_in, gla_gk_up, gla_gk_bias, gla_norm_g, hg_lb_logits, hg_norm_g, rec_w_out, moe_router, moe_w_gate, moe_w_up, moe_w_down, final_norm_g):
    bsz, seq, d = x.shape
    n_ctx = ctx.shape[1]
    depth = ada_w.shape[0]
    assert depth == 2 and n_ctx % ROW_TILE == 0 and seq % ROW_TILE == 0 and bsz + 1 <= 16

    cvec = jnp.concatenate([c, c_ctx[None, :], jnp.zeros((16 - bsz - 1, d), F32)], axis=0)
    ada = _ada_vectors(cvec, ada_w, ada_b).reshape(depth, 16, 6, d)
    ada = jnp.pad(ada, ((0, 0), (0, 0), (0, 2), (0, 0)))
    mods = jnp.stack([jnp.broadcast_to(ada[:, bsz][:, None], (depth, bsz, 8, d)), ada[:, :bsz]], axis=2)

    xall = jnp.concatenate([ctx, x], axis=1)

    mod = mods[0]
    tabs_da = _rope_tables(n_ctx, seq, DA_QK, 0, DA_QK)
    tabs_m = _rope_tables(n_ctx, seq, MLA_ROPE, MLA_NOPE, LANES)
    qda, kda, va, qm, km, vm = _attn_project(xall, mod, norm1_g[0], att_w_in[0], mla_q_norm_g[0], mla_q_up[0],
                                             mla_kv_norm_g[0], mla_kv_up[0], tabs_da, tabs_m, n_ctx)
    lamv = jnp.stack([da_lam_q1[0], da_lam_k1[0], da_lam_q2[0], da_lam_k2[0]])
    lam_init = 0.8 - 0.6 * math.exp(-0.3 * 0)
    xall = _attention(qda, kda, va, qm, km, vm, lamv, da_subln_g[0], att_w_out[0], xall, mod, n_ctx, lam_init)
    xall = _moe(xall, mod, norm2_g[0], moe_router[0], moe_w_gate[0], moe_w_up[0], moe_w_down[0], n_ctx)

    mod = mods[1]
    gq, gk, gv, gla, gg, hq, hf, hv, hg = _rec_project(xall, mod, norm1_g[1], rec_w_in[0], gla_gk_up[0],
                                                       gla_gk_bias[0], hg_lb_logits, n_ctx, 1)
    og = _scan(gq, gk, gv, gla, n_ctx, False)
    oh = _scan(hq, None, hv, hf, n_ctx, True)
    xl = _rec_readout(og, oh, gg, hg, gla_norm_g[0], hg_norm_g[0], rec_w_out[0], xall, mod, n_ctx)
    return _moe(xl, mod, norm2_g[1], moe_router[1], moe_w_gate[1], moe_w_up[1], moe_w_down[1], 0,
                final_g=final_norm_g)
```

```python
import functools
import math

import jax
import jax.numpy as jnp
from jax import lax
from jax.experimental import pallas as pl
from jax.experimental.pallas import tpu as pltpu

F32 = jnp.float32
BF16 = jnp.bfloat16

EPS = 1e-6
GRID_W = 64
ROPE_BASE = 10000.0

DA_HEADS = 4
DA_QK = 64
DA_V = 2 * DA_QK
DA_SCALE = DA_QK ** -0.5
MLA_HEADS = 8
MLA_NOPE = 64
MLA_ROPE = 32
MLA_V = 64
MLA_Q_RANK = 384
MLA_KV_RANK = 256
MLA_SCALE = (MLA_NOPE + MLA_ROPE) ** -0.5
LOG2E = math.log2(math.e)

GLA_HEADS = 4
GLA_DK = 64
GLA_DV = 128
GLA_GATE_RANK = 16
GLA_GATE_NORM = 16.0
HG_HEADS = 4
HG_DF = 128
HG_DV = 128
CHUNK = 64

N_EXPERTS = 16
EC_CAPACITY_FACTOR = 2

LANES = 128
ROW_TILE = 256
VMEM_LIMIT = 56 * 1024 * 1024


def _params(*sem):
    return pltpu.CompilerParams(dimension_semantics=sem, vmem_limit_bytes=VMEM_LIMIT)


def _dot(a, b):
    return jnp.dot(a, b, preferred_element_type=F32)


def _dot_nt(a, b):
    return lax.dot_general(a, b, (((1,), (1,)), ((), ())), preferred_element_type=F32)


def _dot_tn(a, b):
    return lax.dot_general(a, b, (((0,), (0,)), ((), ())), preferred_element_type=F32)


def _split2(a):
    hi = a.astype(BF16)
    lo = (a - hi.astype(F32)).astype(BF16)
    return hi, lo


def _split3(a):
    hi = a.astype(BF16)
    r = a - hi.astype(F32)
    mid = r.astype(BF16)
    lo = (r - mid.astype(F32)).astype(BF16)
    return hi, mid, lo


def _dot3(a, b, dot=_dot):
    ah, al = _split2(a)
    bh, bl = _split2(b)
    return dot(ah, bh) + (dot(ah, bl) + dot(al, bh))


def _rms(x):
    return x * lax.rsqrt(jnp.mean(x * x, axis=-1, keepdims=True) + EPS)


def _silu(x):
    return x * jax.nn.sigmoid(x)


def _modulate(x, g, shift, scale):
    return _rms(x) * g * (1.0 + scale) + shift


def _stream_specs(d):
    return [pl.BlockSpec((1, ROW_TILE, d), lambda b, i: (b, 0, 0)),
            pl.BlockSpec((1, ROW_TILE, d), lambda b, i: (b, jnp.maximum(i - 1, 0), 0))]


def _stream_block(xc_ref, xl_ref):
    return jnp.where(pl.program_id(1) == 0, xc_ref[0], xl_ref[0])


def _mod_spec(d, group=None):
    if group is None:
        return pl.BlockSpec((1, 1, 8, d), lambda b, i: (b, jnp.minimum(i, 1), 0, 0))
    return pl.BlockSpec((1, 1, 8, d), lambda b, i: (b, group, 0, 0))


def _ada_kernel(c_ref, w_ref, b_ref, o_ref):
    o_ref[...] = _dot3(_silu(c_ref[...]), w_ref[...]) + b_ref[...]


def _ada_vectors(cvec, ada_w, ada_b):
    depth, d, n6 = ada_w.shape
    tn = n6 // 4
    rows = cvec.shape[0]
    return pl.pallas_call(
        _ada_kernel,
        grid=(depth, n6 // tn),
        in_specs=[pl.BlockSpec((rows, d), lambda l, j: (0, 0)),
                  pl.BlockSpec((None, d, tn), lambda l, j: (l, 0, j)),
                  pl.BlockSpec((None, 1, tn), lambda l, j: (l, 0, j))],
        out_specs=pl.BlockSpec((None, rows, tn), lambda l, j: (l, 0, j)),
        out_shape=jax.ShapeDtypeStruct((depth, rows, n6), F32),
        compiler_params=_params("arbitrary", "arbitrary"),
        name="ada_vectors",
    )(cvec, ada_w, ada_b.reshape(depth, 1, n6))


def _axial_angles(rows, rot_dim):
    row = jnp.repeat(jnp.arange(rows, dtype=F32), GRID_W)
    col = jnp.tile(jnp.arange(GRID_W, dtype=F32), rows)
    n_freq = rot_dim // 4
    inv = ROPE_BASE ** (-jnp.arange(n_freq, dtype=F32) / n_freq)
    ang = jnp.concatenate([row[:, None] * inv, col[:, None] * inv], axis=-1)
    return jnp.cos(ang), jnp.sin(ang)


def _rope_tables(n_ctx, seq, width, start, group):
    half = width // 2
    cos, sin = _axial_angles(seq // GRID_W, width)
    lane = jnp.arange(LANES) % group - start
    inside = (lane >= 0) & (lane < width)
    idx = jnp.clip(lane, 0, width - 1) % half
    second = inside & (lane >= half)
    firsth = inside & (lane < half)
    c = jnp.where(inside[None, :], cos[:, idx], 1.0)
    s_prev = jnp.where(second[None, :], sin[:, idx], 0.0)
    s_next = jnp.where(firsth[None, :], -sin[:, idx], 0.0)
    ident = jnp.ones((n_ctx, LANES), F32)
    zero = jnp.zeros((n_ctx, LANES), F32)
    return (jnp.concatenate([ident, c], 0), jnp.concatenate([zero, s_prev], 0),
            jnp.concatenate([zero, s_next], 0))


def _rope(x, cos, s_prev, s_next, half):
    return x * cos + pltpu.roll(x, half, 1) * s_prev + pltpu.roll(x, LANES - half, 1) * s_next


def _attn_proj_kernel(xc_ref, xl_ref, mod_ref, g_ref, w_ref, wvt_ref, qn_ref, qup_ref, kvn_ref, kvup_ref,
                      kvupt_ref, cda_ref, pda_ref, nda_ref, cm_ref, pm_ref, nm_ref,
                      qda_ref, kda_ref, vat_ref, qm_ref, km_ref, vmt_ref):
    mod = mod_ref[0, 0]
    h = _modulate(_stream_block(xc_ref, xl_ref), g_ref[...], mod[0:1], mod[1:2]).astype(BF16)
    proj = _dot(h, w_ref[...])
    vat_ref[0] = _dot_nt(wvt_ref[...], h).astype(BF16)
    cda, pda, nda = cda_ref[...], pda_ref[...], nda_ref[...]
    cm, pm, nm = cm_ref[...], pm_ref[...], nm_ref[...]
    n_da = DA_HEADS * LANES
    for hh in range(DA_HEADS):
        sl = slice(LANES * hh, LANES * (hh + 1))
        q = proj[:, LANES * hh:LANES * (hh + 1)]
        k = proj[:, n_da + LANES * hh:n_da + LANES * (hh + 1)]
        qda_ref[0, :, sl] = (_rope(q, cda, pda, nda, DA_QK // 2) * (DA_SCALE * LOG2E)).astype(BF16)
        kda_ref[0, :, sl] = _rope(k, cda, pda, nda, DA_QK // 2).astype(BF16)
    o = 2 * n_da
    cq = proj[:, o:o + MLA_Q_RANK]
    ckv = proj[:, o + MLA_Q_RANK:o + MLA_Q_RANK + MLA_KV_RANK]
    kr = proj[:, o + MLA_Q_RANK + MLA_KV_RANK:o + MLA_Q_RANK + MLA_KV_RANK + LANES]
    qm = _dot((_rms(cq) * qn_ref[...]).astype(BF16), qup_ref[...])
    ckvn = (_rms(ckv) * kvn_ref[...]).astype(BF16)
    kv = _dot(ckvn, kvup_ref[...])
    vmt_ref[0] = _dot_nt(kvupt_ref[...], ckvn).astype(BF16)
    rep = kr + pltpu.roll(kr, 32, 1) + pltpu.roll(kr, 64, 1) + pltpu.roll(kr, 96, 1)
    lane = lax.broadcasted_iota(jnp.int32, (1, LANES), 1)
    rep = jnp.where((lane >= MLA_NOPE) & (lane < MLA_NOPE + MLA_ROPE), rep, 0.0)
    rep = _rope(rep, cm, pm, nm, MLA_ROPE // 2)
    for hh in range(MLA_HEADS):
        sl = slice(LANES * hh, LANES * (hh + 1))
        qm_ref[0, :, sl] = (_rope(qm[:, sl], cm, pm, nm, MLA_ROPE // 2) * (MLA_SCALE * LOG2E)).astype(BF16)
        km_ref[0, :, sl] = (kv[:, sl] + rep).astype(BF16)


def _attn_project(xc, xl, mod, g, w_in, qn_g, q_up, kvn_g, kv_up, tabs_da, tabs_m):
    bsz, n_ctx, d = xc.shape
    t = n_ctx + xl.shape[1]
    tm = ROW_TILE
    n_da = DA_HEADS * LANES
    nm = MLA_HEADS * LANES
    nv = MLA_HEADS * MLA_V
    w = jnp.concatenate([w_in[:, :2 * n_da], w_in[:, 3 * n_da:]], axis=1)
    w = jnp.pad(w, ((0, 0), (0, (-w.shape[1]) % LANES))).astype(BF16)
    wvt = w_in[:, 2 * n_da:3 * n_da].T.astype(BF16)
    qup = jnp.pad(q_up.reshape(MLA_Q_RANK, MLA_HEADS, MLA_NOPE + MLA_ROPE),
                  ((0, 0), (0, 0), (0, LANES - MLA_NOPE - MLA_ROPE))).reshape(MLA_Q_RANK, -1).astype(BF16)
    kv3 = kv_up.reshape(MLA_KV_RANK, MLA_HEADS, MLA_NOPE + MLA_V)
    kvup = jnp.pad(kv3[:, :, :MLA_NOPE],
                   ((0, 0), (0, 0), (0, LANES - MLA_NOPE))).reshape(MLA_KV_RANK, -1).astype(BF16)
    kvupt = kv3[:, :, MLA_NOPE:].reshape(MLA_KV_RANK, -1).T.astype(BF16)
    row = lambda b, i: (b, i, 0)
    col = lambda b, i: (b, 0, i)
    const = lambda b, i: (0, 0)
    tab = pl.BlockSpec((tm, LANES), lambda b, i: (i, 0))
    rows_out = lambda n: (pl.BlockSpec((1, tm, n), row), jax.ShapeDtypeStruct((bsz, t, n), BF16))
    cols_out = lambda n: (pl.BlockSpec((1, n, tm), col), jax.ShapeDtypeStruct((bsz, n, t), BF16))
    outs = [rows_out(n_da), rows_out(n_da), cols_out(n_da), rows_out(nm), rows_out(nm), cols_out(nv)]
    return pl.pallas_call(
        _attn_proj_kernel,
        grid=(bsz, t // tm),
        in_specs=_stream_specs(d) + [
            _mod_spec(d),
            pl.BlockSpec((1, d), const),
            pl.BlockSpec(w.shape, const),
            pl.BlockSpec(wvt.shape, const),
            pl.BlockSpec((1, MLA_Q_RANK), const),
            pl.BlockSpec(qup.shape, const),
            pl.BlockSpec((1, MLA_KV_RANK), const),
            pl.BlockSpec(kvup.shape, const),
            pl.BlockSpec(kvupt.shape, const),
            tab, tab, tab, tab, tab, tab],
        out_specs=[o[0] for o in outs],
        out_shape=[o[1] for o in outs],
        compiler_params=_params("arbitrary", "arbitrary"),
        name="attn_project",
    )(xc, xl, mod, g.reshape(1, d), w, wvt, qn_g.reshape(1, -1), qup, kvn_g.reshape(1, -1), kvup, kvupt,
      *tabs_da, *tabs_m)


def _reduce_rows(x, fn):
    n, w = x.shape
    slab = 256
    if n > slab and n % slab == 0:
        x = fn(x.reshape(n // slab, slab, w), axis=0)
        n = slab
    if n > 32 and n % 32 == 0:
        x = fn(x.reshape(n // 32, 32, w), axis=0)
    return fn(x, axis=0, keepdims=True)


def _softmax_parts(s):
    e = jnp.exp2(s - _reduce_rows(s, jnp.max))
    return e.astype(BF16), 1.0 / _reduce_rows(e, jnp.sum)


def _attn_kernel(qda_ref, qm_ref, kda_ref, vat_ref, km_ref, vmt_ref, lam_ref, subg_ref, wout_ref,
                 x_ref, mod_ref, o_ref, *, lam_init):
    lamv = lam_ref[...]
    lam = (jnp.exp(jnp.sum(lamv[0:1] * lamv[1:2], axis=1, keepdims=True))
           - jnp.exp(jnp.sum(lamv[2:3] * lamv[3:4], axis=1, keepdims=True)) + lam_init)
    first = lax.broadcasted_iota(jnp.int32, (1, LANES), 1) < LANES // 2
    top = lax.broadcasted_iota(jnp.int32, (LANES, 1), 0) < LANES // 2

    def scores(i):
        if i < 2 * DA_HEADS:
            sl = slice(LANES * (i // 2), LANES * (i // 2 + 1))
            q = qda_ref[0, :, sl].astype(F32)
            q = jnp.where(first, q, 0.0) if i % 2 == 0 else jnp.where(first, 0.0, q)
            return _dot_nt(kda_ref[0, :, sl], q.astype(BF16))
        sl = slice(LANES * (i - 2 * DA_HEADS), LANES * (i - 2 * DA_HEADS + 1))
        return _dot_nt(km_ref[0, :, sl], qm_ref[0, :, sl])

    n_maps = 2 * DA_HEADS + MLA_HEADS
    heads = []
    s_next = scores(0)
    for i in range(n_maps):
        s = s_next
        if i + 1 < n_maps:
            s_next = scores(i + 1)
        e, r = _softmax_parts(s)
        if i < 2 * DA_HEADS:
            vt = vat_ref[0, LANES * (i // 2):LANES * (i // 2 + 1), :]
        else:
            j = (i - 2 * DA_HEADS) // 2
            vt = vmt_ref[0, LANES * j:LANES * (j + 1), :]
        heads.append(_dot(vt, e) * r)
    outs = []
    for h in range(DA_HEADS):
        oa = heads[2 * h] - heads[2 * h + 1] * lam
        oa = oa * lax.rsqrt(jnp.mean(oa * oa, axis=0, keepdims=True) + EPS)
        outs.append(oa * subg_ref[...] * (1.0 - lam_init))
    for j in range(MLA_HEADS // 2):
        outs.append(jnp.where(top, heads[2 * DA_HEADS + 2 * j], heads[2 * DA_HEADS + 2 * j + 1]))
    att = jnp.concatenate(outs, axis=0).astype(BF16)
    o_ref[0] = x_ref[0] + mod_ref[0, 0][2:3] * _dot_tn(att, wout_ref[...])


def _attention(qda, kda, vat, qm, km, vmt, lamv, subln_g, w_out, x, mod, group, q_block0, n_keys, lam_init):
    bsz, nq, d = x.shape
    tq = ROW_TILE
    row = lambda b, i: (b, i, 0)
    qrow = lambda b, i: (b, i + q_block0, 0)
    full = lambda b, i: (b, 0, 0)
    const = lambda b, i: (0, 0)
    wout = w_out.astype(BF16)
    return pl.pallas_call(
        functools.partial(_attn_kernel, lam_init=lam_init),
        grid=(bsz, nq // tq),
        in_specs=[pl.BlockSpec((1, tq, qda.shape[2]), qrow),
                  pl.BlockSpec((1, tq, qm.shape[2]), qrow),
                  pl.BlockSpec((1, n_keys, kda.shape[2]), full),
                  pl.BlockSpec((1, vat.shape[1], n_keys), full),
                  pl.BlockSpec((1, n_keys, km.shape[2]), full),
                  pl.BlockSpec((1, vmt.shape[1], n_keys), full),
                  pl.BlockSpec(lamv.shape, const),
                  pl.BlockSpec((DA_V, 1), const),
                  pl.BlockSpec(wout.shape, const),
                  pl.BlockSpec((1, tq, d), row),
                  _mod_spec(d, group)],
        out_specs=pl.BlockSpec((1, tq, d), row),
        out_shape=jax.ShapeDtypeStruct((bsz, nq, d), F32),
        compiler_params=_params("arbitrary", "arbitrary"),
        name="attention",
    )(qda, qm, kda, vat, km, vmt, lamv, subln_g.reshape(-1, 1), wout, x, mod)


def _router_kernel(x_ref, mod_ref, g_ref, wrt_ref, h_ref, lg_ref):
    mod = mod_ref[0, 0]
    h = _modulate(x_ref[0], g_ref[...], mod[3:4], mod[4:5])
    h_ref[0] = h.astype(BF16)
    lg_ref[0] = _dot3(wrt_ref[...], h, _dot_nt)


def _router(x, mod, group, g, w_router):
    bsz, n, d = x.shape
    tm = ROW_TILE
    ne = w_router.shape[1]
    return pl.pallas_call(
        _router_kernel,
        grid=(bsz, n // tm),
        in_specs=[pl.BlockSpec((1, tm, d), lambda b, i: (b, i, 0)),
                  _mod_spec(d, group),
                  pl.BlockSpec((1, d), lambda b, i: (0, 0)),
                  pl.BlockSpec((ne, d), lambda b, i: (0, 0))],
        out_specs=[pl.BlockSpec((1, tm, d), lambda b, i: (b, i, 0)),
                   pl.BlockSpec((1, ne, tm), lambda b, i: (b, 0, i))],
        out_shape=[jax.ShapeDtypeStruct((bsz, n, d), BF16),
                   jax.ShapeDtypeStruct((bsz, ne, n), F32)],
        compiler_params=_params("arbitrary", "arbitrary"),
        name="moe_router",
    )(x, mod, g.reshape(1, d), w_router.T)


def _kth_largest(a, k):
    rows = a.shape[0]
    lo0 = jnp.full((rows, 1), -1.0, F32)
    hi0 = jnp.max(a, axis=1, keepdims=True)

    def cond(c):
        return c[2] > 0

    def body(c):
        lo, hi, _ = c
        mid = 0.5 * (lo + hi)
        open_ = (mid > lo) & (mid < hi)
        cnt = jnp.sum(jnp.where(a > mid, 1.0, 0.0), axis=1, keepdims=True)
        up = open_ & (cnt >= k)
        down = open_ & (cnt < k)
        lo = jnp.where(up, mid, lo)
        hi = jnp.where(down, mid, hi)
        return lo, hi, jnp.sum(jnp.where(open_, 1.0, 0.0))

    _, hi, _ = lax.while_loop(cond, body, (lo0, hi0, jnp.float32(1.0)))
    return hi


def _select_kernel(lg_ref, slot_ref, aff_ref, tri_ref, *, ne, cap):
    rows, n = lg_ref.shape
    rt = 256
    for r in range(0, n, rt):
        ri = lax.broadcasted_iota(jnp.int32, (rt, n), 0) + r
        ci = lax.broadcasted_iota(jnp.int32, (rt, n), 1)
        tri_ref[r:r + rt, :] = jnp.where(ri < ci, 1.0, 0.0).astype(BF16)

    lg = lg_ref[...].reshape(rows // ne, ne, n)
    e = jnp.exp(lg - jnp.max(lg, axis=1, keepdims=True))
    aff = (e / jnp.sum(e, axis=1, keepdims=True)).reshape(rows, n)
    aff_ref[...] = aff
    thr = _kth_largest(aff, cap)
    gt = aff > thr
    eq = aff == thr
    need = cap - jnp.sum(jnp.where(gt, 1.0, 0.0), axis=1, keepdims=True)
    tri = tri_ref[...]
    before = _dot(jnp.where(eq, 1.0, 0.0).astype(BF16), tri)
    sel = gt | (eq & (before < need))
    pos = _dot(jnp.where(sel, 1.0, 0.0).astype(BF16), tri)
    slot_ref[...] = jnp.where(sel, pos, -1.0).astype(jnp.int32)


def _select(logits, cap):
    bsz, ne, n = logits.shape
    blk = pl.BlockSpec((bsz * ne, n), lambda i: (0, 0))
    slot, aff = pl.pallas_call(
        functools.partial(_select_kernel, ne=ne, cap=cap),
        grid=(1,),
        in_specs=[blk],
        out_specs=[blk, blk],
        out_shape=[jax.ShapeDtypeStruct((bsz * ne, n), jnp.int32),
                   jax.ShapeDtypeStruct((bsz * ne, n), F32)],
        scratch_shapes=[pltpu.VMEM((n, n), BF16)],
        compiler_params=_params("arbitrary"),
        name="moe_select",
    )(logits.reshape(bsz * ne, n))
    return slot.reshape(bsz, ne, n), aff.reshape(bsz, ne, n)


GATHER_EXPERTS = 8


def _gather_kernel(h_ref, slot_ref, aff_ref, xe_ref, gate_ref, onehot_ref, *, cap):
    n = h_ref.shape[1]
    pi = lax.broadcasted_iota(jnp.int32, (cap, n), 0)
    for e in range(GATHER_EXPERTS):
        hit = slot_ref[0, e:e + 1, :] == pi
        onehot_ref[e * cap:(e + 1) * cap, :] = jnp.where(hit, 1.0, 0.0).astype(BF16)
        gate_ref[e] = jnp.sum(jnp.where(hit, aff_ref[0, e:e + 1, :], 0.0), axis=1, keepdims=True)
    xe = _dot(onehot_ref[...], h_ref[0]).astype(BF16)
    xe_ref[...] = xe.reshape(GATHER_EXPERTS, cap, xe.shape[1])


def _gather(h, slot, aff, cap):
    bsz, n, d = h.shape
    ne = slot.shape[1]
    ge = GATHER_EXPERTS
    return pl.pallas_call(
        functools.partial(_gather_kernel, cap=cap),
        grid=(bsz, ne // ge),
        in_specs=[pl.BlockSpec((1, n, d), lambda b, j: (b, 0, 0)),
                  pl.BlockSpec((1, ge, n), lambda b, j: (b, j, 0)),
                  pl.BlockSpec((1, ge, n), lambda b, j: (b, j, 0))],
        out_specs=[pl.BlockSpec((ge, cap, d), lambda b, j: (j, b, 0)),
                   pl.BlockSpec((ge, cap, 1), lambda b, j: (j, b, 0))],
        out_shape=[jax.ShapeDtypeStruct((ne, bsz * cap, d), BF16),
                   jax.ShapeDtypeStruct((ne, bsz * cap, 1), F32)],
        scratch_shapes=[pltpu.VMEM((ge * cap, n), BF16)],
        compiler_params=_params("arbitrary", "arbitrary"),
        name="moe_gather",
    )(h, slot, aff)


def _ffn_kernel(*refs, nseg, rows_per_pass):
    x_refs = refs[:nseg]
    gate_refs = refs[nseg:2 * nseg]
    wg_ref, wu_ref, wd_ref = refs[2 * nseg:2 * nseg + 3]
    y_refs = refs[2 * nseg + 3:3 * nseg + 3]
    acc_refs = refs[3 * nseg + 3:]
    f = pl.program_id(1)

    @pl.when(f == 0)
    def _():
        for acc_ref in acc_refs:
            acc_ref[...] = jnp.zeros_like(acc_ref)

    wg = wg_ref[0].astype(BF16)
    wu = wu_ref[0].astype(BF16)
    wd = wd_ref[0].astype(BF16)
    for x_ref, acc_ref in zip(x_refs, acc_refs):
        m = x_ref.shape[1]
        step = min(rows_per_pass, m)
        for r in range(0, m, step):
            x = x_ref[0, r:r + step, :]
            hid = (_silu(_dot(x, wg)) * _dot(x, wu)).astype(BF16)
            acc_ref[r:r + step, :] += _dot(hid, wd)

    @pl.when(f == pl.num_programs(1) - 1)
    def _():
        for y_ref, gate_ref, acc_ref in zip(y_refs, gate_refs, acc_refs):
            y_ref[0] = (acc_ref[...] * gate_ref[0]).astype(BF16)


def _ffn(xes, gates, w_gate, w_up, w_down, layer):
    _, ne, d, ff = w_gate.shape
    tf = 256
    nseg = len(xes)
    return pl.pallas_call(
        functools.partial(_ffn_kernel, nseg=nseg, rows_per_pass=512),
        grid=(ne, ff // tf),
        in_specs=[pl.BlockSpec((1, xe.shape[1], d), lambda e, f: (e, 0, 0)) for xe in xes]
        + [pl.BlockSpec((1, xe.shape[1], 1), lambda e, f: (e, 0, 0)) for xe in xes]
        + [pl.BlockSpec((None, 1, d, tf), lambda e, f: (layer, e, 0, f)),
           pl.BlockSpec((None, 1, d, tf), lambda e, f: (layer, e, 0, f)),
           pl.BlockSpec((None, 1, tf, d), lambda e, f: (layer, e, f, 0))],
        out_specs=[pl.BlockSpec((1, xe.shape[1], d), lambda e, f: (e, 0, 0)) for xe in xes],
        out_shape=[jax.ShapeDtypeStruct(xe.shape, BF16) for xe in xes],
        scratch_shapes=[pltpu.VMEM(xe.shape[1:], F32) for xe in xes],
        compiler_params=_params("arbitrary", "arbitrary"),
        name="moe_ffn",
    )(*xes, *gates, w_gate, w_up, w_down)


def _combine_kernel(*refs, cap, final_norm):
    slot_ref, ye_ref, x_ref, mod_ref = refs[:4]
    if final_norm:
        fg_ref, o_ref, onehot_ref = refs[4:]
    else:
        o_ref, onehot_ref = refs[4:]
    ne = ye_ref.shape[0]
    if cap % LANES == 0:
        pi = lax.broadcasted_iota(jnp.int32, (1, cap), 1)
        for e in range(ne):
            hit = slot_ref[0, :, e:e + 1] == pi
            onehot_ref[:, e * cap:(e + 1) * cap] = jnp.where(hit, 1.0, 0.0).astype(BF16)
    else:
        ei = lax.broadcasted_iota(jnp.int32, (ne, ne * cap), 0)
        li = lax.broadcasted_iota(jnp.int32, (ne, ne * cap), 1)
        spread = jnp.where((li >= ei * cap) & (li < (ei + 1) * cap), 1.0, 0.0).astype(BF16)
        slots = _dot(slot_ref[0].astype(F32).astype(BF16), spread)
        group = jnp.sum(jnp.where(li >= (ei + 1) * cap, 1, 0), axis=0, keepdims=True)
        pos = (li[0:1] - group * cap).astype(F32)
        onehot_ref[...] = jnp.where(slots == pos, 1.0, 0.0).astype(BF16)
    ye = ye_ref[...]
    moe = _dot(onehot_ref[...], ye.reshape(ne * cap, ye.shape[2]))
    y = x_ref[0] + mod_ref[0, 0][5:6] * moe
    if final_norm:
        y = _rms(y) * fg_ref[...]
    o_ref[0] = y


def _combine(slot_t, ye, x, mod, group, cap, final_g):
    bsz, n, d = x.shape
    ne = slot_t.shape[2]
    tm = min(n, 1024)
    final_norm = final_g is not None
    in_specs = [pl.BlockSpec((1, tm, ne), lambda b, i: (b, i, 0)),
                pl.BlockSpec((ne, cap, d), lambda b, i: (0, b, 0)),
                pl.BlockSpec((1, tm, d), lambda b, i: (b, i, 0)),
                _mod_spec(d, group)]
    args = [slot_t, ye, x, mod]
    if final_norm:
        in_specs.append(pl.BlockSpec((1, d), lambda b, i: (0, 0)))
        args.append(final_g.reshape(1, d))
    return pl.pallas_call(
        functools.partial(_combine_kernel, cap=cap, final_norm=final_norm),
        grid=(bsz, n // tm),
        in_specs=in_specs,
        out_specs=pl.BlockSpec((1, tm, d), lambda b, i: (b, i, 0)),
        out_shape=jax.ShapeDtypeStruct((bsz, n, d), F32),
        scratch_shapes=[pltpu.VMEM((tm, ne * cap), BF16)],
        compiler_params=_params("arbitrary", "arbitrary"),
        name="moe_combine",
    )(*args)


def _moe(xs, groups, mod, g, w_router, w_gate, w_up, w_down, layer, final_g=None):
    ne = w_router.shape[1]
    routed = []
    for x, group in zip(xs, groups):
        cap = EC_CAPACITY_FACTOR * x.shape[1] // ne
        h, logits = _router(x, mod, group, g, w_router)
        slot, aff = _select(logits, cap)
        xe, gate = _gather(h, slot, aff, cap)
        routed.append((cap, slot, xe, gate))
    yes = _ffn([r[2] for r in routed], [r[3] for r in routed], w_gate, w_up, w_down, layer)
    return [_combine(jnp.swapaxes(slot, 1, 2), ye, x, mod, group, cap, final_g)
            for x, group, ye, (cap, slot, _, _) in zip(xs, groups, yes, routed)]


def _rec_proj_kernel(xc_ref, xl_ref, mod_ref, g_ref, w_ref, gkup_ref, gkb_ref, lbl_ref,
                     gq_ref, gk_ref, gv_ref, gla_ref, gg_ref, hq_ref, hf_ref, hv_ref, hg_ref, *, layer):
    mod = mod_ref[0, 0]
    h = _modulate(_stream_block(xc_ref, xl_ref), g_ref[...], mod[0:1], mod[1:2]).astype(BF16)
    proj = _dot(h, w_ref[...])
    n = GLA_HEADS * LANES
    gq_ref[0] = proj[:, 0:n] * GLA_DK ** -0.5
    gk_ref[0] = proj[:, n:2 * n]
    gv_ref[0] = proj[:, 2 * n:3 * n].astype(BF16)
    gg_ref[0] = proj[:, 3 * n:4 * n]
    hq_ref[0] = _silu(proj[:, 4 * n:5 * n])
    hv_ref[0] = proj[:, 7 * n:8 * n].astype(BF16)
    hg_ref[0] = proj[:, 8 * n:9 * n]
    gdown = proj[:, 9 * n:9 * n + LANES]
    gate = _dot(gdown.astype(BF16), gkup_ref[...]) + gkb_ref[...]
    lbl = lbl_ref[...]
    e = jnp.exp(lbl - jnp.max(lbl, axis=0, keepdims=True))
    sm = e / jnp.sum(e, axis=0, keepdims=True)
    lb = jnp.sum(sm[0:layer + 1], axis=0, keepdims=True) - sm[0:1]
    f = lb + (1.0 - lb) * jax.nn.sigmoid(proj[:, 5 * n:7 * n])
    for dd in range(2):
        gla_ref[dd, 0] = jax.nn.log_sigmoid(gate[:, dd * n:(dd + 1) * n]) / GLA_GATE_NORM
        hf_ref[dd, 0] = f[:, dd * n:(dd + 1) * n]


def _pad_heads(w, heads, dk):
    lead = w.shape[:-1]
    w3 = w.reshape(lead + (heads, dk))
    return jnp.pad(w3, [(0, 0)] * len(lead) + [(0, 0), (0, LANES - dk)]).reshape(lead + (heads * LANES,))


def _rec_project(xc, xl, mod, g, w_in, gk_up, gk_bias, lb_logits, layer):
    bsz, n_ctx, d = xc.shape
    t = n_ctx + xl.shape[1]
    tm = ROW_TILE
    hk = GLA_HEADS * GLA_DK
    hv = GLA_HEADS * GLA_DV
    hf = HG_HEADS * HG_DF
    sizes = (hk, hk, hv, hv, 2 * GLA_GATE_RANK, hf, 2 * hf, HG_HEADS * HG_DV, HG_HEADS * HG_DV)
    offs = [0]
    for s in sizes:
        offs.append(offs[-1] + s)
    gq, gk, gv, gg, gdown, hq, hff, hi, hgt = [w_in[:, offs[i]:offs[i + 1]] for i in range(9)]
    w = jnp.concatenate([_pad_heads(gq, GLA_HEADS, GLA_DK), _pad_heads(gk, GLA_HEADS, GLA_DK), gv, gg,
                         hq, hff, hi, hgt, jnp.pad(gdown, ((0, 0), (0, LANES - 2 * GLA_GATE_RANK)))],
                        axis=1).astype(BF16)
    n = GLA_HEADS * LANES
    ups = [_pad_heads(gk_up[dd], GLA_HEADS, GLA_DK) for dd in range(2)]
    zero = jnp.zeros_like(ups[0])
    gkup = jnp.concatenate([jnp.concatenate([ups[0], zero], axis=1), jnp.concatenate([zero, ups[1]], axis=1),
                            jnp.zeros((LANES - 2 * GLA_GATE_RANK, 2 * n), F32)], axis=0).astype(BF16)
    gkb = jnp.concatenate([_pad_heads(gk_bias[dd], GLA_HEADS, GLA_DK) for dd in range(2)]).reshape(1, 2 * n)
    row = lambda b, i: (b, i, 0)
    drow = lambda b, i: (0, b, i, 0)
    const = lambda b, i: (0, 0)
    one = lambda dt: (pl.BlockSpec((1, tm, n), row), jax.ShapeDtypeStruct((bsz, t, n), dt))
    two = lambda dt: (pl.BlockSpec((2, 1, tm, n), drow), jax.ShapeDtypeStruct((2, bsz, t, n), dt))
    outs = [one(F32), one(F32), one(BF16), two(F32), one(F32), one(F32), two(F32), one(BF16), one(F32)]
    return pl.pallas_call(
        functools.partial(_rec_proj_kernel, layer=layer),
        grid=(bsz, t // tm),
        in_specs=_stream_specs(d) + [
            _mod_spec(d),
            pl.BlockSpec((1, d), const),
            pl.BlockSpec(w.shape, const),
            pl.BlockSpec(gkup.shape, const),
            pl.BlockSpec(gkb.shape, const),
            pl.BlockSpec((lb_logits.shape[0], 2 * hf), const)],
        out_specs=[o[0] for o in outs],
        out_shape=[o[1] for o in outs],
        compiler_params=_params("arbitrary", "arbitrary"),
        name="rec_project",
    )(xc, xl, mod, g.reshape(1, d), w, gkup, gkb, lb_logits.reshape(lb_logits.shape[0], 2 * hf))


def _scan_kernel(q_ref, k_ref, v_ref, la_ref, o_ref, st_ref, *, heads, from_forget):
    tm, n = q_ref.shape[1], q_ref.shape[2]
    nc = tm // CHUNK

    @pl.when(pl.program_id(2) == 0)
    def _():
        st_ref[...] = jnp.zeros_like(st_ref)

    def run(direction):
        ri = lax.broadcasted_iota(jnp.int32, (tm, tm), 0)
        ci = lax.broadcasted_iota(jnp.int32, (tm, tm), 1)
        same = jnp.right_shift(ri, 6) == jnp.right_shift(ci, 6)
        walked = (ci <= ri) if direction == 0 else (ci >= ri)
        causal = same & walked
        cum = jnp.where(causal, 1.0, 0.0).astype(BF16)
        q = q_ref[0]
        if from_forget:
            fg = la_ref[0, 0]
            k = 1.0 - fg
            la = jnp.log(fg)
        else:
            k = k_ref[0]
            la = la_ref[0, 0]
        l1, l2, l3 = _split3(la)
        b = (_dot(cum, l1) + (_dot(cum, l2) + _dot(cum, l3))).reshape(nc, CHUNK, n)
        last, mid = (CHUNK - 1, CHUNK // 2) if direction == 0 else (0, CHUNK // 2 - 1)
        b_tot = b[:, last:last + 1]
        b_mid = b[:, mid:mid + 1]
        q = q.reshape(nc, CHUNK, n)
        k = k.reshape(nc, CHUNK, n)
        q_in = (q * jnp.exp(b)).astype(BF16).reshape(tm, n)
        q_mid = (q * jnp.exp(b - b_mid)).astype(BF16).reshape(tm, n)
        k_mid = (k * jnp.exp(b_mid - b)).astype(BF16).reshape(tm, n)
        k_out = (k * jnp.exp(b_tot - b)).astype(BF16).reshape(tm, n)
        decay = jnp.exp(b_tot)
        order = range(nc) if direction == 0 else range(nc - 1, -1, -1)
        for h in range(heads):
            sl = slice(LANES * h, LANES * (h + 1))
            v = v_ref[0, :, sl]
            scores = jnp.where(causal, _dot_nt(q_mid[:, sl], k_mid[:, sl]), 0.0)
            intra = _dot(scores.astype(BF16), v)
            st = st_ref[h]
            for c in order:
                rows = slice(c * CHUNK, (c + 1) * CHUNK)
                o_ref[0, 0, rows, sl] = intra[rows] + _dot_nt(q_in[rows, sl], st.astype(BF16))
                st = st * decay[c, :, sl] + _dot_tn(v[rows], k_out[rows, sl])
            st_ref[h] = st

    for direction in range(2):
        @pl.when(pl.program_id(1) == direction)
        def _(direction=direction):
            run(direction)


def _scan(q, k, v, la, n_ctx, from_forget):
    bsz, t, n = q.shape
    heads = n // LANES
    tm = ROW_TILE
    nb = t // tm
    ncb = n_ctx // tm

    def blk(d, s):
        back = jnp.where(s < ncb, ncb - 1 - s, nb - 1 - (s - ncb))
        return jnp.where(d == 0, s, back)

    row = lambda b, d, s: (b, blk(d, s), 0)
    drow = lambda b, d, s: (d, b, blk(d, s), 0)
    if k is None:
        k = q
    return pl.pallas_call(
        functools.partial(_scan_kernel, heads=heads, from_forget=from_forget),
        grid=(bsz, 2, nb),
        in_specs=[pl.BlockSpec((1, tm, n), row),
                  pl.BlockSpec((1, tm, n), row),
                  pl.BlockSpec((1, tm, n), row),
                  pl.BlockSpec((1, 1, tm, n), drow)],
        out_specs=pl.BlockSpec((1, 1, tm, n), drow),
        out_shape=jax.ShapeDtypeStruct((2, bsz, t, n), F32),
        scratch_shapes=[pltpu.VMEM((heads, LANES, LANES), F32)],
        compiler_params=_params("arbitrary", "arbitrary", "arbitrary"),
        name="rec_scan",
    )(q, k, v, la)


def _rec_out_kernel(og_ref, oh_ref, gg_ref, hg_ref, gn_ref, hn_ref, wout_ref, x_ref, mod_ref, o_ref):
    parts = []
    for o2_ref, gate_ref, n_ref in ((og_ref, gg_ref, gn_ref), (oh_ref, hg_ref, hn_ref)):
        for h in range(o2_ref.shape[3] // LANES):
            sl = slice(LANES * h, LANES * (h + 1))
            o = o2_ref[0, 0, :, sl] + o2_ref[1, 0, :, sl]
            parts.append(_rms(o) * n_ref[...] * _silu(gate_ref[0, :, sl]))
    a = jnp.concatenate(parts, axis=1).astype(BF16)
    o_ref[0] = x_ref[0] + mod_ref[0, 0][2:3] * _dot(a, wout_ref[...])


def _rec_readout(og, oh, gg, hg, gn, hn, w_out, xl, mod, n_ctx):
    bsz, seq, d = xl.shape
    tm = ROW_TILE
    ncb = n_ctx // tm
    n = og.shape[3]
    row = lambda b, i: (b, i + ncb, 0)
    drow = lambda b, i: (0, b, i + ncb, 0)
    const = lambda b, i: (0, 0)
    wout = w_out.astype(BF16)
    return pl.pallas_call(
        _rec_out_kernel,
        grid=(bsz, seq // tm),
        in_specs=[pl.BlockSpec((2, 1, tm, n), drow),
                  pl.BlockSpec((2, 1, tm, n), drow),
                  pl.BlockSpec((1, tm, n), row),
                  pl.BlockSpec((1, tm, n), row),
                  pl.BlockSpec((1, LANES), const),
                  pl.BlockSpec((1, LANES), const),
                  pl.BlockSpec(wout.shape, const),
                  pl.BlockSpec((1, tm, d), lambda b, i: (b, i, 0)),
                  _mod_spec(d, 1)],
        out_specs=pl.BlockSpec((1, tm, d), lambda b, i: (b, i, 0)),
        out_shape=jax.ShapeDtypeStruct((bsz, seq, d), F32),
        compiler_params=_params("arbitrary", "arbitrary"),
        name="rec_readout",
    )(og, oh, gg, hg, gn.reshape(1, -1), hn.reshape(1, -1), wout, xl, mod)


def kernel(x, c, ctx, c_ctx, ada_w, ada_b, norm1_g, norm2_g, att_w_in, mla_q_norm_g, mla_q_up, mla_kv_norm_g, mla_kv_up, da_lam_q1, da_lam_k1, da_lam_q2, da_lam_k2, da_subln_g, att_w_out, rec_w_in, gla_gk_up, gla_gk_bias, gla_norm_g, hg_lb_logits, hg_norm_g, rec_w_out, moe_router, moe_w_gate, moe_w_up, moe_w_down, final_norm_g):
    bsz, seq, d = x.shape
    n_ctx = ctx.shape[1]
    depth = ada_w.shape[0]
    assert depth == 2 and n_ctx == ROW_TILE and seq % ROW_TILE == 0 and bsz + 1 <= 16

    cvec = jnp.concatenate([c, c_ctx[None, :], jnp.zeros((16 - bsz - 1, d), F32)], axis=0)
    ada = _ada_vectors(cvec, ada_w, ada_b).reshape(depth, 16, 6, d)
    ada = jnp.pad(ada, ((0, 0), (0, 0), (0, 2), (0, 0)))
    mods = jnp.stack([jnp.broadcast_to(ada[:, bsz][:, None], (depth, bsz, 8, d)), ada[:, :bsz]], axis=2)
    xc, xl = ctx, x

    mod = mods[0]
    tabs_da = _rope_tables(n_ctx, seq, DA_QK, 0, DA_QK)
    tabs_m = _rope_tables(n_ctx, seq, MLA_ROPE, MLA_NOPE, LANES)
    proj = _attn_project(xc, xl, mod, norm1_g[0], att_w_in[0], mla_q_norm_g[0], mla_q_up[0],
                         mla_kv_norm_g[0], mla_kv_up[0], tabs_da, tabs_m)
    lamv = jnp.stack([da_lam_q1[0], da_lam_k1[0], da_lam_q2[0], da_lam_k2[0]])
    lam_init = 0.8 - 0.6 * math.exp(-0.3 * 0)
    xc = _attention(*proj, lamv, da_subln_g[0], att_w_out[0], xc, mod, 0, 0, n_ctx, lam_init)
    xl = _attention(*proj, lamv, da_subln_g[0], att_w_out[0], xl, mod, 1, n_ctx // ROW_TILE, n_ctx + seq,
                    lam_init)
    xc, xl = _moe([xc, xl], [0, 1], mod, norm2_g[0], moe_router[0], moe_w_gate, moe_w_up, moe_w_down, 0)

    mod = mods[1]
    gq, gk, gv, gla, gg, hq, hf, hv, hg = _rec_project(xc, xl, mod, norm1_g[1], rec_w_in[0], gla_gk_up[0],
                                                       gla_gk_bias[0], hg_lb_logits, 1)
    og = _scan(gq, gk, gv, gla, n_ctx, False)
    oh = _scan(hq, None, hv, hf, n_ctx, True)
    xl = _rec_readout(og, oh, gg, hg, gla_norm_g[0], hg_norm_g[0], rec_w_out[0], xl, mod, n_ctx)
    (out,) = _moe([xl], [1], mod, norm2_g[1], moe_router[1], moe_w_gate, moe_w_up, moe_w_down, 1,
                  final_g=final_norm_g)
    return out
```

```python
import functools
import math

import jax
import jax.numpy as jnp
from jax import lax
from jax.experimental import pallas as pl
from jax.experimental.pallas import tpu as pltpu

F32 = jnp.float32
BF16 = jnp.bfloat16

EPS = 1e-6
GRID_W = 64
ROPE_BASE = 10000.0

DA_HEADS = 4
DA_QK = 64
DA_V = 2 * DA_QK
DA_SCALE = DA_QK ** -0.5
MLA_HEADS = 8
MLA_NOPE = 64
MLA_ROPE = 32
MLA_V = 64
MLA_Q_RANK = 384
MLA_KV_RANK = 256
MLA_SCALE = (MLA_NOPE + MLA_ROPE) ** -0.5
LOG2E = math.log2(math.e)

GLA_HEADS = 4
GLA_DK = 64
GLA_DV = 128
GLA_GATE_RANK = 16
GLA_GATE_NORM = 16.0
HG_HEADS = 4
HG_DF = 128
HG_DV = 128
CHUNK = 64

N_EXPERTS = 16
EC_CAPACITY_FACTOR = 2

LANES = 128
ROW_TILE = 256
ATTN_KEY_BLOCK = 256
ATTN_LOOKAHEAD = 4
VMEM_LIMIT = 56 * 1024 * 1024


def _params(*sem):
    return pltpu.CompilerParams(dimension_semantics=sem, vmem_limit_bytes=VMEM_LIMIT)


def _dot(a, b):
    return jnp.dot(a, b, preferred_element_type=F32)


def _dot_nt(a, b):
    return lax.dot_general(a, b, (((1,), (1,)), ((), ())), preferred_element_type=F32)


def _dot_tn(a, b):
    return lax.dot_general(a, b, (((0,), (0,)), ((), ())), preferred_element_type=F32)


def _split2(a):
    hi = a.astype(BF16)
    lo = (a - hi.astype(F32)).astype(BF16)
    return hi, lo


def _split3(a):
    hi = a.astype(BF16)
    r = a - hi.astype(F32)
    mid = r.astype(BF16)
    lo = (r - mid.astype(F32)).astype(BF16)
    return hi, mid, lo


def _dot3(a, b, dot=_dot):
    ah, al = _split2(a)
    bh, bl = _split2(b)
    return dot(ah, bh) + (dot(ah, bl) + dot(al, bh))


def _rms(x):
    return x * lax.rsqrt(jnp.mean(x * x, axis=-1, keepdims=True) + EPS)


def _silu(x):
    return x * jax.nn.sigmoid(x)


def _modulate(x, g, shift, scale):
    return _rms(x) * g * (1.0 + scale) + shift


def _stream_specs(d):
    return [pl.BlockSpec((1, ROW_TILE, d), lambda b, i: (b, 0, 0)),
            pl.BlockSpec((1, ROW_TILE, d), lambda b, i: (b, jnp.maximum(i - 1, 0), 0))]


def _stream_block(xc_ref, xl_ref):
    return jnp.where(pl.program_id(1) == 0, xc_ref[0], xl_ref[0])


def _route_rows(y, mod, g2_ref, wrt_ref, h_ref, lg_ref):
    h = _modulate(y, g2_ref[...], mod[3:4], mod[4:5])
    h_ref[0] = h.astype(BF16)
    lg_ref[0] = _dot3(wrt_ref[...], h, _dot_nt)


def _route_specs(bsz, n, d, ne, tm):
    const = lambda b, i: (0, 0)
    return ([pl.BlockSpec((1, d), const), pl.BlockSpec((ne, d), const)],
            [pl.BlockSpec((1, tm, d), lambda b, i: (b, i, 0)), pl.BlockSpec((1, ne, tm), lambda b, i: (b, 0, i))],
            [jax.ShapeDtypeStruct((bsz, n, d), BF16), jax.ShapeDtypeStruct((bsz, ne, n), F32)])


def _mod_spec(d, group=None):
    if group is None:
        return pl.BlockSpec((1, 1, 8, d), lambda b, i: (b, jnp.minimum(i, 1), 0, 0))
    return pl.BlockSpec((1, 1, 8, d), lambda b, i: (b, group, 0, 0))


def _ada_kernel(c_ref, w_ref, b_ref, o_ref):
    o_ref[...] = _dot3(_silu(c_ref[...]), w_ref[...]) + b_ref[...]


def _ada_vectors(cvec, ada_w, ada_b):
    depth, d, n6 = ada_w.shape
    tn = n6 // 4
    rows = cvec.shape[0]
    return pl.pallas_call(
        _ada_kernel,
        grid=(depth, n6 // tn),
        in_specs=[pl.BlockSpec((rows, d), lambda l, j: (0, 0)),
                  pl.BlockSpec((None, d, tn), lambda l, j: (l, 0, j)),
                  pl.BlockSpec((None, 1, tn), lambda l, j: (l, 0, j))],
        out_specs=pl.BlockSpec((None, rows, tn), lambda l, j: (l, 0, j)),
        out_shape=jax.ShapeDtypeStruct((depth, rows, n6), F32),
        compiler_params=_params("arbitrary", "arbitrary"),
        name="ada_vectors",
    )(cvec, ada_w, ada_b.reshape(depth, 1, n6))


def _axial_angles(rows, rot_dim):
    row = jnp.repeat(jnp.arange(rows, dtype=F32), GRID_W)
    col = jnp.tile(jnp.arange(GRID_W, dtype=F32), rows)
    n_freq = rot_dim // 4
    inv = ROPE_BASE ** (-jnp.arange(n_freq, dtype=F32) / n_freq)
    ang = jnp.concatenate([row[:, None] * inv, col[:, None] * inv], axis=-1)
    return jnp.cos(ang), jnp.sin(ang)


def _rope_tables(n_ctx, seq, width, start, group):
    half = width // 2
    cos, sin = _axial_angles(seq // GRID_W, width)
    lane = jnp.arange(LANES) % group - start
    inside = (lane >= 0) & (lane < width)
    idx = jnp.clip(lane, 0, width - 1) % half
    second = inside & (lane >= half)
    firsth = inside & (lane < half)
    c = jnp.where(inside[None, :], cos[:, idx], 1.0)
    s_prev = jnp.where(second[None, :], sin[:, idx], 0.0)
    s_next = jnp.where(firsth[None, :], -sin[:, idx], 0.0)
    ident = jnp.ones((n_ctx, LANES), F32)
    zero = jnp.zeros((n_ctx, LANES), F32)
    return (jnp.concatenate([ident, c], 0), jnp.concatenate([zero, s_prev], 0),
            jnp.concatenate([zero, s_next], 0))


def _rope(x, cos, s_prev, s_next, half):
    return x * cos + pltpu.roll(x, half, 1) * s_prev + pltpu.roll(x, LANES - half, 1) * s_next


def _attn_proj_kernel(xc_ref, xl_ref, mod_ref, g_ref, w_ref, wvt_ref, qn_ref, qup_ref, kvn_ref, kvup_ref,
                      kvupt_ref, cda_ref, pda_ref, nda_ref, cm_ref, pm_ref, nm_ref,
                      qda_ref, kda_ref, vat_ref, qm_ref, km_ref, vmt_ref):
    mod = mod_ref[0, 0]
    h = _modulate(_stream_block(xc_ref, xl_ref), g_ref[...], mod[0:1], mod[1:2]).astype(BF16)
    proj = _dot(h, w_ref[...])
    vat_ref[0] = _dot_nt(wvt_ref[...], h).astype(BF16)
    cda, pda, nda = cda_ref[...], pda_ref[...], nda_ref[...]
    cm, pm, nm = cm_ref[...], pm_ref[...], nm_ref[...]
    n_da = DA_HEADS * LANES
    for hh in range(DA_HEADS):
        sl = slice(LANES * hh, LANES * (hh + 1))
        q = proj[:, LANES * hh:LANES * (hh + 1)]
        k = proj[:, n_da + LANES * hh:n_da + LANES * (hh + 1)]
        qda_ref[0, :, sl] = (_rope(q, cda, pda, nda, DA_QK // 2) * (DA_SCALE * LOG2E)).astype(BF16)
        kda_ref[0, :, sl] = _rope(k, cda, pda, nda, DA_QK // 2).astype(BF16)
    o = 2 * n_da
    cq = proj[:, o:o + MLA_Q_RANK]
    ckv = proj[:, o + MLA_Q_RANK:o + MLA_Q_RANK + MLA_KV_RANK]
    kr = proj[:, o + MLA_Q_RANK + MLA_KV_RANK:o + MLA_Q_RANK + MLA_KV_RANK + LANES]
    qm = _dot((_rms(cq) * qn_ref[...]).astype(BF16), qup_ref[...])
    ckvn = (_rms(ckv) * kvn_ref[...]).astype(BF16)
    kv = _dot(ckvn, kvup_ref[...])
    vmt_ref[0] = _dot_nt(kvupt_ref[...], ckvn).astype(BF16)
    rep = kr + pltpu.roll(kr, 32, 1) + pltpu.roll(kr, 64, 1) + pltpu.roll(kr, 96, 1)
    lane = lax.broadcasted_iota(jnp.int32, (1, LANES), 1)
    rep = jnp.where((lane >= MLA_NOPE) & (lane < MLA_NOPE + MLA_ROPE), rep, 0.0)
    rep = _rope(rep, cm, pm, nm, MLA_ROPE // 2)
    for hh in range(MLA_HEADS):
        sl = slice(LANES * hh, LANES * (hh + 1))
        qm_ref[0, :, sl] = (_rope(qm[:, sl], cm, pm, nm, MLA_ROPE // 2) * (MLA_SCALE * LOG2E)).astype(BF16)
        km_ref[0, :, sl] = (kv[:, sl] + rep).astype(BF16)


def _attn_project(xc, xl, mod, g, w_in, qn_g, q_up, kvn_g, kv_up, tabs_da, tabs_m):
    bsz, n_ctx, d = xc.shape
    t = n_ctx + xl.shape[1]
    tm = ROW_TILE
    n_da = DA_HEADS * LANES
    nm = MLA_HEADS * LANES
    nv = MLA_HEADS * MLA_V
    w = jnp.concatenate([w_in[:, :2 * n_da], w_in[:, 3 * n_da:]], axis=1)
    w = jnp.pad(w, ((0, 0), (0, (-w.shape[1]) % LANES))).astype(BF16)
    wvt = w_in[:, 2 * n_da:3 * n_da].T.astype(BF16)
    qup = jnp.pad(q_up.reshape(MLA_Q_RANK, MLA_HEADS, MLA_NOPE + MLA_ROPE),
                  ((0, 0), (0, 0), (0, LANES - MLA_NOPE - MLA_ROPE))).reshape(MLA_Q_RANK, -1).astype(BF16)
    kv3 = kv_up.reshape(MLA_KV_RANK, MLA_HEADS, MLA_NOPE + MLA_V)
    kvup = jnp.pad(kv3[:, :, :MLA_NOPE],
                   ((0, 0), (0, 0), (0, LANES - MLA_NOPE))).reshape(MLA_KV_RANK, -1).astype(BF16)
    kvupt = kv3[:, :, MLA_NOPE:].reshape(MLA_KV_RANK, -1).T.astype(BF16)
    row = lambda b, i: (b, i, 0)
    col = lambda b, i: (b, 0, i)
    const = lambda b, i: (0, 0)
    tab = pl.BlockSpec((tm, LANES), lambda b, i: (i, 0))
    rows_out = lambda n: (pl.BlockSpec((1, tm, n), row), jax.ShapeDtypeStruct((bsz, t, n), BF16))
    cols_out = lambda n: (pl.BlockSpec((1, n, tm), col), jax.ShapeDtypeStruct((bsz, n, t), BF16))
    outs = [rows_out(n_da), rows_out(n_da), cols_out(n_da), rows_out(nm), rows_out(nm), cols_out(nv)]
    return pl.pallas_call(
        _attn_proj_kernel,
        grid=(bsz, t // tm),
        in_specs=_stream_specs(d) + [
            _mod_spec(d),
            pl.BlockSpec((1, d), const),
            pl.BlockSpec(w.shape, const),
            pl.BlockSpec(wvt.shape, const),
            pl.BlockSpec((1, MLA_Q_RANK), const),
            pl.BlockSpec(qup.shape, const),
            pl.BlockSpec((1, MLA_KV_RANK), const),
            pl.BlockSpec(kvup.shape, const),
            pl.BlockSpec(kvupt.shape, const),
            tab, tab, tab, tab, tab, tab],
        out_specs=[o[0] for o in outs],
        out_shape=[o[1] for o in outs],
        compiler_params=_params("arbitrary", "arbitrary"),
        name="attn_project",
    )(xc, xl, mod, g.reshape(1, d), w, wvt, qn_g.reshape(1, -1), qup, kvn_g.reshape(1, -1), kvup, kvupt,
      *tabs_da, *tabs_m)


def _reduce_rows(x, fn):
    n, w = x.shape
    slab = 256
    if n > slab and n % slab == 0:
        x = fn(x.reshape(n // slab, slab, w), axis=0)
        n = slab
    if n > 32 and n % 32 == 0:
        x = fn(x.reshape(n // 32, 32, w), axis=0)
    return fn(x, axis=0, keepdims=True)


def _softmax_parts(s):
    e = jnp.exp2(s - _reduce_rows(s, jnp.max))
    return e.astype(BF16), 1.0 / _reduce_rows(e, jnp.sum)


def _attn_kernel(qda_ref, qm_ref, kda_ref, vat_ref, km_ref, vmt_ref, lam_ref, subg_ref, wout_ref,
                 x_ref, mod_ref, g2_ref, wrt_ref, o_ref, h_ref, lg_ref, *, lam_init):
    lamv = lam_ref[...]
    lam = (jnp.exp(jnp.sum(lamv[0:1] * lamv[1:2], axis=1, keepdims=True))
           - jnp.exp(jnp.sum(lamv[2:3] * lamv[3:4], axis=1, keepdims=True)) + lam_init)
    first = lax.broadcasted_iota(jnp.int32, (1, LANES), 1) < LANES // 2

    n_maps = 2 * DA_HEADS + MLA_HEADS
    maps = []
    for i in range(2 * DA_HEADS):
        sl = slice(LANES * (i // 2), LANES * (i // 2 + 1))
        q = qda_ref[0, :, sl].astype(F32)
        q = jnp.where(first, q, 0.0) if i % 2 == 0 else jnp.where(first, 0.0, q)
        maps.append((q.astype(BF16), kda_ref, sl, vat_ref, sl))
    for i in range(MLA_HEADS):
        sl = slice(LANES * i, LANES * (i + 1))
        maps.append((qm_ref[0, :, sl], km_ref, sl, vmt_ref, slice(MLA_V * i, MLA_V * (i + 1))))

    kb = ATTN_KEY_BLOCK
    n_blocks = kda_ref.shape[1] // kb

    units = [(i, j) for i in range(n_maps) for j in range(n_blocks)]

    def scores(u):
        i, j = units[u]
        q, k_ref, ksl, _, _ = maps[i]
        return _dot_nt(k_ref[0, j * kb:(j + 1) * kb, ksl], q)

    pending = [scores(u) for u in range(min(ATTN_LOOKAHEAD, len(units)))]
    heads = []
    for u, (i, j) in enumerate(units):
        s = pending.pop(0)
        if u + ATTN_LOOKAHEAD < len(units):
            pending.append(scores(u + ATTN_LOOKAHEAD))
        _, _, _, v_ref, vsl = maps[i]
        vt = v_ref[0, vsl, j * kb:(j + 1) * kb]
        mb = _reduce_rows(s, jnp.max)
        if j == 0:
            p = jnp.exp2(s - mb)
            m, l = mb, _reduce_rows(p, jnp.sum)
            acc = _dot(vt, p.astype(BF16))
        else:
            mn = jnp.maximum(m, mb)
            a = jnp.exp2(m - mn)
            p = jnp.exp2(s - mn)
            m, l = mn, l * a + _reduce_rows(p, jnp.sum)
            acc = acc * a + _dot(vt, p.astype(BF16))
        if j == n_blocks - 1:
            heads.append(acc * (1.0 / l))
    outs = []
    for h in range(DA_HEADS):
        oa = heads[2 * h] - heads[2 * h + 1] * lam
        oa = oa * lax.rsqrt(jnp.mean(oa * oa, axis=0, keepdims=True) + EPS)
        outs.append(oa * subg_ref[...] * (1.0 - lam_init))
    outs.extend(heads[2 * DA_HEADS:])
    att = jnp.concatenate(outs, axis=0).astype(BF16)
    mod = mod_ref[0, 0]
    y = x_ref[0] + mod[2:3] * _dot_tn(att, wout_ref[...])
    o_ref[0] = y
    _route_rows(y, mod, g2_ref, wrt_ref, h_ref, lg_ref)


def _attention(qda, kda, vat, qm, km, vmt, lamv, subln_g, w_out, x, mod, g2, w_router, group, q_block0,
               n_keys, lam_init):
    bsz, nq, d = x.shape
    tq = ROW_TILE
    row = lambda b, i: (b, i, 0)
    qrow = lambda b, i: (b, i + q_block0, 0)
    full = lambda b, i: (b, 0, 0)
    const = lambda b, i: (0, 0)
    wout = w_out.astype(BF16)
    r_in, r_out, r_shape = _route_specs(bsz, nq, d, w_router.shape[1], tq)
    return pl.pallas_call(
        functools.partial(_attn_kernel, lam_init=lam_init),
        grid=(bsz, nq // tq),
        in_specs=[pl.BlockSpec((1, tq, qda.shape[2]), qrow),
                  pl.BlockSpec((1, tq, qm.shape[2]), qrow),
                  pl.BlockSpec((1, n_keys, kda.shape[2]), full),
                  pl.BlockSpec((1, vat.shape[1], n_keys), full),
                  pl.BlockSpec((1, n_keys, km.shape[2]), full),
                  pl.BlockSpec((1, vmt.shape[1], n_keys), full),
                  pl.BlockSpec(lamv.shape, const),
                  pl.BlockSpec((DA_V, 1), const),
                  pl.BlockSpec(wout.shape, const),
                  pl.BlockSpec((1, tq, d), row),
                  _mod_spec(d, group)] + r_in,
        out_specs=[pl.BlockSpec((1, tq, d), row)] + r_out,
        out_shape=[jax.ShapeDtypeStruct((bsz, nq, d), F32)] + r_shape,
        compiler_params=_params("arbitrary", "arbitrary"),
        name="attention",
    )(qda, qm, kda, vat, km, vmt, lamv, subln_g.reshape(-1, 1), wout, x, mod, g2.reshape(1, d), w_router.T)


def _kth_largest(a, k):
    rows = a.shape[0]
    lo0 = jnp.full((rows, 1), -1.0, F32)
    hi0 = jnp.max(a, axis=1, keepdims=True)

    def cond(c):
        return c[2] > 0

    def body(c):
        lo, hi, _ = c
        mid = 0.5 * (lo + hi)
        open_ = (mid > lo) & (mid < hi)
        cnt = jnp.sum(jnp.where(a > mid, 1.0, 0.0), axis=1, keepdims=True)
        up = open_ & (cnt >= k)
        down = open_ & (cnt < k)
        lo = jnp.where(up, mid, lo)
        hi = jnp.where(down, mid, hi)
        return lo, hi, jnp.sum(jnp.where(open_, 1.0, 0.0))

    _, hi, _ = lax.while_loop(cond, body, (lo0, hi0, jnp.float32(1.0)))
    return hi


def _select_kernel(lg_ref, slot_ref, aff_ref, tri_ref, *, ne, cap):
    rows, n = lg_ref.shape
    rt = 256
    for r in range(0, n, rt):
        ri = lax.broadcasted_iota(jnp.int32, (rt, n), 0) + r
        ci = lax.broadcasted_iota(jnp.int32, (rt, n), 1)
        tri_ref[r:r + rt, :] = jnp.where(ri < ci, 1.0, 0.0).astype(BF16)

    lg = lg_ref[...].reshape(rows // ne, ne, n)
    e = jnp.exp(lg - jnp.max(lg, axis=1, keepdims=True))
    aff = (e / jnp.sum(e, axis=1, keepdims=True)).reshape(rows, n)
    aff_ref[...] = aff
    thr = _kth_largest(aff, cap)
    gt = aff > thr
    eq = aff == thr
    need = cap - jnp.sum(jnp.where(gt, 1.0, 0.0), axis=1, keepdims=True)
    tri = tri_ref[...]
    before = _dot(jnp.where(eq, 1.0, 0.0).astype(BF16), tri)
    sel = gt | (eq & (before < need))
    pos = _dot(jnp.where(sel, 1.0, 0.0).astype(BF16), tri)
    slot_ref[...] = jnp.where(sel, pos, -1.0).astype(jnp.int32)


def _select(logits, cap):
    bsz, ne, n = logits.shape
    blk = pl.BlockSpec((bsz * ne, n), lambda i: (0, 0))
    slot, aff = pl.pallas_call(
        functools.partial(_select_kernel, ne=ne, cap=cap),
        grid=(1,),
        in_specs=[blk],
        out_specs=[blk, blk],
        out_shape=[jax.ShapeDtypeStruct((bsz * ne, n), jnp.int32),
                   jax.ShapeDtypeStruct((bsz * ne, n), F32)],
        scratch_shapes=[pltpu.VMEM((n, n), BF16)],
        compiler_params=_params("arbitrary"),
        name="moe_select",
    )(logits.reshape(bsz * ne, n))
    return slot.reshape(bsz, ne, n), aff.reshape(bsz, ne, n)


GATHER_EXPERTS = 8


def _gather_kernel(h_ref, slot_ref, aff_ref, xe_ref, gate_ref, onehot_ref, *, cap):
    n = h_ref.shape[1]
    pi = lax.broadcasted_iota(jnp.int32, (cap, n), 0)
    for e in range(GATHER_EXPERTS):
        hit = slot_ref[0, e:e + 1, :] == pi
        onehot_ref[e * cap:(e + 1) * cap, :] = jnp.where(hit, 1.0, 0.0).astype(BF16)
        gate_ref[e] = jnp.sum(jnp.where(hit, aff_ref[0, e:e + 1, :], 0.0), axis=1, keepdims=True)
    xe = _dot(onehot_ref[...], h_ref[0]).astype(BF16)
    xe_ref[...] = xe.reshape(GATHER_EXPERTS, cap, xe.shape[1])


def _gather(h, slot, aff, cap):
    bsz, n, d = h.shape
    ne = slot.shape[1]
    ge = GATHER_EXPERTS
    return pl.pallas_call(
        functools.partial(_gather_kernel, cap=cap),
        grid=(bsz, ne // ge),
        in_specs=[pl.BlockSpec((1, n, d), lambda b, j: (b, 0, 0)),
                  pl.BlockSpec((1, ge, n), lambda b, j: (b, j, 0)),
                  pl.BlockSpec((1, ge, n), lambda b, j: (b, j, 0))],
        out_specs=[pl.BlockSpec((ge, cap, d), lambda b, j: (j, b, 0)),
                   pl.BlockSpec((ge, cap, 1), lambda b, j: (j, b, 0))],
        out_shape=[jax.ShapeDtypeStruct((ne, bsz * cap, d), BF16),
                   jax.ShapeDtypeStruct((ne, bsz * cap, 1), F32)],
        scratch_shapes=[pltpu.VMEM((ge * cap, n), BF16)],
        compiler_params=_params("arbitrary", "arbitrary"),
        name="moe_gather",
    )(h, slot, aff)


def _ffn_kernel(*refs, nseg, rows_per_pass):
    x_refs = refs[:nseg]
    gate_refs = refs[nseg:2 * nseg]
    wg_ref, wu_ref, wd_ref = refs[2 * nseg:2 * nseg + 3]
    y_refs = refs[2 * nseg + 3:3 * nseg + 3]
    acc_refs = refs[3 * nseg + 3:]
    f = pl.program_id(1)

    @pl.when(f == 0)
    def _():
        for acc_ref in acc_refs:
            acc_ref[...] = jnp.zeros_like(acc_ref)

    wg = wg_ref[0].astype(BF16)
    wu = wu_ref[0].astype(BF16)
    wd = wd_ref[0].astype(BF16)
    for x_ref, acc_ref in zip(x_refs, acc_refs):
        m = x_ref.shape[1]
        step = min(rows_per_pass, m)
        for r in range(0, m, step):
            x = x_ref[0, r:r + step, :]
            hid = (_silu(_dot(x, wg)) * _dot(x, wu)).astype(BF16)
            acc_ref[r:r + step, :] += _dot(hid, wd)

    @pl.when(f == pl.num_programs(1) - 1)
    def _():
        for y_ref, gate_ref, acc_ref in zip(y_refs, gate_refs, acc_refs):
            y_ref[0] = (acc_ref[...] * gate_ref[0]).astype(BF16)


def _ffn(xes, gates, w_gate, w_up, w_down, layer):
    _, ne, d, ff = w_gate.shape
    tf = 256
    nseg = len(xes)
    return pl.pallas_call(
        functools.partial(_ffn_kernel, nseg=nseg, rows_per_pass=512),
        grid=(ne, ff // tf),
        in_specs=[pl.BlockSpec((1, xe.shape[1], d), lambda e, f: (e, 0, 0)) for xe in xes]
        + [pl.BlockSpec((1, xe.shape[1], 1), lambda e, f: (e, 0, 0)) for xe in xes]
        + [pl.BlockSpec((None, 1, d, tf), lambda e, f: (layer, e, 0, f)),
           pl.BlockSpec((None, 1, d, tf), lambda e, f: (layer, e, 0, f)),
           pl.BlockSpec((None, 1, tf, d), lambda e, f: (layer, e, f, 0))],
        out_specs=[pl.BlockSpec((1, xe.shape[1], d), lambda e, f: (e, 0, 0)) for xe in xes],
        out_shape=[jax.ShapeDtypeStruct(xe.shape, BF16) for xe in xes],
        scratch_shapes=[pltpu.VMEM(xe.shape[1:], F32) for xe in xes],
        compiler_params=_params("arbitrary", "arbitrary"),
        name="moe_ffn",
    )(*xes, *gates, w_gate, w_up, w_down)


def _combine_kernel(*refs, cap, final_norm):
    slot_ref, ye_ref, x_ref, mod_ref = refs[:4]
    if final_norm:
        fg_ref, o_ref, onehot_ref = refs[4:]
    else:
        o_ref, onehot_ref = refs[4:]
    ne = ye_ref.shape[0]
    if cap % LANES == 0:
        pi = lax.broadcasted_iota(jnp.int32, (1, cap), 1)
        for e in range(ne):
            hit = slot_ref[0, :, e:e + 1] == pi
            onehot_ref[:, e * cap:(e + 1) * cap] = jnp.where(hit, 1.0, 0.0).astype(BF16)
    else:
        ei = lax.broadcasted_iota(jnp.int32, (ne, ne * cap), 0)
        li = lax.broadcasted_iota(jnp.int32, (ne, ne * cap), 1)
        spread = jnp.where((li >= ei * cap) & (li < (ei + 1) * cap), 1.0, 0.0).astype(BF16)
        slots = _dot(slot_ref[0].astype(F32).astype(BF16), spread)
        group = jnp.sum(jnp.where(li >= (ei + 1) * cap, 1, 0), axis=0, keepdims=True)
        pos = (li[0:1] - group * cap).astype(F32)
        onehot_ref[...] = jnp.where(slots == pos, 1.0, 0.0).astype(BF16)
    ye = ye_ref[...]
    moe = _dot(onehot_ref[...], ye.reshape(ne * cap, ye.shape[2]))
    y = x_ref[0] + mod_ref[0, 0][5:6] * moe
    if final_norm:
        y = _rms(y) * fg_ref[...]
    o_ref[0] = y


def _combine(slot_t, ye, x, mod, group, cap, final_g):
    bsz, n, d = x.shape
    ne = slot_t.shape[2]
    tm = min(n, 1024)
    final_norm = final_g is not None
    in_specs = [pl.BlockSpec((1, tm, ne), lambda b, i: (b, i, 0)),
                pl.BlockSpec((ne, cap, d), lambda b, i: (0, b, 0)),
                pl.BlockSpec((1, tm, d), lambda b, i: (b, i, 0)),
                _mod_spec(d, group)]
    args = [slot_t, ye, x, mod]
    if final_norm:
        in_specs.append(pl.BlockSpec((1, d), lambda b, i: (0, 0)))
        args.append(final_g.reshape(1, d))
    return pl.pallas_call(
        functools.partial(_combine_kernel, cap=cap, final_norm=final_norm),
        grid=(bsz, n // tm),
        in_specs=in_specs,
        out_specs=pl.BlockSpec((1, tm, d), lambda b, i: (b, i, 0)),
        out_shape=jax.ShapeDtypeStruct((bsz, n, d), F32),
        scratch_shapes=[pltpu.VMEM((tm, ne * cap), BF16)],
        compiler_params=_params("arbitrary", "arbitrary"),
        name="moe_combine",
    )(*args)


def _moe(streams, groups, mod, w_gate, w_up, w_down, layer, final_g=None):
    xs = [s[0] for s in streams]
    routed = []
    for x, h, logits in streams:
        cap = EC_CAPACITY_FACTOR * x.shape[1] // logits.shape[1]
        slot, aff = _select(logits, cap)
        xe, gate = _gather(h, slot, aff, cap)
        routed.append((cap, slot, xe, gate))
    yes = _ffn([r[2] for r in routed], [r[3] for r in routed], w_gate, w_up, w_down, layer)
    return [_combine(jnp.swapaxes(slot, 1, 2), ye, x, mod, group, cap, final_g)
            for x, group, ye, (cap, slot, _, _) in zip(xs, groups, yes, routed)]


def _chunk_masks(tm, direction):
    ri = lax.broadcasted_iota(jnp.int32, (tm, tm), 0)
    ci = lax.broadcasted_iota(jnp.int32, (tm, tm), 1)
    same = jnp.right_shift(ri, 6) == jnp.right_shift(ci, 6)
    return same & ((ci <= ri) if direction == 0 else (ci >= ri))


def _scan_operands(q, k, la, direction, refs):
    q_in_ref, q_mid_ref, k_mid_ref, k_out_ref, decay_ref = refs
    tm, n = q.shape
    nc = tm // CHUNK
    cum = jnp.where(_chunk_masks(tm, direction), 1.0, 0.0).astype(BF16)
    l1, l2, l3 = _split3(la)
    b = (_dot(cum, l1) + (_dot(cum, l2) + _dot(cum, l3))).reshape(nc, CHUNK, n)
    last, mid = (CHUNK - 1, CHUNK // 2) if direction == 0 else (0, CHUNK // 2 - 1)
    b_tot = b[:, last:last + 1]
    b_mid = b[:, mid:mid + 1]
    q = q.reshape(nc, CHUNK, n)
    k = k.reshape(nc, CHUNK, n)
    q_in_ref[direction, 0] = (q * jnp.exp(b)).astype(BF16).reshape(tm, n)
    q_mid_ref[direction, 0] = (q * jnp.exp(b - b_mid)).astype(BF16).reshape(tm, n)
    k_mid_ref[direction, 0] = (k * jnp.exp(b_mid - b)).astype(BF16).reshape(tm, n)
    k_out_ref[direction, 0] = (k * jnp.exp(b_tot - b)).astype(BF16).reshape(tm, n)
    decay = jnp.exp(b_tot).reshape(nc, n)
    decay_ref[direction, 0, 0] = jnp.concatenate([decay] * (8 // nc), axis=0)


def _rec_proj_kernel(xc_ref, xl_ref, mod_ref, g_ref, w_ref, gkup_ref, gkb_ref, lbl_ref, *out_refs, layer):
    gla_refs, gv_ref, gg_ref = out_refs[0:5], out_refs[5], out_refs[6]
    hg_refs, hv_ref, hg_ref = out_refs[7:12], out_refs[12], out_refs[13]
    mod = mod_ref[0, 0]
    h = _modulate(_stream_block(xc_ref, xl_ref), g_ref[...], mod[0:1], mod[1:2]).astype(BF16)
    proj = _dot(h, w_ref[...])
    n = GLA_HEADS * LANES
    gv_ref[0] = proj[:, 2 * n:3 * n].astype(BF16)
    gg_ref[0] = proj[:, 3 * n:4 * n]
    hv_ref[0] = proj[:, 7 * n:8 * n].astype(BF16)
    hg_ref[0] = proj[:, 8 * n:9 * n]
    gq = proj[:, 0:n] * GLA_DK ** -0.5
    gk = proj[:, n:2 * n]
    hq = _silu(proj[:, 4 * n:5 * n])
    gdown = proj[:, 9 * n:9 * n + LANES]
    gate = _dot(gdown.astype(BF16), gkup_ref[...]) + gkb_ref[...]
    lbl = lbl_ref[...]
    e = jnp.exp(lbl - jnp.max(lbl, axis=0, keepdims=True))
    sm = e / jnp.sum(e, axis=0, keepdims=True)
    lb = jnp.sum(sm[0:layer + 1], axis=0, keepdims=True) - sm[0:1]
    f = lb + (1.0 - lb) * jax.nn.sigmoid(proj[:, 5 * n:7 * n])
    for dd in range(2):
        gla = jax.nn.log_sigmoid(gate[:, dd * n:(dd + 1) * n]) / GLA_GATE_NORM
        _scan_operands(gq, gk, gla, dd, gla_refs)
        fd = f[:, dd * n:(dd + 1) * n]
        _scan_operands(hq, 1.0 - fd, jnp.log(fd), dd, hg_refs)


def _pad_heads(w, heads, dk):
    lead = w.shape[:-1]
    w3 = w.reshape(lead + (heads, dk))
    return jnp.pad(w3, [(0, 0)] * len(lead) + [(0, 0), (0, LANES - dk)]).reshape(lead + (heads * LANES,))


def _rec_project(xc, xl, mod, g, w_in, gk_up, gk_bias, lb_logits, layer):
    bsz, n_ctx, d = xc.shape
    t = n_ctx + xl.shape[1]
    tm = ROW_TILE
    hk = GLA_HEADS * GLA_DK
    hv = GLA_HEADS * GLA_DV
    hf = HG_HEADS * HG_DF
    sizes = (hk, hk, hv, hv, 2 * GLA_GATE_RANK, hf, 2 * hf, HG_HEADS * HG_DV, HG_HEADS * HG_DV)
    offs = [0]
    for s in sizes:
        offs.append(offs[-1] + s)
    gq, gk, gv, gg, gdown, hq, hff, hi, hgt = [w_in[:, offs[i]:offs[i + 1]] for i in range(9)]
    w = jnp.concatenate([_pad_heads(gq, GLA_HEADS, GLA_DK), _pad_heads(gk, GLA_HEADS, GLA_DK), gv, gg,
                         hq, hff, hi, hgt, jnp.pad(gdown, ((0, 0), (0, LANES - 2 * GLA_GATE_RANK)))],
                        axis=1).astype(BF16)
    n = GLA_HEADS * LANES
    ups = [_pad_heads(gk_up[dd], GLA_HEADS, GLA_DK) for dd in range(2)]
    zero = jnp.zeros_like(ups[0])
    gkup = jnp.concatenate([jnp.concatenate([ups[0], zero], axis=1), jnp.concatenate([zero, ups[1]], axis=1),
                            jnp.zeros((LANES - 2 * GLA_GATE_RANK, 2 * n), F32)], axis=0).astype(BF16)
    gkb = jnp.concatenate([_pad_heads(gk_bias[dd], GLA_HEADS, GLA_DK) for dd in range(2)]).reshape(1, 2 * n)
    row = lambda b, i: (b, i, 0)
    drow = lambda b, i: (0, b, i, 0)
    const = lambda b, i: (0, 0)
    one = lambda dt: (pl.BlockSpec((1, tm, n), row), jax.ShapeDtypeStruct((bsz, t, n), dt))
    two = (pl.BlockSpec((2, 1, tm, n), drow), jax.ShapeDtypeStruct((2, bsz, t, n), BF16))
    dec = (pl.BlockSpec((2, 1, 1, 8, n), lambda b, i: (0, b, i, 0, 0)),
           jax.ShapeDtypeStruct((2, bsz, t // tm, 8, n), F32))
    mixer = [two, two, two, two, dec, one(BF16), one(F32)]
    outs = mixer + mixer
    return pl.pallas_call(
        functools.partial(_rec_proj_kernel, layer=layer),
        grid=(bsz, t // tm),
        in_specs=_stream_specs(d) + [
            _mod_spec(d),
            pl.BlockSpec((1, d), const),
            pl.BlockSpec(w.shape, const),
            pl.BlockSpec(gkup.shape, const),
            pl.BlockSpec(gkb.shape, const),
            pl.BlockSpec((lb_logits.shape[0], 2 * hf), const)],
        out_specs=[o[0] for o in outs],
        out_shape=[o[1] for o in outs],
        compiler_params=_params("arbitrary", "arbitrary"),
        name="rec_project",
    )(xc, xl, mod, g.reshape(1, d), w, gkup, gkb, lb_logits.reshape(lb_logits.shape[0], 2 * hf))


def _scan_kernel(*refs, heads):
    ins = [refs[0:6], refs[6:12]]
    o_refs = refs[12:14]
    st_ref = refs[14]
    tm = o_refs[0].shape[1]
    nc = tm // CHUNK

    @pl.when(pl.program_id(1) == 0)
    def _():
        st_ref[...] = jnp.zeros_like(st_ref)

    lanes = [slice(LANES * h, LANES * (h + 1)) for h in range(heads)]
    chunk_rows = [slice(c * CHUNK, (c + 1) * CHUNK) for c in range(nc)]
    units = [(d, h) for d in range(2) for h in range(heads)]
    vs = {(d, h): ins[d][5][0, :, lanes[h]] for d, h in units}
    scores = {(d, h): _dot_nt(ins[d][1][0, 0, :, lanes[h]], ins[d][2][0, 0, :, lanes[h]]) for d, h in units}
    updates = {(d, h): [_dot_tn(vs[d, h][rows], ins[d][3][0, 0, rows, lanes[h]]) for rows in chunk_rows]
               for d, h in units}
    causal = [_chunk_masks(tm, d) for d in range(2)]
    intra = {(d, h): _dot(jnp.where(causal[d], scores[d, h], 0.0).astype(BF16), vs[d, h]) for d, h in units}
    order = [list(range(nc)), list(range(nc - 1, -1, -1))]
    entering = {}
    for d, h in units:
        st = st_ref[d, h]
        for c in order[d]:
            entering[d, h, c] = st.astype(BF16)
            st = st * ins[d][4][0, 0, 0, c:c + 1, lanes[h]] + updates[d, h][c]
        st_ref[d, h] = st
    for d, h in units:
        for c in order[d]:
            rows = chunk_rows[c]
            inter = _dot_nt(ins[d][0][0, 0, rows, lanes[h]], entering[d, h, c])
            o_refs[d][0, rows, lanes[h]] = intra[d, h][rows] + inter


def _scan(q_in, q_mid, k_mid, k_out, decay, v, n_ctx):
    _, bsz, t, n = q_in.shape
    heads = n // LANES
    tm = ROW_TILE
    nb = t // tm
    ncb = n_ctx // tm

    def blk(d, s):
        return s if d == 0 else jnp.where(s < ncb, ncb - 1 - s, nb - 1 - (s - ncb))

    in_specs, args = [], []
    for d in range(2):
        drow = lambda b, s, d=d: (d, b, blk(d, s), 0)
        in_specs += [pl.BlockSpec((1, 1, tm, n), drow)] * 4
        in_specs += [pl.BlockSpec((1, 1, 1, 8, n), lambda b, s, d=d: (d, b, blk(d, s), 0, 0)),
                     pl.BlockSpec((1, tm, n), lambda b, s, d=d: (b, blk(d, s), 0))]
        args += [q_in, q_mid, k_mid, k_out, decay, v]
    return pl.pallas_call(
        functools.partial(_scan_kernel, heads=heads),
        grid=(bsz, nb),
        in_specs=in_specs,
        out_specs=[pl.BlockSpec((1, tm, n), lambda b, s, d=d: (b, blk(d, s), 0)) for d in range(2)],
        out_shape=[jax.ShapeDtypeStruct((bsz, t, n), F32)] * 2,
        scratch_shapes=[pltpu.VMEM((2, heads, LANES, LANES), F32)],
        compiler_params=_params("arbitrary", "arbitrary"),
        name="rec_scan",
    )(*args)


def _rec_out_kernel(og0_ref, og1_ref, oh0_ref, oh1_ref, gg_ref, hg_ref, gn_ref, hn_ref, wout_ref,
                    x_ref, mod_ref, g2_ref, wrt_ref, o_ref, h_ref, lg_ref):
    parts = []
    for fwd_ref, bwd_ref, gate_ref, n_ref in ((og0_ref, og1_ref, gg_ref, gn_ref),
                                              (oh0_ref, oh1_ref, hg_ref, hn_ref)):
        for h in range(fwd_ref.shape[2] // LANES):
            sl = slice(LANES * h, LANES * (h + 1))
            o = fwd_ref[0, :, sl] + bwd_ref[0, :, sl]
            parts.append(_rms(o) * n_ref[...] * _silu(gate_ref[0, :, sl]))
    a = jnp.concatenate(parts, axis=1).astype(BF16)
    mod = mod_ref[0, 0]
    y = x_ref[0] + mod[2:3] * _dot(a, wout_ref[...])
    o_ref[0] = y
    _route_rows(y, mod, g2_ref, wrt_ref, h_ref, lg_ref)


def _rec_readout(og, oh, gg, hg, gn, hn, w_out, xl, mod, g2, w_router, n_ctx):
    bsz, seq, d = xl.shape
    tm = ROW_TILE
    ncb = n_ctx // tm
    n = gg.shape[2]
    row = lambda b, i: (b, i + ncb, 0)
    const = lambda b, i: (0, 0)
    wout = w_out.astype(BF16)
    r_in, r_out, r_shape = _route_specs(bsz, seq, d, w_router.shape[1], tm)
    return pl.pallas_call(
        _rec_out_kernel,
        grid=(bsz, seq // tm),
        in_specs=[pl.BlockSpec((1, tm, n), row),
                  pl.BlockSpec((1, tm, n), row),
                  pl.BlockSpec((1, tm, n), row),
                  pl.BlockSpec((1, tm, n), row),
                  pl.BlockSpec((1, tm, n), row),
                  pl.BlockSpec((1, tm, n), row),
                  pl.BlockSpec((1, LANES), const),
                  pl.BlockSpec((1, LANES), const),
                  pl.BlockSpec(wout.shape, const),
                  pl.BlockSpec((1, tm, d), lambda b, i: (b, i, 0)),
                  _mod_spec(d, 1)] + r_in,
        out_specs=[pl.BlockSpec((1, tm, d), lambda b, i: (b, i, 0))] + r_out,
        out_shape=[jax.ShapeDtypeStruct((bsz, seq, d), F32)] + r_shape,
        compiler_params=_params("arbitrary", "arbitrary"),
        name="rec_readout",
    )(*og, *oh, gg, hg, gn.reshape(1, -1), hn.reshape(1, -1), wout, xl, mod, g2.reshape(1, d), w_router.T)


def kernel(x, c, ctx, c_ctx, ada_w, ada_b, norm1_g, norm2_g, att_w_in, mla_q_norm_g, mla_q_up, mla_kv_norm_g, mla_kv_up, da_lam_q1, da_lam_k1, da_lam_q2, da_lam_k2, da_subln_g, att_w_out, rec_w_in, gla_gk_up, gla_gk_bias, gla_norm_g, hg_lb_logits, hg_norm_g, rec_w_out, moe_router, moe_w_gate, moe_w_up, moe_w_down, final_norm_g):
    bsz, seq, d = x.shape
    n_ctx = ctx.shape[1]
    depth = ada_w.shape[0]
    assert depth == 2 and n_ctx == ROW_TILE and seq % ROW_TILE == 0 and bsz + 1 <= 16

    cvec = jnp.concatenate([c, c_ctx[None, :], jnp.zeros((16 - bsz - 1, d), F32)], axis=0)
    ada = _ada_vectors(cvec, ada_w, ada_b).reshape(depth, 16, 6, d)
    ada = jnp.pad(ada, ((0, 0), (0, 0), (0, 2), (0, 0)))
    mods = jnp.stack([jnp.broadcast_to(ada[:, bsz][:, None], (depth, bsz, 8, d)), ada[:, :bsz]], axis=2)
    xc, xl = ctx, x

    mod = mods[0]
    tabs_da = _rope_tables(n_ctx, seq, DA_QK, 0, DA_QK)
    tabs_m = _rope_tables(n_ctx, seq, MLA_ROPE, MLA_NOPE, LANES)
    proj = _attn_project(xc, xl, mod, norm1_g[0], att_w_in[0], mla_q_norm_g[0], mla_q_up[0],
                         mla_kv_norm_g[0], mla_kv_up[0], tabs_da, tabs_m)
    lamv = jnp.stack([da_lam_q1[0], da_lam_k1[0], da_lam_q2[0], da_lam_k2[0]])
    lam_init = 0.8 - 0.6 * math.exp(-0.3 * 0)
    sc = _attention(*proj, lamv, da_subln_g[0], att_w_out[0], xc, mod, norm2_g[0], moe_router[0], 0, 0, n_ctx,
                    lam_init)
    sl = _attention(*proj, lamv, da_subln_g[0], att_w_out[0], xl, mod, norm2_g[0], moe_router[0], 1,
                    n_ctx // ROW_TILE, n_ctx + seq, lam_init)
    xc, xl = _moe([sc, sl], [0, 1], mod, moe_w_gate, moe_w_up, moe_w_down, 0)

    mod = mods[1]
    rec = _rec_project(xc, xl, mod, norm1_g[1], rec_w_in[0], gla_gk_up[0], gla_gk_bias[0], hg_lb_logits, 1)
    og = _scan(*rec[0:6], n_ctx)
    oh = _scan(*rec[7:13], n_ctx)
    sl = _rec_readout(og, oh, rec[6], rec[13], gla_norm_g[0], hg_norm_g[0], rec_w_out[0], xl, mod, norm2_g[1],
                      moe_router[1], n_ctx)
    (out,) = _moe([sl], [1], mod, moe_w_gate, moe_w_up, moe_w_down, 1, final_g=final_norm_g)
    return out
```

```python
import functools
import math

import jax
import jax.numpy as jnp
from jax import lax
from jax.experimental import pallas as pl
from jax.experimental.pallas import tpu as pltpu

F32 = jnp.float32
BF16 = jnp.bfloat16

EPS = 1e-6
GRID_W = 64
ROPE_BASE = 10000.0

DA_HEADS = 4
DA_QK = 64
DA_V = 2 * DA_QK
DA_SCALE = DA_QK ** -0.5
MLA_HEADS = 8
MLA_NOPE = 64
MLA_ROPE = 32
MLA_V = 64
MLA_Q_RANK = 384
MLA_KV_RANK = 256
MLA_SCALE = (MLA_NOPE + MLA_ROPE) ** -0.5
LOG2E = math.log2(math.e)

GLA_HEADS = 4
GLA_DK = 64
GLA_DV = 128
GLA_GATE_RANK = 16
GLA_GATE_NORM = 16.0
HG_HEADS = 4
HG_DF = 128
HG_DV = 128
CHUNK = 64

N_EXPERTS = 16
EC_CAPACITY_FACTOR = 2

LANES = 128
ROW_TILE = 256
ATTN_KEY_BLOCK = 256
SUM_ROWS = 16
ATTN_LOOKAHEAD = 4
VMEM_LIMIT = 56 * 1024 * 1024


def _params(*sem):
    return pltpu.CompilerParams(dimension_semantics=sem, vmem_limit_bytes=VMEM_LIMIT)


def _dot(a, b):
    return jnp.dot(a, b, preferred_element_type=F32)


def _dot_nt(a, b):
    return lax.dot_general(a, b, (((1,), (1,)), ((), ())), preferred_element_type=F32)


def _dot_tn(a, b):
    return lax.dot_general(a, b, (((0,), (0,)), ((), ())), preferred_element_type=F32)


def _split2(a):
    hi = a.astype(BF16)
    lo = (a - hi.astype(F32)).astype(BF16)
    return hi, lo


def _dot3(a, b, dot=_dot):
    ah, al = _split2(a)
    bh, bl = _split2(b)
    return dot(ah, bh) + (dot(ah, bl) + dot(al, bh))


def _rms(x):
    return x * lax.rsqrt(jnp.mean(x * x, axis=-1, keepdims=True) + EPS)


def _silu(x):
    return x * jax.nn.sigmoid(x)


def _modulate(x, g, shift, scale):
    return _rms(x) * g * (1.0 + scale) + shift


def _stream_specs(d):
    return [pl.BlockSpec((1, ROW_TILE, d), lambda b, i: (b, 0, 0)),
            pl.BlockSpec((1, ROW_TILE, d), lambda b, i: (b, jnp.maximum(i - 1, 0), 0))]


def _stream_block(xc_ref, xl_ref):
    return jnp.where(pl.program_id(1) == 0, xc_ref[0], xl_ref[0])


def _route_rows(y, mod, g2_ref, wrt_ref, h_ref, lg_ref):
    h = _modulate(y, g2_ref[...], mod[3:4], mod[4:5])
    hh, hl = _split2(h)
    h_ref[0] = hh
    wh, wl = _split2(wrt_ref[...])
    ne = wh.shape[0]
    both = _dot_nt(jnp.concatenate([wh, wl], axis=0), hh)
    lg_ref[0] = both[0:ne] + (_dot_nt(wh, hl) + both[ne:2 * ne])


def _route_specs(bsz, n, d, ne, tm):
    const = lambda b, i: (0, 0)
    return ([pl.BlockSpec((1, d), const), pl.BlockSpec((ne, d), const)],
            [pl.BlockSpec((1, tm, d), lambda b, i: (b, i, 0)), pl.BlockSpec((1, ne, tm), lambda b, i: (b, 0, i))],
            [jax.ShapeDtypeStruct((bsz, n, d), BF16), jax.ShapeDtypeStruct((bsz, ne, n), F32)])


def _mod_spec(d, group=None):
    if group is None:
        return pl.BlockSpec((1, 1, 8, d), lambda b, i: (b, jnp.minimum(i, 1), 0, 0))
    return pl.BlockSpec((1, 1, 8, d), lambda b, i: (b, group, 0, 0))


def _ada_kernel(c_ref, w_ref, b_ref, o_ref):
    o_ref[...] = _dot3(_silu(c_ref[...]), w_ref[...]) + b_ref[...]


def _ada_vectors(cvec, ada_w, ada_b):
    depth, d, n6 = ada_w.shape
    tn = n6 // 4
    rows = cvec.shape[0]
    return pl.pallas_call(
        _ada_kernel,
        grid=(depth, n6 // tn),
        in_specs=[pl.BlockSpec((rows, d), lambda l, j: (0, 0)),
                  pl.BlockSpec((None, d, tn), lambda l, j: (l, 0, j)),
                  pl.BlockSpec((None, 1, tn), lambda l, j: (l, 0, j))],
        out_specs=pl.BlockSpec((None, rows, tn), lambda l, j: (l, 0, j)),
        out_shape=jax.ShapeDtypeStruct((depth, rows, n6), F32),
        compiler_params=_params("arbitrary", "arbitrary"),
        name="ada_vectors",
    )(cvec, ada_w, ada_b.reshape(depth, 1, n6))


def _axial_angles(rows, rot_dim):
    row = jnp.repeat(jnp.arange(rows, dtype=F32), GRID_W)
    col = jnp.tile(jnp.arange(GRID_W, dtype=F32), rows)
    n_freq = rot_dim // 4
    inv = ROPE_BASE ** (-jnp.arange(n_freq, dtype=F32) / n_freq)
    ang = jnp.concatenate([row[:, None] * inv, col[:, None] * inv], axis=-1)
    return jnp.cos(ang), jnp.sin(ang)


def _rope_tables(n_ctx, seq, width, start, group):
    half = width // 2
    cos, sin = _axial_angles(seq // GRID_W, width)
    lane = jnp.arange(LANES) % group - start
    inside = (lane >= 0) & (lane < width)
    idx = jnp.clip(lane, 0, width - 1) % half
    second = inside & (lane >= half)
    firsth = inside & (lane < half)
    c = jnp.where(inside[None, :], cos[:, idx], 1.0)
    s_prev = jnp.where(second[None, :], sin[:, idx], 0.0)
    s_next = jnp.where(firsth[None, :], -sin[:, idx], 0.0)
    ident = jnp.ones((n_ctx, LANES), F32)
    zero = jnp.zeros((n_ctx, LANES), F32)
    return (jnp.concatenate([ident, c], 0), jnp.concatenate([zero, s_prev], 0),
            jnp.concatenate([zero, s_next], 0))


def _rope(x, cos, s_prev, s_next, half):
    return x * cos + pltpu.roll(x, half, 1) * s_prev + pltpu.roll(x, LANES - half, 1) * s_next


def _attn_proj_kernel(xc_ref, xl_ref, mod_ref, g_ref, w_ref, wvt_ref, vab_ref, qn_ref, qup_ref, kvn_ref,
                      kvup_ref, kvupt_ref, vmb_ref, cda_ref, pda_ref, nda_ref, cm_ref, pm_ref, nm_ref,
                      qda_ref, kda_ref, vat_ref, qm_ref, km_ref, vmt_ref):
    mod = mod_ref[0, 0]
    h = _modulate(_stream_block(xc_ref, xl_ref), g_ref[...], mod[0:1], mod[1:2]).astype(BF16)
    proj = _dot(h, w_ref[...])
    vat_ref[0] = (_dot_nt(wvt_ref[...], h) + vab_ref[...]).astype(BF16)
    cda, pda, nda = cda_ref[...], pda_ref[...], nda_ref[...]
    cm, pm, nm = cm_ref[...], pm_ref[...], nm_ref[...]
    n_da = DA_HEADS * LANES
    for hh in range(DA_HEADS):
        sl = slice(LANES * hh, LANES * (hh + 1))
        q = proj[:, LANES * hh:LANES * (hh + 1)]
        k = proj[:, n_da + LANES * hh:n_da + LANES * (hh + 1)]
        qda_ref[0, :, sl] = (_rope(q, cda, pda, nda, DA_QK // 2) * (DA_SCALE * LOG2E)).astype(BF16)
        kda_ref[0, :, sl] = _rope(k, cda, pda, nda, DA_QK // 2).astype(BF16)
    o = 2 * n_da
    cq = proj[:, o:o + MLA_Q_RANK]
    ckv = proj[:, o + MLA_Q_RANK:o + MLA_Q_RANK + MLA_KV_RANK]
    kr = proj[:, o + MLA_Q_RANK + MLA_KV_RANK:o + MLA_Q_RANK + MLA_KV_RANK + LANES]
    qm = _dot((_rms(cq) * qn_ref[...]).astype(BF16), qup_ref[...])
    ckvn = (_rms(ckv) * kvn_ref[...]).astype(BF16)
    kv = _dot(ckvn, kvup_ref[...])
    vmt_ref[0] = (_dot_nt(kvupt_ref[...], ckvn) + vmb_ref[...]).astype(BF16)
    rep = kr + pltpu.roll(kr, 32, 1) + pltpu.roll(kr, 64, 1) + pltpu.roll(kr, 96, 1)
    lane = lax.broadcasted_iota(jnp.int32, (1, LANES), 1)
    rep = jnp.where((lane >= MLA_NOPE) & (lane < MLA_NOPE + MLA_ROPE), rep, 0.0)
    rep = _rope(rep, cm, pm, nm, MLA_ROPE // 2)
    for hh in range(MLA_HEADS):
        sl = slice(LANES * hh, LANES * (hh + 1))
        qm_ref[0, :, sl] = (_rope(qm[:, sl], cm, pm, nm, MLA_ROPE // 2) * (MLA_SCALE * LOG2E)).astype(BF16)
        km_ref[0, :, sl] = (kv[:, sl] + rep).astype(BF16)


def _with_sum_rows(wt, heads):
    dv = wt.shape[0] // heads
    w3 = jnp.pad(wt.reshape(heads, dv, -1), ((0, 0), (0, SUM_ROWS), (0, 0)))
    bias = jnp.pad(jnp.zeros((heads, dv, 1), F32), ((0, 0), (0, SUM_ROWS), (0, 0)), constant_values=1.0)
    return w3.reshape(heads * (dv + SUM_ROWS), -1).astype(BF16), bias.reshape(heads * (dv + SUM_ROWS), 1)


def _attn_project(xc, xl, mod, g, w_in, qn_g, q_up, kvn_g, kv_up, tabs_da, tabs_m):
    bsz, n_ctx, d = xc.shape
    t = n_ctx + xl.shape[1]
    tm = ROW_TILE
    n_da = DA_HEADS * LANES
    nm = MLA_HEADS * LANES
    nv = MLA_HEADS * MLA_V
    w = jnp.concatenate([w_in[:, :2 * n_da], w_in[:, 3 * n_da:]], axis=1)
    w = jnp.pad(w, ((0, 0), (0, (-w.shape[1]) % LANES))).astype(BF16)
    wvt, vab = _with_sum_rows(w_in[:, 2 * n_da:3 * n_da].T, DA_HEADS)
    qup = jnp.pad(q_up.reshape(MLA_Q_RANK, MLA_HEADS, MLA_NOPE + MLA_ROPE),
                  ((0, 0), (0, 0), (0, LANES - MLA_NOPE - MLA_ROPE))).reshape(MLA_Q_RANK, -1).astype(BF16)
    kv3 = kv_up.reshape(MLA_KV_RANK, MLA_HEADS, MLA_NOPE + MLA_V)
    kvup = jnp.pad(kv3[:, :, :MLA_NOPE],
                   ((0, 0), (0, 0), (0, LANES - MLA_NOPE))).reshape(MLA_KV_RANK, -1).astype(BF16)
    kvupt, vmb = _with_sum_rows(kv3[:, :, MLA_NOPE:].reshape(MLA_KV_RANK, -1).T, MLA_HEADS)
    row = lambda b, i: (b, i, 0)
    col = lambda b, i: (b, 0, i)
    const = lambda b, i: (0, 0)
    tab = pl.BlockSpec((tm, LANES), lambda b, i: (i, 0))
    rows_out = lambda n: (pl.BlockSpec((1, tm, n), row), jax.ShapeDtypeStruct((bsz, t, n), BF16))
    cols_out = lambda n: (pl.BlockSpec((1, n, tm), col), jax.ShapeDtypeStruct((bsz, n, t), BF16))
    outs = [rows_out(n_da), rows_out(n_da), cols_out(wvt.shape[0]), rows_out(nm), rows_out(nm),
            cols_out(kvupt.shape[0])]
    return pl.pallas_call(
        _attn_proj_kernel,
        grid=(bsz, t // tm),
        in_specs=_stream_specs(d) + [
            _mod_spec(d),
            pl.BlockSpec((1, d), const),
            pl.BlockSpec(w.shape, const),
            pl.BlockSpec(wvt.shape, const),
            pl.BlockSpec(vab.shape, const),
            pl.BlockSpec((1, MLA_Q_RANK), const),
            pl.BlockSpec(qup.shape, const),
            pl.BlockSpec((1, MLA_KV_RANK), const),
            pl.BlockSpec(kvup.shape, const),
            pl.BlockSpec(kvupt.shape, const),
            pl.BlockSpec(vmb.shape, const),
            tab, tab, tab, tab, tab, tab],
        out_specs=[o[0] for o in outs],
        out_shape=[o[1] for o in outs],
        compiler_params=_params("arbitrary", "arbitrary"),
        name="attn_project",
    )(xc, xl, mod, g.reshape(1, d), w, wvt, vab, qn_g.reshape(1, -1), qup, kvn_g.reshape(1, -1), kvup, kvupt,
      vmb, *tabs_da, *tabs_m)


def _reduce_rows(x, fn):
    n, w = x.shape
    slab = 256
    if n > slab and n % slab == 0:
        x = fn(x.reshape(n // slab, slab, w), axis=0)
        n = slab
    if n > 32 and n % 32 == 0:
        x = fn(x.reshape(n // 32, 32, w), axis=0)
    return fn(x, axis=0, keepdims=True)


def _softmax_parts(s):
    e = jnp.exp2(s - _reduce_rows(s, jnp.max))
    return e.astype(BF16), 1.0 / _reduce_rows(e, jnp.sum)


def _attn_kernel(qda_ref, qm_ref, kda_ref, vat_ref, km_ref, vmt_ref, lam_ref, subg_ref, wout_ref,
                 x_ref, mod_ref, g2_ref, wrt_ref, o_ref, h_ref, lg_ref, *, lam_init):
    lamv = lam_ref[...]
    lam = (jnp.exp(jnp.sum(lamv[0:1] * lamv[1:2], axis=1, keepdims=True))
           - jnp.exp(jnp.sum(lamv[2:3] * lamv[3:4], axis=1, keepdims=True)) + lam_init)
    first = lax.broadcasted_iota(jnp.int32, (1, LANES), 1) < LANES // 2

    n_maps = 2 * DA_HEADS + MLA_HEADS
    maps = []
    for i in range(2 * DA_HEADS):
        sl = slice(LANES * (i // 2), LANES * (i // 2 + 1))
        q = qda_ref[0, :, sl].astype(F32)
        q = jnp.where(first, q, 0.0) if i % 2 == 0 else jnp.where(first, 0.0, q)
        rows = DA_V + SUM_ROWS
        maps.append((q.astype(BF16), kda_ref, sl, vat_ref, slice(rows * (i // 2), rows * (i // 2 + 1)), DA_V))
    for i in range(MLA_HEADS):
        sl = slice(LANES * i, LANES * (i + 1))
        rows = MLA_V + SUM_ROWS
        maps.append((qm_ref[0, :, sl], km_ref, sl, vmt_ref, slice(rows * i, rows * (i + 1)), MLA_V))

    kb = min(ATTN_KEY_BLOCK, kda_ref.shape[1])
    n_blocks = kda_ref.shape[1] // kb

    units = [(i, j) for i in range(n_maps) for j in range(n_blocks)]

    def scores(u):
        i, j = units[u]
        q, k_ref, ksl = maps[i][:3]
        return _dot_nt(k_ref[0, j * kb:(j + 1) * kb, ksl], q)

    pending = [scores(u) for u in range(min(ATTN_LOOKAHEAD, len(units)))]
    heads = []
    for u, (i, j) in enumerate(units):
        s = pending.pop(0)
        if u + ATTN_LOOKAHEAD < len(units):
            pending.append(scores(u + ATTN_LOOKAHEAD))
        v_ref, vsl, dv = maps[i][3:]
        vt = v_ref[0, vsl, j * kb:(j + 1) * kb]
        mb = _reduce_rows(s, jnp.max)
        if j == 0:
            m = mb
            acc = _dot(vt, jnp.exp2((s - m).astype(BF16)))
        else:
            mn = jnp.maximum(m, mb)
            acc = acc * jnp.exp2(m - mn) + _dot(vt, jnp.exp2((s - mn).astype(BF16)))
            m = mn
        if j == n_blocks - 1:
            heads.append(acc[0:dv] * (1.0 / acc[dv:dv + 1]))
    outs = []
    for h in range(DA_HEADS):
        oa = heads[2 * h] - heads[2 * h + 1] * lam
        oa = oa * lax.rsqrt(jnp.mean(oa * oa, axis=0, keepdims=True) + EPS)
        outs.append(oa * subg_ref[...] * (1.0 - lam_init))
    outs.extend(heads[2 * DA_HEADS:])
    att = jnp.concatenate(outs, axis=0).astype(BF16)
    mod = mod_ref[0, 0]
    y = x_ref[0] + mod[2:3] * _dot_tn(att, wout_ref[...])
    o_ref[0] = y
    _route_rows(y, mod, g2_ref, wrt_ref, h_ref, lg_ref)


def _attention(qda, kda, vat, qm, km, vmt, lamv, subln_g, w_out, x, mod, g2, w_router, group, q_block0,
               n_keys, lam_init):
    bsz, nq, d = x.shape
    tq = ROW_TILE
    row = lambda b, i: (b, i, 0)
    qrow = lambda b, i: (b, i + q_block0, 0)
    full = lambda b, i: (b, 0, 0)
    const = lambda b, i: (0, 0)
    wout = w_out.astype(BF16)
    r_in, r_out, r_shape = _route_specs(bsz, nq, d, w_router.shape[1], tq)
    return pl.pallas_call(
        functools.partial(_attn_kernel, lam_init=lam_init),
        grid=(bsz, nq // tq),
        in_specs=[pl.BlockSpec((1, tq, qda.shape[2]), qrow),
                  pl.BlockSpec((1, tq, qm.shape[2]), qrow),
                  pl.BlockSpec((1, n_keys, kda.shape[2]), full),
                  pl.BlockSpec((1, vat.shape[1], n_keys), full),
                  pl.BlockSpec((1, n_keys, km.shape[2]), full),
                  pl.BlockSpec((1, vmt.shape[1], n_keys), full),
                  pl.BlockSpec(lamv.shape, const),
                  pl.BlockSpec((DA_V, 1), const),
                  pl.BlockSpec(wout.shape, const),
                  pl.BlockSpec((1, tq, d), row),
                  _mod_spec(d, group)] + r_in,
        out_specs=[pl.BlockSpec((1, tq, d), row)] + r_out,
        out_shape=[jax.ShapeDtypeStruct((bsz, nq, d), F32)] + r_shape,
        compiler_params=_params("arbitrary", "arbitrary"),
        name="attention",
    )(qda, qm, kda, vat, km, vmt, lamv, subln_g.reshape(-1, 1), wout, x, mod, g2.reshape(1, d), w_router.T)


def _kth_largest(a, k):
    rows = a.shape[0]
    lo0 = jnp.full((rows, 1), -1.0, F32)
    hi0 = jnp.max(a, axis=1, keepdims=True)

    def cond(c):
        return c[2] > 0

    def body(c):
        lo, hi, _ = c
        mid = 0.5 * (lo + hi)
        open_ = (mid > lo) & (mid < hi)
        cnt = jnp.sum(jnp.where(a > mid, 1.0, 0.0), axis=1, keepdims=True)
        up = open_ & (cnt >= k)
        down = open_ & (cnt < k)
        lo = jnp.where(up, mid, lo)
        hi = jnp.where(down, mid, hi)
        return lo, hi, jnp.sum(jnp.where(open_, 1.0, 0.0))

    _, hi, _ = lax.while_loop(cond, body, (lo0, hi0, jnp.float32(1.0)))
    return hi


def _select_kernel(lg_ref, slot_ref, aff_ref, tri_ref, *, ne, cap):
    rows, n = lg_ref.shape
    rt = 256
    for r in range(0, n, rt):
        ri = lax.broadcasted_iota(jnp.int32, (rt, n), 0) + r
        ci = lax.broadcasted_iota(jnp.int32, (rt, n), 1)
        tri_ref[r:r + rt, :] = jnp.where(ri < ci, 1.0, 0.0).astype(BF16)

    lg = lg_ref[...].reshape(rows // ne, ne, n)
    e = jnp.exp(lg - jnp.max(lg, axis=1, keepdims=True))
    aff = (e / jnp.sum(e, axis=1, keepdims=True)).reshape(rows, n)
    aff_ref[...] = aff
    thr = _kth_largest(aff, cap)
    gt = aff > thr
    eq = aff == thr
    need = cap - jnp.sum(jnp.where(gt, 1.0, 0.0), axis=1, keepdims=True)
    tri = tri_ref[...]
    before = _dot(jnp.where(eq, 1.0, 0.0).astype(BF16), tri)
    sel = gt | (eq & (before < need))
    pos = _dot(jnp.where(sel, 1.0, 0.0).astype(BF16), tri)
    slot_ref[...] = jnp.where(sel, pos, -1.0).astype(jnp.int32)


def _select(logits, cap):
    bsz, ne, n = logits.shape
    blk = pl.BlockSpec((bsz * ne, n), lambda i: (0, 0))
    slot, aff = pl.pallas_call(
        functools.partial(_select_kernel, ne=ne, cap=cap),
        grid=(1,),
        in_specs=[blk],
        out_specs=[blk, blk],
        out_shape=[jax.ShapeDtypeStruct((bsz * ne, n), jnp.int32),
                   jax.ShapeDtypeStruct((bsz * ne, n), F32)],
        scratch_shapes=[pltpu.VMEM((n, n), BF16)],
        compiler_params=_params("arbitrary"),
        name="moe_select",
    )(logits.reshape(bsz * ne, n))
    return slot.reshape(bsz, ne, n), aff.reshape(bsz, ne, n)


GATHER_EXPERTS = 8


def _gather_kernel(h_ref, slot_ref, aff_ref, xe_ref, gate_ref, onehot_ref, *, cap):
    n = h_ref.shape[1]
    pi = lax.broadcasted_iota(jnp.int32, (cap, n), 0)
    for e in range(GATHER_EXPERTS):
        hit = slot_ref[0, e:e + 1, :] == pi
        onehot_ref[e * cap:(e + 1) * cap, :] = jnp.where(hit, 1.0, 0.0).astype(BF16)
        gate_ref[e] = jnp.sum(jnp.where(hit, aff_ref[0, e:e + 1, :], 0.0), axis=1, keepdims=True)
    xe = _dot(onehot_ref[...], h_ref[0]).astype(BF16)
    xe_ref[...] = xe.reshape(GATHER_EXPERTS, cap, xe.shape[1])


def _gather(h, slot, aff, cap):
    bsz, n, d = h.shape
    ne = slot.shape[1]
    ge = GATHER_EXPERTS
    return pl.pallas_call(
        functools.partial(_gather_kernel, cap=cap),
        grid=(bsz, ne // ge),
        in_specs=[pl.BlockSpec((1, n, d), lambda b, j: (b, 0, 0)),
                  pl.BlockSpec((1, ge, n), lambda b, j: (b, j, 0)),
                  pl.BlockSpec((1, ge, n), lambda b, j: (b, j, 0))],
        out_specs=[pl.BlockSpec((ge, cap, d), lambda b, j: (j, b, 0)),
                   pl.BlockSpec((ge, cap, 1), lambda b, j: (j, b, 0))],
        out_shape=[jax.ShapeDtypeStruct((ne, bsz * cap, d), BF16),
                   jax.ShapeDtypeStruct((ne, bsz * cap, 1), F32)],
        scratch_shapes=[pltpu.VMEM((ge * cap, n), BF16)],
        compiler_params=_params("arbitrary", "arbitrary"),
        name="moe_gather",
    )(h, slot, aff)


def _ffn_kernel(*refs, nseg, rows_per_pass):
    x_refs = refs[:nseg]
    gate_refs = refs[nseg:2 * nseg]
    wg_ref, wu_ref, wd_ref = refs[2 * nseg:2 * nseg + 3]
    y_refs = refs[2 * nseg + 3:3 * nseg + 3]
    acc_refs = refs[3 * nseg + 3:]
    f = pl.program_id(1)

    @pl.when(f == 0)
    def _():
        for acc_ref in acc_refs:
            acc_ref[...] = jnp.zeros_like(acc_ref)

    wg = wg_ref[0].astype(BF16)
    wu = wu_ref[0].astype(BF16)
    wd = wd_ref[0].astype(BF16)
    for x_ref, acc_ref in zip(x_refs, acc_refs):
        m = x_ref.shape[1]
        step = min(rows_per_pass, m)
        for r in range(0, m, step):
            x = x_ref[0, r:r + step, :]
            hid = (_silu(_dot(x, wg)) * _dot(x, wu)).astype(BF16)
            acc_ref[r:r + step, :] += _dot(hid, wd)

    @pl.when(f == pl.num_programs(1) - 1)
    def _():
        for y_ref, gate_ref, acc_ref in zip(y_refs, gate_refs, acc_refs):
            y_ref[0] = (acc_ref[...] * gate_ref[0]).astype(BF16)


def _ffn(xes, gates, w_gate, w_up, w_down, layer):
    _, ne, d, ff = w_gate.shape
    tf = 256
    nseg = len(xes)
    return pl.pallas_call(
        functools.partial(_ffn_kernel, nseg=nseg, rows_per_pass=512),
        grid=(ne, ff // tf),
        in_specs=[pl.BlockSpec((1, xe.shape[1], d), lambda e, f: (e, 0, 0)) for xe in xes]
        + [pl.BlockSpec((1, xe.shape[1], 1), lambda e, f: (e, 0, 0)) for xe in xes]
        + [pl.BlockSpec((None, 1, d, tf), lambda e, f: (layer, e, 0, f)),
           pl.BlockSpec((None, 1, d, tf), lambda e, f: (layer, e, 0, f)),
           pl.BlockSpec((None, 1, tf, d), lambda e, f: (layer, e, f, 0))],
        out_specs=[pl.BlockSpec((1, xe.shape[1], d), lambda e, f: (e, 0, 0)) for xe in xes],
        out_shape=[jax.ShapeDtypeStruct(xe.shape, BF16) for xe in xes],
        scratch_shapes=[pltpu.VMEM(xe.shape[1:], F32) for xe in xes],
        compiler_params=_params("arbitrary", "arbitrary"),
        name="moe_ffn",
    )(*xes, *gates, w_gate, w_up, w_down)


def _combine_kernel(*refs, cap, final_norm):
    slot_ref, ye_ref, x_ref, mod_ref = refs[:4]
    if final_norm:
        fg_ref, o_ref, onehot_ref = refs[4:]
    else:
        o_ref, onehot_ref = refs[4:]
    ne = ye_ref.shape[0]
    if cap % LANES == 0:
        pi = lax.broadcasted_iota(jnp.int32, (1, cap), 1)
        for e in range(ne):
            hit = slot_ref[0, :, e:e + 1] == pi
            onehot_ref[:, e * cap:(e + 1) * cap] = jnp.where(hit, 1.0, 0.0).astype(BF16)
    else:
        ei = lax.broadcasted_iota(jnp.int32, (ne, ne * cap), 0)
        li = lax.broadcasted_iota(jnp.int32, (ne, ne * cap), 1)
        spread = jnp.where((li >= ei * cap) & (li < (ei + 1) * cap), 1.0, 0.0).astype(BF16)
        slots = _dot(slot_ref[0].astype(F32).astype(BF16), spread)
        group = jnp.sum(jnp.where(li >= (ei + 1) * cap, 1, 0), axis=0, keepdims=True)
        pos = (li[0:1] - group * cap).astype(F32)
        onehot_ref[...] = jnp.where(slots == pos, 1.0, 0.0).astype(BF16)
    ye = ye_ref[...]
    moe = _dot(onehot_ref[...], ye.reshape(ne * cap, ye.shape[2]))
    y = x_ref[0] + mod_ref[0, 0][5:6] * moe
    if final_norm:
        y = _rms(y) * fg_ref[...]
    o_ref[0] = y


def _combine(slot_t, ye, x, mod, group, cap, final_g):
    bsz, n, d = x.shape
    ne = slot_t.shape[2]
    tm = min(n, 1024)
    final_norm = final_g is not None
    in_specs = [pl.BlockSpec((1, tm, ne), lambda b, i: (b, i, 0)),
                pl.BlockSpec((ne, cap, d), lambda b, i: (0, b, 0)),
                pl.BlockSpec((1, tm, d), lambda b, i: (b, i, 0)),
                _mod_spec(d, group)]
    args = [slot_t, ye, x, mod]
    if final_norm:
        in_specs.append(pl.BlockSpec((1, d), lambda b, i: (0, 0)))
        args.append(final_g.reshape(1, d))
    return pl.pallas_call(
        functools.partial(_combine_kernel, cap=cap, final_norm=final_norm),
        grid=(bsz, n // tm),
        in_specs=in_specs,
        out_specs=pl.BlockSpec((1, tm, d), lambda b, i: (b, i, 0)),
        out_shape=jax.ShapeDtypeStruct((bsz, n, d), F32),
        scratch_shapes=[pltpu.VMEM((tm, ne * cap), BF16)],
        compiler_params=_params("arbitrary", "arbitrary"),
        name="moe_combine",
    )(*args)


def _moe(streams, groups, mod, w_gate, w_up, w_down, layer, final_g=None):
    xs = [s[0] for s in streams]
    routed = []
    for x, h, logits in streams:
        cap = EC_CAPACITY_FACTOR * x.shape[1] // logits.shape[1]
        slot, aff = _select(logits, cap)
        xe, gate = _gather(h, slot, aff, cap)
        routed.append((cap, slot, xe, gate))
    yes = _ffn([r[2] for r in routed], [r[3] for r in routed], w_gate, w_up, w_down, layer)
    return [_combine(jnp.swapaxes(slot, 1, 2), ye, x, mod, group, cap, final_g)
            for x, group, ye, (cap, slot, _, _) in zip(xs, groups, yes, routed)]


def _chunk_masks(tm, direction):
    ri = lax.broadcasted_iota(jnp.int32, (tm, tm), 0)
    ci = lax.broadcasted_iota(jnp.int32, (tm, tm), 1)
    same = jnp.right_shift(ri, 6) == jnp.right_shift(ci, 6)
    return same & ((ci <= ri) if direction == 0 else (ci >= ri))


def _chunk_logdecay(la, direction):
    cum = jnp.where(_chunk_masks(la.shape[0], direction), 1.0, 0.0).astype(BF16)
    hi, lo = _split2(la * LOG2E)
    return _dot(cum, hi) + _dot(cum, lo)


def _scan_operands(q, k, b, direction, refs):
    q_in_ref, q_mid_ref, k_mid_ref, k_out_ref, decay_ref = refs
    tm, n = q.shape
    nc = tm // CHUNK
    b = b.reshape(nc, CHUNK, n)
    last, mid = (CHUNK - 1, CHUNK // 2) if direction == 0 else (0, CHUNK // 2 - 1)
    b_tot = b[:, last:last + 1]
    b_mid = b[:, mid:mid + 1]
    q = q.reshape(nc, CHUNK, n)
    k = k.reshape(nc, CHUNK, n)
    q_in_ref[direction, 0] = (q * jnp.exp2(b)).astype(BF16).reshape(tm, n)
    q_mid_ref[direction, 0] = (q * jnp.exp2(b - b_mid)).astype(BF16).reshape(tm, n)
    k_mid_ref[direction, 0] = (k * jnp.exp2(b_mid - b)).astype(BF16).reshape(tm, n)
    k_out_ref[direction, 0] = (k * jnp.exp2(b_tot - b)).astype(BF16).reshape(tm, n)
    decay = jnp.exp2(b_tot).reshape(nc, n)
    decay_ref[direction, 0, 0] = jnp.concatenate([decay] * (8 // nc), axis=0)


def _rec_proj_kernel(xc_ref, xl_ref, mod_ref, g_ref, w_ref, gkup_ref, gkb_ref, lbl_ref, *out_refs, layer):
    gla_refs, gv_ref, gg_ref = out_refs[0:5], out_refs[5], out_refs[6]
    hg_refs, hv_ref, hg_ref = out_refs[7:12], out_refs[12], out_refs[13]
    mod = mod_ref[0, 0]
    h = _modulate(_stream_block(xc_ref, xl_ref), g_ref[...], mod[0:1], mod[1:2]).astype(BF16)
    n = GLA_HEADS * LANES
    c0 = 2 * n + LANES
    c1 = c0 + 3 * n
    head = _dot(h, w_ref[:, 0:c0])
    gate = _dot(head[:, 2 * n:c0].astype(BF16), gkup_ref[...]) + gkb_ref[...]
    qk = _dot(h, w_ref[:, c0:c1])
    lbl = lbl_ref[...]
    e = jnp.exp(lbl - jnp.max(lbl, axis=0, keepdims=True))
    sm = e / jnp.sum(e, axis=0, keepdims=True)
    lb = jnp.sum(sm[0:layer + 1], axis=0, keepdims=True) - sm[0:1]
    f = lb + (1.0 - lb) * jax.nn.sigmoid(head[:, 0:2 * n])
    fs = [f[:, dd * n:(dd + 1) * n] for dd in range(2)]
    rest = _dot(h, w_ref[:, c1:])
    b_hg = [_chunk_logdecay(jnp.log(fs[dd]), dd) for dd in range(2)]
    b_gla = [_chunk_logdecay(jax.nn.log_sigmoid(gate[:, dd * n:(dd + 1) * n]) / GLA_GATE_NORM, dd)
             for dd in range(2)]
    hq = _silu(qk[:, 2 * n:3 * n])
    gq = qk[:, 0:n] * GLA_DK ** -0.5
    gk = qk[:, n:2 * n]
    for dd in range(2):
        _scan_operands(hq, 1.0 - fs[dd], b_hg[dd], dd, hg_refs)
    for dd in range(2):
        _scan_operands(gq, gk, b_gla[dd], dd, gla_refs)
    gv_ref[0] = rest[:, 0:n].astype(BF16)
    gg_ref[0] = rest[:, n:2 * n]
    hv_ref[0] = rest[:, 2 * n:3 * n].astype(BF16)
    hg_ref[0] = rest[:, 3 * n:4 * n]


def _pad_heads(w, heads, dk):
    lead = w.shape[:-1]
    w3 = w.reshape(lead + (heads, dk))
    return jnp.pad(w3, [(0, 0)] * len(lead) + [(0, 0), (0, LANES - dk)]).reshape(lead + (heads * LANES,))


def _rec_project(xc, xl, mod, g, w_in, gk_up, gk_bias, lb_logits, layer):
    bsz, n_ctx, d = xc.shape
    t = n_ctx + xl.shape[1]
    tm = ROW_TILE
    hk = GLA_HEADS * GLA_DK
    hv = GLA_HEADS * GLA_DV
    hf = HG_HEADS * HG_DF
    sizes = (hk, hk, hv, hv, 2 * GLA_GATE_RANK, hf, 2 * hf, HG_HEADS * HG_DV, HG_HEADS * HG_DV)
    offs = [0]
    for s in sizes:
        offs.append(offs[-1] + s)
    gq, gk, gv, gg, gdown, hq, hff, hi, hgt = [w_in[:, offs[i]:offs[i + 1]] for i in range(9)]
    w = jnp.concatenate([hff, jnp.pad(gdown, ((0, 0), (0, LANES - 2 * GLA_GATE_RANK))),
                         _pad_heads(gq, GLA_HEADS, GLA_DK), _pad_heads(gk, GLA_HEADS, GLA_DK), hq,
                         gv, gg, hi, hgt], axis=1).astype(BF16)
    n = GLA_HEADS * LANES
    ups = [_pad_heads(gk_up[dd], GLA_HEADS, GLA_DK) for dd in range(2)]
    zero = jnp.zeros_like(ups[0])
    gkup = jnp.concatenate([jnp.concatenate([ups[0], zero], axis=1), jnp.concatenate([zero, ups[1]], axis=1),
                            jnp.zeros((LANES - 2 * GLA_GATE_RANK, 2 * n), F32)], axis=0).astype(BF16)
    gkb = jnp.concatenate([_pad_heads(gk_bias[dd], GLA_HEADS, GLA_DK) for dd in range(2)]).reshape(1, 2 * n)
    row = lambda b, i: (b, i, 0)
    drow = lambda b, i: (0, b, i, 0)
    const = lambda b, i: (0, 0)
    one = lambda dt: (pl.BlockSpec((1, tm, n), row), jax.ShapeDtypeStruct((bsz, t, n), dt))
    two = (pl.BlockSpec((2, 1, tm, n), drow), jax.ShapeDtypeStruct((2, bsz, t, n), BF16))
    dec = (pl.BlockSpec((2, 1, 1, 8, n), lambda b, i: (0, b, i, 0, 0)),
           jax.ShapeDtypeStruct((2, bsz, t // tm, 8, n), F32))
    mixer = [two, two, two, two, dec, one(BF16), one(F32)]
    outs = mixer + mixer
    return pl.pallas_call(
        functools.partial(_rec_proj_kernel, layer=layer),
        grid=(bsz, t // tm),
        in_specs=_stream_specs(d) + [
            _mod_spec(d),
            pl.BlockSpec((1, d), const),
            pl.BlockSpec(w.shape, const),
            pl.BlockSpec(gkup.shape, const),
            pl.BlockSpec(gkb.shape, const),
            pl.BlockSpec((lb_logits.shape[0], 2 * hf), const)],
        out_specs=[o[0] for o in outs],
        out_shape=[o[1] for o in outs],
        compiler_params=_params("arbitrary", "arbitrary"),
        name="rec_project",
    )(xc, xl, mod, g.reshape(1, d), w, gkup, gkb, lb_logits.reshape(lb_logits.shape[0], 2 * hf))


def _scan_kernel(*refs, heads):
    ins = [refs[0:6], refs[6:12]]
    o_refs = refs[12:14]
    st_ref = refs[14]
    tm = o_refs[0].shape[1]
    nc = tm // CHUNK

    @pl.when(pl.program_id(1) == 0)
    def _():
        st_ref[...] = jnp.zeros_like(st_ref)

    lanes = [slice(LANES * h, LANES * (h + 1)) for h in range(heads)]
    chunk_rows = [slice(c * CHUNK, (c + 1) * CHUNK) for c in range(nc)]
    units = [(d, h) for d in range(2) for h in range(heads)]
    vs = {(d, h): ins[d][5][0, :, lanes[h]] for d, h in units}
    scores = {(d, h): _dot_nt(ins[d][1][0, 0, :, lanes[h]], ins[d][2][0, 0, :, lanes[h]]) for d, h in units}
    updates = {(d, h): [_dot_tn(vs[d, h][rows], ins[d][3][0, 0, rows, lanes[h]]) for rows in chunk_rows]
               for d, h in units}
    causal = [_chunk_masks(tm, d) for d in range(2)]
    intra = {(d, h): _dot(jnp.where(causal[d], scores[d, h], 0.0).astype(BF16), vs[d, h]) for d, h in units}
    order = [list(range(nc)), list(range(nc - 1, -1, -1))]
    entering = {}
    for d, h in units:
        st = st_ref[d, h]
        for c in order[d]:
            entering[d, h, c] = st.astype(BF16)
            st = st * ins[d][4][0, 0, 0, c:c + 1, lanes[h]] + updates[d, h][c]
        st_ref[d, h] = st
    for d, h in units:
        for c in order[d]:
            rows = chunk_rows[c]
            inter = _dot_nt(ins[d][0][0, 0, rows, lanes[h]], entering[d, h, c])
            o_refs[d][0, rows, lanes[h]] = intra[d, h][rows] + inter


def _scan(q_in, q_mid, k_mid, k_out, decay, v, n_ctx):
    _, bsz, t, n = q_in.shape
    heads = n // LANES
    tm = ROW_TILE
    nb = t // tm
    ncb = n_ctx // tm

    def blk(d, s):
        return s if d == 0 else jnp.where(s < ncb, ncb - 1 - s, nb - 1 - (s - ncb))

    in_specs, args = [], []
    for d in range(2):
        drow = lambda b, s, d=d: (d, b, blk(d, s), 0)
        in_specs += [pl.BlockSpec((1, 1, tm, n), drow)] * 4
        in_specs += [pl.BlockSpec((1, 1, 1, 8, n), lambda b, s, d=d: (d, b, blk(d, s), 0, 0)),
                     pl.BlockSpec((1, tm, n), lambda b, s, d=d: (b, blk(d, s), 0))]
        args += [q_in, q_mid, k_mid, k_out, decay, v]
    return pl.pallas_call(
        functools.partial(_scan_kernel, heads=heads),
        grid=(bsz, nb),
        in_specs=in_specs,
        out_specs=[pl.BlockSpec((1, tm, n), lambda b, s, d=d: (b, blk(d, s), 0)) for d in range(2)],
        out_shape=[jax.ShapeDtypeStruct((bsz, t, n), F32)] * 2,
        scratch_shapes=[pltpu.VMEM((2, heads, LANES, LANES), F32)],
        compiler_params=_params("arbitrary", "arbitrary"),
        name="rec_scan",
    )(*args)


def _rec_out_kernel(og0_ref, og1_ref, oh0_ref, oh1_ref, gg_ref, hg_ref, gn_ref, hn_ref, wout_ref,
                    x_ref, mod_ref, g2_ref, wrt_ref, o_ref, h_ref, lg_ref):
    parts = []
    for fwd_ref, bwd_ref, gate_ref, n_ref in ((og0_ref, og1_ref, gg_ref, gn_ref),
                                              (oh0_ref, oh1_ref, hg_ref, hn_ref)):
        for h in range(fwd_ref.shape[2] // LANES):
            sl = slice(LANES * h, LANES * (h + 1))
            o = fwd_ref[0, :, sl] + bwd_ref[0, :, sl]
            parts.append(_rms(o) * n_ref[...] * _silu(gate_ref[0, :, sl]))
    a = jnp.concatenate(parts, axis=1).astype(BF16)
    mod = mod_ref[0, 0]
    y = x_ref[0] + mod[2:3] * _dot(a, wout_ref[...])
    o_ref[0] = y
    _route_rows(y, mod, g2_ref, wrt_ref, h_ref, lg_ref)


def _rec_readout(og, oh, gg, hg, gn, hn, w_out, xl, mod, g2, w_router, n_ctx):
    bsz, seq, d = xl.shape
    tm = ROW_TILE
    ncb = n_ctx // tm
    n = gg.shape[2]
    row = lambda b, i: (b, i + ncb, 0)
    const = lambda b, i: (0, 0)
    wout = w_out.astype(BF16)
    r_in, r_out, r_shape = _route_specs(bsz, seq, d, w_router.shape[1], tm)
    return pl.pallas_call(
        _rec_out_kernel,
        grid=(bsz, seq // tm),
        in_specs=[pl.BlockSpec((1, tm, n), row),
                  pl.BlockSpec((1, tm, n), row),
                  pl.BlockSpec((1, tm, n), row),
                  pl.BlockSpec((1, tm, n), row),
                  pl.BlockSpec((1, tm, n), row),
                  pl.BlockSpec((1, tm, n), row),
                  pl.BlockSpec((1, LANES), const),
                  pl.BlockSpec((1, LANES), const),
                  pl.BlockSpec(wout.shape, const),
                  pl.BlockSpec((1, tm, d), lambda b, i: (b, i, 0)),
                  _mod_spec(d, 1)] + r_in,
        out_specs=[pl.BlockSpec((1, tm, d), lambda b, i: (b, i, 0))] + r_out,
        out_shape=[jax.ShapeDtypeStruct((bsz, seq, d), F32)] + r_shape,
        compiler_params=_params("arbitrary", "arbitrary"),
        name="rec_readout",
    )(*og, *oh, gg, hg, gn.reshape(1, -1), hn.reshape(1, -1), wout, xl, mod, g2.reshape(1, d), w_router.T)


def kernel(x, c, ctx, c_ctx, ada_w, ada_b, norm1_g, norm2_g, att_w_in, mla_q_norm_g, mla_q_up, mla_kv_norm_g, mla_kv_up, da_lam_q1, da_lam_k1, da_lam_q2, da_lam_k2, da_subln_g, att_w_out, rec_w_in, gla_gk_up, gla_gk_bias, gla_norm_g, hg_lb_logits, hg_norm_g, rec_w_out, moe_router, moe_w_gate, moe_w_up, moe_w_down, final_norm_g):
    bsz, seq, d = x.shape
    n_ctx = ctx.shape[1]
    depth = ada_w.shape[0]
    assert depth == 2 and n_ctx == ROW_TILE and seq % ROW_TILE == 0 and bsz + 1 <= 16

    cvec = jnp.concatenate([c, c_ctx[None, :], jnp.zeros((16 - bsz - 1, d), F32)], axis=0)
    ada = _ada_vectors(cvec, ada_w, ada_b).reshape(depth, 16, 6, d)
    ada = jnp.pad(ada, ((0, 0), (0, 0), (0, 2), (0, 0)))
    mods = jnp.stack([jnp.broadcast_to(ada[:, bsz][:, None], (depth, bsz, 8, d)), ada[:, :bsz]], axis=2)
    xc, xl = ctx, x

    mod = mods[0]
    tabs_da = _rope_tables(n_ctx, seq, DA_QK, 0, DA_QK)
    tabs_m = _rope_tables(n_ctx, seq, MLA_ROPE, MLA_NOPE, LANES)
    proj = _attn_project(xc, xl, mod, norm1_g[0], att_w_in[0], mla_q_norm_g[0], mla_q_up[0],
                         mla_kv_norm_g[0], mla_kv_up[0], tabs_da, tabs_m)
    lamv = jnp.stack([da_lam_q1[0], da_lam_k1[0], da_lam_q2[0], da_lam_k2[0]])
    lam_init = 0.8 - 0.6 * math.exp(-0.3 * 0)
    sc = _attention(*proj, lamv, da_subln_g[0], att_w_out[0], xc, mod, norm2_g[0], moe_router[0], 0, 0, n_ctx,
                    lam_init)
    sl = _attention(*proj, lamv, da_subln_g[0], att_w_out[0], xl, mod, norm2_g[0], moe_router[0], 1,
                    n_ctx // ROW_TILE, n_ctx + seq, lam_init)
    xc, xl = _moe([sc, sl], [0, 1], mod, moe_w_gate, moe_w_up, moe_w_down, 0)

    mod = mods[1]
    rec = _rec_project(xc, xl, mod, norm1_g[1], rec_w_in[0], gla_gk_up[0], gla_gk_bias[0], hg_lb_logits, 1)
    og = _scan(*rec[0:6], n_ctx)
    oh = _scan(*rec[7:13], n_ctx)
    sl = _rec_readout(og, oh, rec[6], rec[13], gla_norm_g[0], hg_norm_g[0], rec_w_out[0], xl, mod, norm2_g[1],
                      moe_router[1], n_ctx)
    (out,) = _moe([sl], [1], mod, moe_w_gate, moe_w_up, moe_w_down, 1, final_g=final_norm_g)
    return out
```

```python
import functools
import math

import jax
import jax.numpy as jnp
from jax import lax
from jax.experimental import pallas as pl
from jax.experimental.pallas import tpu as pltpu

F32 = jnp.float32
BF16 = jnp.bfloat16

EPS = 1e-6
GRID_W = 64
ROPE_BASE = 10000.0

DA_HEADS = 4
DA_QK = 64
DA_V = 2 * DA_QK
DA_SCALE = DA_QK ** -0.5
MLA_HEADS = 8
MLA_NOPE = 64
MLA_ROPE = 32
MLA_V = 64
MLA_Q_RANK = 384
MLA_KV_RANK = 256
MLA_SCALE = (MLA_NOPE + MLA_ROPE) ** -0.5
LOG2E = math.log2(math.e)

GLA_HEADS = 4
GLA_DK = 64
GLA_DV = 128
GLA_GATE_RANK = 16
GLA_GATE_NORM = 16.0
HG_HEADS = 4
HG_DF = 128
HG_DV = 128
CHUNK = 64

N_EXPERTS = 16
EC_CAPACITY_FACTOR = 2

LANES = 128
ROW_TILE = 256
ATTN_KEY_BLOCK = 256
SUM_ROWS = 16
ATTN_ROUTER_UNIT = 12
ATTN_LOOKAHEAD = 4
VMEM_LIMIT = 56 * 1024 * 1024


def _params(*sem):
    return pltpu.CompilerParams(dimension_semantics=sem, vmem_limit_bytes=VMEM_LIMIT)


def _dot(a, b):
    return jnp.dot(a, b, preferred_element_type=F32)


def _dot_nt(a, b):
    return lax.dot_general(a, b, (((1,), (1,)), ((), ())), preferred_element_type=F32)


def _dot_tn(a, b):
    return lax.dot_general(a, b, (((0,), (0,)), ((), ())), preferred_element_type=F32)


def _split2(a):
    hi = a.astype(BF16)
    lo = (a - hi.astype(F32)).astype(BF16)
    return hi, lo


def _dot3(a, b, dot=_dot):
    ah, al = _split2(a)
    bh, bl = _split2(b)
    return dot(ah, bh) + (dot(ah, bl) + dot(al, bh))


def _rms(x):
    return x * lax.rsqrt(jnp.mean(x * x, axis=-1, keepdims=True) + EPS)


def _silu(x):
    return x * jax.nn.sigmoid(x)


def _modulate(x, g, shift, scale):
    return _rms(x) * g * (1.0 + scale) + shift


def _stream_specs(d):
    return [pl.BlockSpec((1, ROW_TILE, d), lambda b, i: (b, 0, 0)),
            pl.BlockSpec((1, ROW_TILE, d), lambda b, i: (b, jnp.maximum(i - 1, 0), 0))]


def _stream_block(xc_ref, xl_ref):
    return jnp.where(pl.program_id(1) == 0, xc_ref[0], xl_ref[0])


def _route_rows(y, mod, g2_ref, wrt_ref, h_ref, lg_ref):
    _route_logits(_route_split(y, mod, g2_ref, h_ref), wrt_ref, lg_ref)


def _route_split(y, mod, g2_ref, h_ref):
    hh, hl = _split2(_modulate(y, g2_ref[...], mod[3:4], mod[4:5]))
    h_ref[0] = hh
    return hh, hl


def _route_logits(split, wrt_ref, lg_ref):
    hh, hl = split
    wh, wl = _split2(wrt_ref[...])
    ne = wh.shape[0]
    both = _dot_nt(jnp.concatenate([wh, wl], axis=0), hh)
    lg_ref[0] = both[0:ne] + (_dot_nt(wh, hl) + both[ne:2 * ne])


def _route_specs(bsz, n, d, ne, tm):
    const = lambda b, i: (0, 0)
    return ([pl.BlockSpec((1, d), const), pl.BlockSpec((ne, d), const)],
            [pl.BlockSpec((1, tm, d), lambda b, i: (b, i, 0)), pl.BlockSpec((1, ne, tm), lambda b, i: (b, 0, i))],
            [jax.ShapeDtypeStruct((bsz, n, d), BF16), jax.ShapeDtypeStruct((bsz, ne, n), F32)])


def _mod_spec(d, group=None):
    if group is None:
        return pl.BlockSpec((1, 1, 8, d), lambda b, i: (b, jnp.minimum(i, 1), 0, 0))
    return pl.BlockSpec((1, 1, 8, d), lambda b, i: (b, group, 0, 0))


def _ada_kernel(c_ref, w_ref, b_ref, o_ref):
    o_ref[...] = _dot3(_silu(c_ref[...]), w_ref[...]) + b_ref[...]


def _ada_vectors(cvec, ada_w, ada_b):
    depth, d, n6 = ada_w.shape
    tn = n6 // 4
    rows = cvec.shape[0]
    return pl.pallas_call(
        _ada_kernel,
        grid=(depth, n6 // tn),
        in_specs=[pl.BlockSpec((rows, d), lambda l, j: (0, 0)),
                  pl.BlockSpec((None, d, tn), lambda l, j: (l, 0, j)),
                  pl.BlockSpec((None, 1, tn), lambda l, j: (l, 0, j))],
        out_specs=pl.BlockSpec((None, rows, tn), lambda l, j: (l, 0, j)),
        out_shape=jax.ShapeDtypeStruct((depth, rows, n6), F32),
        compiler_params=_params("arbitrary", "arbitrary"),
        name="ada_vectors",
    )(cvec, ada_w, ada_b.reshape(depth, 1, n6))


def _axial_angles(rows, rot_dim):
    row = jnp.repeat(jnp.arange(rows, dtype=F32), GRID_W)
    col = jnp.tile(jnp.arange(GRID_W, dtype=F32), rows)
    n_freq = rot_dim // 4
    inv = ROPE_BASE ** (-jnp.arange(n_freq, dtype=F32) / n_freq)
    ang = jnp.concatenate([row[:, None] * inv, col[:, None] * inv], axis=-1)
    return jnp.cos(ang), jnp.sin(ang)


def _rope_tables(n_ctx, seq, width, start, group):
    half = width // 2
    cos, sin = _axial_angles(seq // GRID_W, width)
    lane = jnp.arange(LANES) % group - start
    inside = (lane >= 0) & (lane < width)
    idx = jnp.clip(lane, 0, width - 1) % half
    second = inside & (lane >= half)
    firsth = inside & (lane < half)
    c = jnp.where(inside[None, :], cos[:, idx], 1.0)
    s_prev = jnp.where(second[None, :], sin[:, idx], 0.0)
    s_next = jnp.where(firsth[None, :], -sin[:, idx], 0.0)
    ident = jnp.ones((n_ctx, LANES), F32)
    zero = jnp.zeros((n_ctx, LANES), F32)
    return (jnp.concatenate([ident, c], 0), jnp.concatenate([zero, s_prev], 0),
            jnp.concatenate([zero, s_next], 0))


def _rope(x, cos, s_prev, s_next, half):
    return x * cos + pltpu.roll(x, half, 1) * s_prev + pltpu.roll(x, LANES - half, 1) * s_next


def _attn_proj_kernel(xc_ref, xl_ref, mod_ref, g_ref, w_ref, wvt_ref, vab_ref, qn_ref, qup_ref, kvn_ref,
                      kvup_ref, kvupt_ref, vmb_ref, cda_ref, pda_ref, nda_ref, cm_ref, pm_ref, nm_ref,
                      qda_ref, kda_ref, vat_ref, qm_ref, km_ref, vmt_ref):
    mod = mod_ref[0, 0]
    h = _modulate(_stream_block(xc_ref, xl_ref), g_ref[...], mod[0:1], mod[1:2]).astype(BF16)
    proj = _dot(h, w_ref[...])
    vat_ref[0] = (_dot_nt(wvt_ref[...], h) + vab_ref[...]).astype(BF16)
    cda, pda, nda = cda_ref[...], pda_ref[...], nda_ref[...]
    cm, pm, nm = cm_ref[...], pm_ref[...], nm_ref[...]
    n_da = DA_HEADS * LANES
    for hh in range(DA_HEADS):
        sl = slice(LANES * hh, LANES * (hh + 1))
        q = proj[:, LANES * hh:LANES * (hh + 1)]
        k = proj[:, n_da + LANES * hh:n_da + LANES * (hh + 1)]
        qda_ref[0, :, sl] = (_rope(q, cda, pda, nda, DA_QK // 2) * (DA_SCALE * LOG2E)).astype(BF16)
        kda_ref[0, :, sl] = _rope(k, cda, pda, nda, DA_QK // 2).astype(BF16)
    o = 2 * n_da
    cq = proj[:, o:o + MLA_Q_RANK]
    ckv = proj[:, o + MLA_Q_RANK:o + MLA_Q_RANK + MLA_KV_RANK]
    kr = proj[:, o + MLA_Q_RANK + MLA_KV_RANK:o + MLA_Q_RANK + MLA_KV_RANK + LANES]
    qm = _dot((_rms(cq) * qn_ref[...]).astype(BF16), qup_ref[...])
    ckvn = (_rms(ckv) * kvn_ref[...]).astype(BF16)
    kv = _dot(ckvn, kvup_ref[...])
    vmt_ref[0] = (_dot_nt(kvupt_ref[...], ckvn) + vmb_ref[...]).astype(BF16)
    rep = kr + pltpu.roll(kr, 32, 1) + pltpu.roll(kr, 64, 1) + pltpu.roll(kr, 96, 1)
    lane = lax.broadcasted_iota(jnp.int32, (1, LANES), 1)
    rep = jnp.where((lane >= MLA_NOPE) & (lane < MLA_NOPE + MLA_ROPE), rep, 0.0)
    rep = _rope(rep, cm, pm, nm, MLA_ROPE // 2)
    for hh in range(MLA_HEADS):
        sl = slice(LANES * hh, LANES * (hh + 1))
        qm_ref[0, :, sl] = (_rope(qm[:, sl], cm, pm, nm, MLA_ROPE // 2) * (MLA_SCALE * LOG2E)).astype(BF16)
        km_ref[0, :, sl] = (kv[:, sl] + rep).astype(BF16)


def _with_sum_rows(wt, heads):
    dv = wt.shape[0] // heads
    w3 = jnp.pad(wt.reshape(heads, dv, -1), ((0, 0), (0, SUM_ROWS), (0, 0)))
    bias = jnp.pad(jnp.zeros((heads, dv, 1), F32), ((0, 0), (0, SUM_ROWS), (0, 0)), constant_values=1.0)
    return w3.reshape(heads * (dv + SUM_ROWS), -1).astype(BF16), bias.reshape(heads * (dv + SUM_ROWS), 1)


def _attn_project(xc, xl, mod, g, w_in, qn_g, q_up, kvn_g, kv_up, tabs_da, tabs_m):
    bsz, n_ctx, d = xc.shape
    t = n_ctx + xl.shape[1]
    tm = ROW_TILE
    n_da = DA_HEADS * LANES
    nm = MLA_HEADS * LANES
    nv = MLA_HEADS * MLA_V
    w = jnp.concatenate([w_in[:, :2 * n_da], w_in[:, 3 * n_da:]], axis=1)
    w = jnp.pad(w, ((0, 0), (0, (-w.shape[1]) % LANES))).astype(BF16)
    wvt, vab = _with_sum_rows(w_in[:, 2 * n_da:3 * n_da].T, DA_HEADS)
    qup = jnp.pad(q_up.reshape(MLA_Q_RANK, MLA_HEADS, MLA_NOPE + MLA_ROPE),
                  ((0, 0), (0, 0), (0, LANES - MLA_NOPE - MLA_ROPE))).reshape(MLA_Q_RANK, -1).astype(BF16)
    kv3 = kv_up.reshape(MLA_KV_RANK, MLA_HEADS, MLA_NOPE + MLA_V)
    kvup = jnp.pad(kv3[:, :, :MLA_NOPE],
                   ((0, 0), (0, 0), (0, LANES - MLA_NOPE))).reshape(MLA_KV_RANK, -1).astype(BF16)
    kvupt, vmb = _with_sum_rows(kv3[:, :, MLA_NOPE:].reshape(MLA_KV_RANK, -1).T, MLA_HEADS)
    row = lambda b, i: (b, i, 0)
    col = lambda b, i: (b, 0, i)
    const = lambda b, i: (0, 0)
    tab = pl.BlockSpec((tm, LANES), lambda b, i: (i, 0))
    rows_out = lambda n: (pl.BlockSpec((1, tm, n), row), jax.ShapeDtypeStruct((bsz, t, n), BF16))
    cols_out = lambda n: (pl.BlockSpec((1, n, tm), col), jax.ShapeDtypeStruct((bsz, n, t), BF16))
    outs = [rows_out(n_da), rows_out(n_da), cols_out(wvt.shape[0]), rows_out(nm), rows_out(nm),
            cols_out(kvupt.shape[0])]
    return pl.pallas_call(
        _attn_proj_kernel,
        grid=(bsz, t // tm),
        in_specs=_stream_specs(d) + [
            _mod_spec(d),
            pl.BlockSpec((1, d), const),
            pl.BlockSpec(w.shape, const),
            pl.BlockSpec(wvt.shape, const),
            pl.BlockSpec(vab.shape, const),
            pl.BlockSpec((1, MLA_Q_RANK), const),
            pl.BlockSpec(qup.shape, const),
            pl.BlockSpec((1, MLA_KV_RANK), const),
            pl.BlockSpec(kvup.shape, const),
            pl.BlockSpec(kvupt.shape, const),
            pl.BlockSpec(vmb.shape, const),
            tab, tab, tab, tab, tab, tab],
        out_specs=[o[0] for o in outs],
        out_shape=[o[1] for o in outs],
        compiler_params=_params("arbitrary", "arbitrary"),
        name="attn_project",
    )(xc, xl, mod, g.reshape(1, d), w, wvt, vab, qn_g.reshape(1, -1), qup, kvn_g.reshape(1, -1), kvup, kvupt,
      vmb, *tabs_da, *tabs_m)


def _reduce_rows(x, fn):
    n, w = x.shape
    slab = 256
    if n > slab and n % slab == 0:
        x = fn(x.reshape(n // slab, slab, w), axis=0)
        n = slab
    if n > 32 and n % 32 == 0:
        x = fn(x.reshape(n // 32, 32, w), axis=0)
    return fn(x, axis=0, keepdims=True)


def _softmax_parts(s):
    e = jnp.exp2(s - _reduce_rows(s, jnp.max))
    return e.astype(BF16), 1.0 / _reduce_rows(e, jnp.sum)


def _attn_kernel(qda_ref, qm_ref, kda_ref, vat_ref, km_ref, vmt_ref, lam_ref, subg_ref, wout_ref,
                 x_ref, mod_ref, g2_ref, wrt_ref, o_ref, h_ref, lg_ref, att_ref, *, lam_init):
    @pl.when(pl.program_id(0) == 0)
    def _():
        att_ref[...] = jnp.zeros_like(att_ref)

    mod = mod_ref[0, 0]
    y = x_ref[0] + mod[2:3] * _dot_tn(att_ref[...], wout_ref[...])
    o_ref[0] = y
    route = _route_split(y, mod, g2_ref, h_ref)

    lamv = lam_ref[...]
    lam = (jnp.exp(jnp.sum(lamv[0:1] * lamv[1:2], axis=1, keepdims=True))
           - jnp.exp(jnp.sum(lamv[2:3] * lamv[3:4], axis=1, keepdims=True)) + lam_init)
    first = lax.broadcasted_iota(jnp.int32, (1, LANES), 1) < LANES // 2

    n_maps = 2 * DA_HEADS + MLA_HEADS
    maps = []
    for i in range(2 * DA_HEADS):
        sl = slice(LANES * (i // 2), LANES * (i // 2 + 1))
        q = qda_ref[0, :, sl].astype(F32)
        q = jnp.where(first, q, 0.0) if i % 2 == 0 else jnp.where(first, 0.0, q)
        rows = DA_V + SUM_ROWS
        maps.append((q.astype(BF16), kda_ref, sl, vat_ref, slice(rows * (i // 2), rows * (i // 2 + 1)), DA_V))
    for i in range(MLA_HEADS):
        sl = slice(LANES * i, LANES * (i + 1))
        rows = MLA_V + SUM_ROWS
        maps.append((qm_ref[0, :, sl], km_ref, sl, vmt_ref, slice(rows * i, rows * (i + 1)), MLA_V))

    kb = min(ATTN_KEY_BLOCK, kda_ref.shape[1])
    n_blocks = kda_ref.shape[1] // kb

    units = [(i, j) for i in range(n_maps) for j in range(n_blocks)]

    def scores(u):
        i, j = units[u]
        q, k_ref, ksl = maps[i][:3]
        return _dot_nt(k_ref[0, j * kb:(j + 1) * kb, ksl], q)

    pending = [scores(u) for u in range(min(ATTN_LOOKAHEAD, len(units)))]
    heads = []
    for u, (i, j) in enumerate(units):
        s = pending.pop(0)
        if u + ATTN_LOOKAHEAD < len(units):
            pending.append(scores(u + ATTN_LOOKAHEAD))
        if u == min(ATTN_ROUTER_UNIT, len(units) - 1):
            _route_logits(route, wrt_ref, lg_ref)
        v_ref, vsl, dv = maps[i][3:]
        vt = v_ref[0, vsl, j * kb:(j + 1) * kb]
        mb = _reduce_rows(s, jnp.max)
        if j == 0:
            m = mb
            acc = _dot(vt, jnp.exp2((s - m).astype(BF16)))
        else:
            mn = jnp.maximum(m, mb)
            acc = acc * jnp.exp2(m - mn) + _dot(vt, jnp.exp2((s - mn).astype(BF16)))
            m = mn
        if j == n_blocks - 1:
            heads.append(acc[0:dv] * (1.0 / acc[dv:dv + 1]))
    outs = []
    for h in range(DA_HEADS):
        oa = heads[2 * h] - heads[2 * h + 1] * lam
        oa = oa * lax.rsqrt(jnp.mean(oa * oa, axis=0, keepdims=True) + EPS)
        outs.append(oa * subg_ref[...] * (1.0 - lam_init))
    outs.extend(heads[2 * DA_HEADS:])
    att_ref[...] = jnp.concatenate(outs, axis=0).astype(BF16)


def _attention(qda, kda, vat, qm, km, vmt, lamv, subln_g, w_out, x, mod, g2, w_router, group, q_block0,
               n_keys, lam_init):
    bsz, nq, d = x.shape
    tq = ROW_TILE
    per = nq // tq
    n_tiles = bsz * per
    ne = w_router.shape[1]
    cur = lambda s: jnp.minimum(s, n_tiles - 1)
    prev = lambda s: jnp.maximum(s - 1, 0)
    qrow = lambda s: (cur(s) // per, cur(s) % per + q_block0, 0)
    full = lambda s: (cur(s) // per, 0, 0)
    row = lambda s: (prev(s) // per, prev(s) % per, 0)
    const = lambda s: (0, 0)
    wout = w_out.astype(BF16)
    return pl.pallas_call(
        functools.partial(_attn_kernel, lam_init=lam_init),
        grid=(n_tiles + 1,),
        in_specs=[pl.BlockSpec((1, tq, qda.shape[2]), qrow),
                  pl.BlockSpec((1, tq, qm.shape[2]), qrow),
                  pl.BlockSpec((1, n_keys, kda.shape[2]), full),
                  pl.BlockSpec((1, vat.shape[1], n_keys), full),
                  pl.BlockSpec((1, n_keys, km.shape[2]), full),
                  pl.BlockSpec((1, vmt.shape[1], n_keys), full),
                  pl.BlockSpec(lamv.shape, const),
                  pl.BlockSpec((DA_V, 1), const),
                  pl.BlockSpec(wout.shape, const),
                  pl.BlockSpec((1, tq, d), row),
                  pl.BlockSpec((1, 1, 8, d), lambda s: (prev(s) // per, group, 0, 0)),
                  pl.BlockSpec((1, d), const),
                  pl.BlockSpec((ne, d), const)],
        out_specs=[pl.BlockSpec((1, tq, d), row),
                   pl.BlockSpec((1, tq, d), row),
                   pl.BlockSpec((1, ne, tq), lambda s: (prev(s) // per, 0, prev(s) % per))],
        out_shape=[jax.ShapeDtypeStruct((bsz, nq, d), F32),
                   jax.ShapeDtypeStruct((bsz, nq, d), BF16),
                   jax.ShapeDtypeStruct((bsz, ne, nq), F32)],
        scratch_shapes=[pltpu.VMEM((w_out.shape[0], tq), BF16)],
        compiler_params=_params("arbitrary"),
        name="attention",
    )(qda, qm, kda, vat, km, vmt, lamv, subln_g.reshape(-1, 1), wout, x, mod, g2.reshape(1, d), w_router.T)


def _kth_largest(a, k):
    rows = a.shape[0]
    lo0 = jnp.full((rows, 1), -1.0, F32)
    hi0 = jnp.max(a, axis=1, keepdims=True)

    def cond(c):
        return c[2] > 0

    def body(c):
        lo, hi, _ = c
        mid = 0.5 * (lo + hi)
        open_ = (mid > lo) & (mid < hi)
        cnt = jnp.sum(jnp.where(a > mid, 1.0, 0.0), axis=1, keepdims=True)
        up = open_ & (cnt >= k)
        down = open_ & (cnt < k)
        lo = jnp.where(up, mid, lo)
        hi = jnp.where(down, mid, hi)
        return lo, hi, jnp.sum(jnp.where(open_, 1.0, 0.0))

    _, hi, _ = lax.while_loop(cond, body, (lo0, hi0, jnp.float32(1.0)))
    return hi


def _select_kernel(lg_ref, slot_ref, aff_ref, tri_ref, *, ne, cap):
    rows, n = lg_ref.shape
    rt = 256
    for r in range(0, n, rt):
        ri = lax.broadcasted_iota(jnp.int32, (rt, n), 0) + r
        ci = lax.broadcasted_iota(jnp.int32, (rt, n), 1)
        tri_ref[r:r + rt, :] = jnp.where(ri < ci, 1.0, 0.0).astype(BF16)

    lg = lg_ref[...].reshape(rows // ne, ne, n)
    e = jnp.exp(lg - jnp.max(lg, axis=1, keepdims=True))
    aff = (e / jnp.sum(e, axis=1, keepdims=True)).reshape(rows, n)
    aff_ref[...] = aff
    thr = _kth_largest(aff, cap)
    gt = aff > thr
    eq = aff == thr
    need = cap - jnp.sum(jnp.where(gt, 1.0, 0.0), axis=1, keepdims=True)
    tri = tri_ref[...]
    before = _dot(jnp.where(eq, 1.0, 0.0).astype(BF16), tri)
    sel = gt | (eq & (before < need))
    pos = _dot(jnp.where(sel, 1.0, 0.0).astype(BF16), tri)
    slot_ref[...] = jnp.where(sel, pos, -1.0).astype(jnp.int32)


def _select(logits, cap):
    bsz, ne, n = logits.shape
    blk = pl.BlockSpec((bsz * ne, n), lambda i: (0, 0))
    slot, aff = pl.pallas_call(
        functools.partial(_select_kernel, ne=ne, cap=cap),
        grid=(1,),
        in_specs=[blk],
        out_specs=[blk, blk],
        out_shape=[jax.ShapeDtypeStruct((bsz * ne, n), jnp.int32),
                   jax.ShapeDtypeStruct((bsz * ne, n), F32)],
        scratch_shapes=[pltpu.VMEM((n, n), BF16)],
        compiler_params=_params("arbitrary"),
        name="moe_select",
    )(logits.reshape(bsz * ne, n))
    return slot.reshape(bsz, ne, n), aff.reshape(bsz, ne, n)


GATHER_EXPERTS = 8


def _gather_kernel(h_ref, slot_ref, aff_ref, xe_ref, gate_ref, onehot_ref, *, cap):
    n = h_ref.shape[1]
    pi = lax.broadcasted_iota(jnp.int32, (cap, n), 0)
    for e in range(GATHER_EXPERTS):
        hit = slot_ref[0, e:e + 1, :] == pi
        onehot_ref[e * cap:(e + 1) * cap, :] = jnp.where(hit, 1.0, 0.0).astype(BF16)
        gate_ref[e] = jnp.sum(jnp.where(hit, aff_ref[0, e:e + 1, :], 0.0), axis=1, keepdims=True)
    xe = _dot(onehot_ref[...], h_ref[0]).astype(BF16)
    xe_ref[...] = xe.reshape(GATHER_EXPERTS, cap, xe.shape[1])


def _gather(h, slot, aff, cap):
    bsz, n, d = h.shape
    ne = slot.shape[1]
    ge = GATHER_EXPERTS
    return pl.pallas_call(
        functools.partial(_gather_kernel, cap=cap),
        grid=(bsz, ne // ge),
        in_specs=[pl.BlockSpec((1, n, d), lambda b, j: (b, 0, 0)),
                  pl.BlockSpec((1, ge, n), lambda b, j: (b, j, 0)),
                  pl.BlockSpec((1, ge, n), lambda b, j: (b, j, 0))],
        out_specs=[pl.BlockSpec((ge, cap, d), lambda b, j: (j, b, 0)),
                   pl.BlockSpec((ge, cap, 1), lambda b, j: (j, b, 0))],
        out_shape=[jax.ShapeDtypeStruct((ne, bsz * cap, d), BF16),
                   jax.ShapeDtypeStruct((ne, bsz * cap, 1), F32)],
        scratch_shapes=[pltpu.VMEM((ge * cap, n), BF16)],
        compiler_params=_params("arbitrary", "arbitrary"),
        name="moe_gather",
    )(h, slot, aff)


def _ffn_kernel(*refs, nseg, rows_per_pass):
    x_refs = refs[:nseg]
    gate_refs = refs[nseg:2 * nseg]
    wg_ref, wu_ref, wd_ref = refs[2 * nseg:2 * nseg + 3]
    y_refs = refs[2 * nseg + 3:3 * nseg + 3]
    acc_refs = refs[3 * nseg + 3:]
    f = pl.program_id(1)

    @pl.when(f == 0)
    def _():
        for acc_ref in acc_refs:
            acc_ref[...] = jnp.zeros_like(acc_ref)

    wg = wg_ref[0].astype(BF16)
    wu = wu_ref[0].astype(BF16)
    wd = wd_ref[0].astype(BF16)
    passes = []
    for x_ref, acc_ref in zip(x_refs, acc_refs):
        m = x_ref.shape[1]
        step = min(rows_per_pass, m)
        passes += [(x_ref, acc_ref, slice(r, r + step)) for r in range(0, m, step)]

    def hidden(p):
        x = passes[p][0][0, passes[p][2], :]
        return (_silu(_dot(x, wg)) * _dot(x, wu)).astype(BF16)

    hid = hidden(0)
    for p, (_, acc_ref, rows) in enumerate(passes):
        hid_next = hidden(p + 1) if p + 1 < len(passes) else None
        acc_ref[rows, :] += _dot(hid, wd)
        hid = hid_next

    @pl.when(f == pl.num_programs(1) - 1)
    def _():
        for y_ref, gate_ref, acc_ref in zip(y_refs, gate_refs, acc_refs):
            y_ref[0] = (acc_ref[...] * gate_ref[0]).astype(BF16)


def _ffn(xes, gates, w_gate, w_up, w_down, layer):
    _, ne, d, ff = w_gate.shape
    tf = 256
    nseg = len(xes)
    return pl.pallas_call(
        functools.partial(_ffn_kernel, nseg=nseg, rows_per_pass=512),
        grid=(ne, ff // tf),
        in_specs=[pl.BlockSpec((1, xe.shape[1], d), lambda e, f: (e, 0, 0)) for xe in xes]
        + [pl.BlockSpec((1, xe.shape[1], 1), lambda e, f: (e, 0, 0)) for xe in xes]
        + [pl.BlockSpec((None, 1, d, tf), lambda e, f: (layer, e, 0, f)),
           pl.BlockSpec((None, 1, d, tf), lambda e, f: (layer, e, 0, f)),
           pl.BlockSpec((None, 1, tf, d), lambda e, f: (layer, e, f, 0))],
        out_specs=[pl.BlockSpec((1, xe.shape[1], d), lambda e, f: (e, 0, 0)) for xe in xes],
        out_shape=[jax.ShapeDtypeStruct(xe.shape, BF16) for xe in xes],
        scratch_shapes=[pltpu.VMEM(xe.shape[1:], F32) for xe in xes],
        compiler_params=_params("arbitrary", "arbitrary"),
        name="moe_ffn",
    )(*xes, *gates, w_gate, w_up, w_down)


def _combine_kernel(*refs, cap, final_norm):
    slot_ref, ye_ref, x_ref, mod_ref = refs[:4]
    if final_norm:
        fg_ref, o_ref, onehot_ref = refs[4:]
    else:
        o_ref, onehot_ref = refs[4:]
    ne = ye_ref.shape[0]
    if cap % LANES == 0:
        pi = lax.broadcasted_iota(jnp.int32, (1, cap), 1)
        for e in range(ne):
            hit = slot_ref[0, :, e:e + 1] == pi
            onehot_ref[:, e * cap:(e + 1) * cap] = jnp.where(hit, 1.0, 0.0).astype(BF16)
    else:
        ei = lax.broadcasted_iota(jnp.int32, (ne, ne * cap), 0)
        li = lax.broadcasted_iota(jnp.int32, (ne, ne * cap), 1)
        spread = jnp.where((li >= ei * cap) & (li < (ei + 1) * cap), 1.0, 0.0).astype(BF16)
        slots = _dot(slot_ref[0].astype(F32).astype(BF16), spread)
        group = jnp.sum(jnp.where(li >= (ei + 1) * cap, 1, 0), axis=0, keepdims=True)
        pos = (li[0:1] - group * cap).astype(F32)
        onehot_ref[...] = jnp.where(slots == pos, 1.0, 0.0).astype(BF16)
    ye = ye_ref[...]
    moe = _dot(onehot_ref[...], ye.reshape(ne * cap, ye.shape[2]))
    y = x_ref[0] + mod_ref[0, 0][5:6] * moe
    if final_norm:
        y = _rms(y) * fg_ref[...]
    o_ref[0] = y


def _combine(slot_t, ye, x, mod, group, cap, final_g):
    bsz, n, d = x.shape
    ne = slot_t.shape[2]
    tm = min(n, 1024)
    final_norm = final_g is not None
    in_specs = [pl.BlockSpec((1, tm, ne), lambda b, i: (b, i, 0)),
                pl.BlockSpec((ne, cap, d), lambda b, i: (0, b, 0)),
                pl.BlockSpec((1, tm, d), lambda b, i: (b, i, 0)),
                _mod_spec(d, group)]
    args = [slot_t, ye, x, mod]
    if final_norm:
        in_specs.append(pl.BlockSpec((1, d), lambda b, i: (0, 0)))
        args.append(final_g.reshape(1, d))
    return pl.pallas_call(
        functools.partial(_combine_kernel, cap=cap, final_norm=final_norm),
        grid=(bsz, n // tm),
        in_specs=in_specs,
        out_specs=pl.BlockSpec((1, tm, d), lambda b, i: (b, i, 0)),
        out_shape=jax.ShapeDtypeStruct((bsz, n, d), F32),
        scratch_shapes=[pltpu.VMEM((tm, ne * cap), BF16)],
        compiler_params=_params("arbitrary", "arbitrary"),
        name="moe_combine",
    )(*args)


def _moe(streams, groups, mod, w_gate, w_up, w_down, layer, final_g=None):
    xs = [s[0] for s in streams]
    routed = []
    for x, h, logits in streams:
        cap = EC_CAPACITY_FACTOR * x.shape[1] // logits.shape[1]
        slot, aff = _select(logits, cap)
        xe, gate = _gather(h, slot, aff, cap)
        routed.append((cap, slot, xe, gate))
    yes = _ffn([r[2] for r in routed], [r[3] for r in routed], w_gate, w_up, w_down, layer)
    return [_combine(jnp.swapaxes(slot, 1, 2), ye, x, mod, group, cap, final_g)
            for x, group, ye, (cap, slot, _, _) in zip(xs, groups, yes, routed)]


def _chunk_masks(tm, direction):
    ri = lax.broadcasted_iota(jnp.int32, (tm, tm), 0)
    ci = lax.broadcasted_iota(jnp.int32, (tm, tm), 1)
    shift = CHUNK.bit_length() - 1
    same = jnp.right_shift(ri, shift) == jnp.right_shift(ci, shift)
    return same & ((ci <= ri) if direction == 0 else (ci >= ri))


def _chunk_logdecay(la, direction):
    cum = jnp.where(_chunk_masks(la.shape[0], direction), 1.0, 0.0).astype(BF16)
    hi, lo = _split2(la * LOG2E)
    return _dot(cum, hi) + _dot(cum, lo)


def _scan_operands(q, k, b, direction, refs):
    q_in_ref, q_mid_ref, k_mid_ref, k_out_ref, decay_ref = refs
    tm, n = q.shape
    nc = tm // CHUNK
    b = b.reshape(nc, CHUNK, n)
    last, mid = (CHUNK - 1, CHUNK // 2) if direction == 0 else (0, CHUNK // 2 - 1)
    b_tot = b[:, last:last + 1]
    b_mid = b[:, mid:mid + 1]
    q = q.reshape(nc, CHUNK, n)
    k = k.reshape(nc, CHUNK, n)
    q_in_ref[direction, 0] = (q * jnp.exp2(b)).astype(BF16).reshape(tm, n)
    q_mid_ref[direction, 0] = (q * jnp.exp2(b - b_mid)).astype(BF16).reshape(tm, n)
    k_mid_ref[direction, 0] = (k * jnp.exp2(b_mid - b)).astype(BF16).reshape(tm, n)
    k_out_ref[direction, 0] = (k * jnp.exp2(b_tot - b)).astype(BF16).reshape(tm, n)
    decay = jnp.exp2(b_tot).reshape(nc, n)
    decay_ref[direction, 0, 0] = jnp.concatenate([decay] * (8 // nc), axis=0)


def _rec_proj_kernel(xc_ref, xl_ref, mod_ref, g_ref, w_ref, gkup_ref, gkb_ref, lbl_ref, *out_refs, layer):
    gla_refs, gv_ref, gg_ref = out_refs[0:5], out_refs[5], out_refs[6]
    hg_refs, hv_ref, hg_ref = out_refs[7:12], out_refs[12], out_refs[13]
    mod = mod_ref[0, 0]
    h = _modulate(_stream_block(xc_ref, xl_ref), g_ref[...], mod[0:1], mod[1:2]).astype(BF16)
    n = GLA_HEADS * LANES
    c0 = 2 * n + LANES
    c1 = c0 + 3 * n
    head = _dot(h, w_ref[:, 0:c0])
    gate = _dot(head[:, 2 * n:c0].astype(BF16), gkup_ref[...]) + gkb_ref[...]
    qk = _dot(h, w_ref[:, c0:c1])
    lbl = lbl_ref[...]
    e = jnp.exp(lbl - jnp.max(lbl, axis=0, keepdims=True))
    sm = e / jnp.sum(e, axis=0, keepdims=True)
    lb = jnp.sum(sm[0:layer + 1], axis=0, keepdims=True) - sm[0:1]
    f = lb + (1.0 - lb) * jax.nn.sigmoid(head[:, 0:2 * n])
    fs = [f[:, dd * n:(dd + 1) * n] for dd in range(2)]
    rest = _dot(h, w_ref[:, c1:])
    b_hg = [_chunk_logdecay(jnp.log(fs[dd]), dd) for dd in range(2)]
    b_gla = [_chunk_logdecay(jax.nn.log_sigmoid(gate[:, dd * n:(dd + 1) * n]) / GLA_GATE_NORM, dd)
             for dd in range(2)]
    hq = _silu(qk[:, 2 * n:3 * n])
    gq = qk[:, 0:n] * GLA_DK ** -0.5
    gk = qk[:, n:2 * n]
    for dd in range(2):
        _scan_operands(hq, 1.0 - fs[dd], b_hg[dd], dd, hg_refs)
    for dd in range(2):
        _scan_operands(gq, gk, b_gla[dd], dd, gla_refs)
    gv_ref[0] = rest[:, 0:n].astype(BF16)
    gg_ref[0] = rest[:, n:2 * n].astype(BF16)
    hv_ref[0] = rest[:, 2 * n:3 * n].astype(BF16)
    hg_ref[0] = rest[:, 3 * n:4 * n].astype(BF16)


def _pad_heads(w, heads, dk):
    lead = w.shape[:-1]
    w3 = w.reshape(lead + (heads, dk))
    return jnp.pad(w3, [(0, 0)] * len(lead) + [(0, 0), (0, LANES - dk)]).reshape(lead + (heads * LANES,))


def _rec_project(xc, xl, mod, g, w_in, gk_up, gk_bias, lb_logits, layer):
    bsz, n_ctx, d = xc.shape
    t = n_ctx + xl.shape[1]
    tm = ROW_TILE
    hk = GLA_HEADS * GLA_DK
    hv = GLA_HEADS * GLA_DV
    hf = HG_HEADS * HG_DF
    sizes = (hk, hk, hv, hv, 2 * GLA_GATE_RANK, hf, 2 * hf, HG_HEADS * HG_DV, HG_HEADS * HG_DV)
    offs = [0]
    for s in sizes:
        offs.append(offs[-1] + s)
    gq, gk, gv, gg, gdown, hq, hff, hi, hgt = [w_in[:, offs[i]:offs[i + 1]] for i in range(9)]
    w = jnp.concatenate([hff, jnp.pad(gdown, ((0, 0), (0, LANES - 2 * GLA_GATE_RANK))),
                         _pad_heads(gq, GLA_HEADS, GLA_DK), _pad_heads(gk, GLA_HEADS, GLA_DK), hq,
                         gv, gg, hi, hgt], axis=1).astype(BF16)
    n = GLA_HEADS * LANES
    ups = [_pad_heads(gk_up[dd], GLA_HEADS, GLA_DK) for dd in range(2)]
    zero = jnp.zeros_like(ups[0])
    gkup = jnp.concatenate([jnp.concatenate([ups[0], zero], axis=1), jnp.concatenate([zero, ups[1]], axis=1),
                            jnp.zeros((LANES - 2 * GLA_GATE_RANK, 2 * n), F32)], axis=0).astype(BF16)
    gkb = jnp.concatenate([_pad_heads(gk_bias[dd], GLA_HEADS, GLA_DK) for dd in range(2)]).reshape(1, 2 * n)
    row = lambda b, i: (b, i, 0)
    drow = lambda b, i: (0, b, i, 0)
    const = lambda b, i: (0, 0)
    one = lambda dt: (pl.BlockSpec((1, tm, n), row), jax.ShapeDtypeStruct((bsz, t, n), dt))
    two = (pl.BlockSpec((2, 1, tm, n), drow), jax.ShapeDtypeStruct((2, bsz, t, n), BF16))
    dec = (pl.BlockSpec((2, 1, 1, 8, n), lambda b, i: (0, b, i, 0, 0)),
           jax.ShapeDtypeStruct((2, bsz, t // tm, 8, n), F32))
    mixer = [two, two, two, two, dec, one(BF16), one(BF16)]
    outs = mixer + mixer
    return pl.pallas_call(
        functools.partial(_rec_proj_kernel, layer=layer),
        grid=(bsz, t // tm),
        in_specs=_stream_specs(d) + [
            _mod_spec(d),
            pl.BlockSpec((1, d), const),
            pl.BlockSpec(w.shape, const),
            pl.BlockSpec(gkup.shape, const),
            pl.BlockSpec(gkb.shape, const),
            pl.BlockSpec((lb_logits.shape[0], 2 * hf), const)],
        out_specs=[o[0] for o in outs],
        out_shape=[o[1] for o in outs],
        compiler_params=_params("arbitrary", "arbitrary"),
        name="rec_project",
    )(xc, xl, mod, g.reshape(1, d), w, gkup, gkb, lb_logits.reshape(lb_logits.shape[0], 2 * hf))


def _scan_kernel(*refs, heads):
    ins = [refs[0:6], refs[6:12]]
    o_refs = refs[12:14]
    st_ref = refs[14]
    tm = o_refs[0].shape[1]
    nc = tm // CHUNK

    @pl.when(pl.program_id(1) == 0)
    def _():
        st_ref[...] = jnp.zeros_like(st_ref)

    lanes = [slice(LANES * h, LANES * (h + 1)) for h in range(heads)]
    chunk_rows = [slice(c * CHUNK, (c + 1) * CHUNK) for c in range(nc)]
    units = [(d, h) for d in range(2) for h in range(heads)]
    vs = {(d, h): ins[d][5][0, :, lanes[h]] for d, h in units}
    scores = {(d, h): _dot_nt(ins[d][1][0, 0, :, lanes[h]], ins[d][2][0, 0, :, lanes[h]]) for d, h in units}
    updates = {(d, h): [_dot_tn(vs[d, h][rows], ins[d][3][0, 0, rows, lanes[h]]) for rows in chunk_rows]
               for d, h in units}
    causal = [_chunk_masks(tm, d) for d in range(2)]
    intra = {(d, h): _dot(jnp.where(causal[d], scores[d, h], 0.0).astype(BF16), vs[d, h]) for d, h in units}
    order = [list(range(nc)), list(range(nc - 1, -1, -1))]
    entering = {}
    for d, h in units:
        st = st_ref[d, h]
        for c in order[d]:
            entering[d, h, c] = st.astype(BF16)
            st = st * ins[d][4][0, 0, 0, c:c + 1, lanes[h]] + updates[d, h][c]
        st_ref[d, h] = st
    for d, h in units:
        for c in order[d]:
            rows = chunk_rows[c]
            inter = _dot_nt(ins[d][0][0, 0, rows, lanes[h]], entering[d, h, c])
            o_refs[d][0, rows, lanes[h]] = (intra[d, h][rows] + inter).astype(BF16)


def _scan(q_in, q_mid, k_mid, k_out, decay, v, n_ctx):
    _, bsz, t, n = q_in.shape
    heads = n // LANES
    tm = ROW_TILE
    nb = t // tm
    ncb = n_ctx // tm

    def blk(d, s):
        return s if d == 0 else jnp.where(s < ncb, ncb - 1 - s, nb - 1 - (s - ncb))

    in_specs, args = [], []
    for d in range(2):
        drow = lambda b, s, d=d: (d, b, blk(d, s), 0)
        in_specs += [pl.BlockSpec((1, 1, tm, n), drow)] * 4
        in_specs += [pl.BlockSpec((1, 1, 1, 8, n), lambda b, s, d=d: (d, b, blk(d, s), 0, 0)),
                     pl.BlockSpec((1, tm, n), lambda b, s, d=d: (b, blk(d, s), 0))]
        args += [q_in, q_mid, k_mid, k_out, decay, v]
    return pl.pallas_call(
        functools.partial(_scan_kernel, heads=heads),
        grid=(bsz, nb),
        in_specs=in_specs,
        out_specs=[pl.BlockSpec((1, tm, n), lambda b, s, d=d: (b, blk(d, s), 0)) for d in range(2)],
        out_shape=[jax.ShapeDtypeStruct((bsz, t, n), BF16)] * 2,
        scratch_shapes=[pltpu.VMEM((2, heads, LANES, LANES), F32)],
        compiler_params=_params("arbitrary", "arbitrary"),
        name="rec_scan",
    )(*args)


def _rec_out_kernel(og0_ref, og1_ref, oh0_ref, oh1_ref, gg_ref, hg_ref, gn_ref, hn_ref, wout_ref,
                    x_ref, mod_ref, g2_ref, wrt_ref, o_ref, h_ref, lg_ref):
    parts = []
    for fwd_ref, bwd_ref, gate_ref, n_ref in ((og0_ref, og1_ref, gg_ref, gn_ref),
                                              (oh0_ref, oh1_ref, hg_ref, hn_ref)):
        for h in range(fwd_ref.shape[2] // LANES):
            sl = slice(LANES * h, LANES * (h + 1))
            o = fwd_ref[0, :, sl].astype(F32) + bwd_ref[0, :, sl].astype(F32)
            parts.append(_rms(o) * n_ref[...] * _silu(gate_ref[0, :, sl].astype(F32)))
    a = jnp.concatenate(parts, axis=1).astype(BF16)
    mod = mod_ref[0, 0]
    y = x_ref[0] + mod[2:3] * _dot(a, wout_ref[...])
    o_ref[0] = y
    _route_rows(y, mod, g2_ref, wrt_ref, h_ref, lg_ref)


def _rec_readout(og, oh, gg, hg, gn, hn, w_out, xl, mod, g2, w_router, n_ctx):
    bsz, seq, d = xl.shape
    tm = ROW_TILE
    ncb = n_ctx // tm
    n = gg.shape[2]
    row = lambda b, i: (b, i + ncb, 0)
    const = lambda b, i: (0, 0)
    wout = w_out.astype(BF16)
    r_in, r_out, r_shape = _route_specs(bsz, seq, d, w_router.shape[1], tm)
    return pl.pallas_call(
        _rec_out_kernel,
        grid=(bsz, seq // tm),
        in_specs=[pl.BlockSpec((1, tm, n), row),
                  pl.BlockSpec((1, tm, n), row),
                  pl.BlockSpec((1, tm, n), row),
                  pl.BlockSpec((1, tm, n), row),
                  pl.BlockSpec((1, tm, n), row),
                  pl.BlockSpec((1, tm, n), row),
                  pl.BlockSpec((1, LANES), const),
                  pl.BlockSpec((1, LANES), const),
                  pl.BlockSpec(wout.shape, const),
                  pl.BlockSpec((1, tm, d), lambda b, i: (b, i, 0)),
                  _mod_spec(d, 1)] + r_in,
        out_specs=[pl.BlockSpec((1, tm, d), lambda b, i: (b, i, 0))] + r_out,
        out_shape=[jax.ShapeDtypeStruct((bsz, seq, d), F32)] + r_shape,
        compiler_params=_params("arbitrary", "arbitrary"),
        name="rec_readout",
    )(*og, *oh, gg, hg, gn.reshape(1, -1), hn.reshape(1, -1), wout, xl, mod, g2.reshape(1, d), w_router.T)


def kernel(x, c, ctx, c_ctx, ada_w, ada_b, norm1_g, norm2_g, att_w_in, mla_q_norm_g, mla_q_up, mla_kv_norm_g, mla_kv_up, da_lam_q1, da_lam_k1, da_lam_q2, da_lam_k2, da_subln_g, att_w_out, rec_w_in, gla_gk_up, gla_gk_bias, gla_norm_g, hg_lb_logits, hg_norm_g, rec_w_out, moe_router, moe_w_gate, moe_w_up, moe_w_down, final_norm_g):
    bsz, seq, d = x.shape
    n_ctx = ctx.shape[1]
    depth = ada_w.shape[0]
    assert depth == 2 and n_ctx == ROW_TILE and seq % ROW_TILE == 0 and bsz + 1 <= 16

    cvec = jnp.concatenate([c, c_ctx[None, :], jnp.zeros((16 - bsz - 1, d), F32)], axis=0)
    ada = _ada_vectors(cvec, ada_w, ada_b).reshape(depth, 16, 6, d)
    ada = jnp.pad(ada, ((0, 0), (0, 0), (0, 2), (0, 0)))
    mods = jnp.stack([jnp.broadcast_to(ada[:, bsz][:, None], (depth, bsz, 8, d)), ada[:, :bsz]], axis=2)
    xc, xl = ctx, x

    mod = mods[0]
    tabs_da = _rope_tables(n_ctx, seq, DA_QK, 0, DA_QK)
    tabs_m = _rope_tables(n_ctx, seq, MLA_ROPE, MLA_NOPE, LANES)
    proj = _attn_project(xc, xl, mod, norm1_g[0], att_w_in[0], mla_q_norm_g[0], mla_q_up[0],
                         mla_kv_norm_g[0], mla_kv_up[0], tabs_da, tabs_m)
    lamv = jnp.stack([da_lam_q1[0], da_lam_k1[0], da_lam_q2[0], da_lam_k2[0]])
    lam_init = 0.8 - 0.6 * math.exp(-0.3 * 0)
    sc = _attention(*proj, lamv, da_subln_g[0], att_w_out[0], xc, mod, norm2_g[0], moe_router[0], 0, 0, n_ctx,
                    lam_init)
    sl = _attention(*proj, lamv, da_subln_g[0], att_w_out[0], xl, mod, norm2_g[0], moe_router[0], 1,
                    n_ctx // ROW_TILE, n_ctx + seq, lam_init)
    xc, xl = _moe([sc, sl], [0, 1], mod, moe_w_gate, moe_w_up, moe_w_down, 0)

    mod = mods[1]
    rec = _rec_project(xc, xl, mod, norm1_g[1], rec_w_in[0], gla_gk_up[0], gla_gk_bias[0], hg_lb_logits, 1)
    og = _scan(*rec[0:6], n_ctx)
    oh = _scan(*rec[7:13], n_ctx)
    sl = _rec_readout(og, oh, rec[6], rec[13], gla_norm_g[0], hg_norm_g[0], rec_w_out[0], xl, mod, norm2_g[1],
                      moe_router[1], n_ctx)
    (out,) = _moe([sl], [1], mod, moe_w_gate, moe_w_up, moe_w_down, 1, final_g=final_norm_g)
    return out
```

```python
import functools
import math

import jax
import jax.numpy as jnp
from jax import lax
from jax.experimental import pallas as pl
from jax.experimental.pallas import tpu as pltpu

F32 = jnp.float32
BF16 = jnp.bfloat16

EPS = 1e-6
GRID_W = 64
ROPE_BASE = 10000.0

DA_HEADS = 4
DA_QK = 64
DA_V = 2 * DA_QK
DA_SCALE = DA_QK ** -0.5
MLA_HEADS = 8
MLA_NOPE = 64
MLA_ROPE = 32
MLA_V = 64
MLA_Q_RANK = 384
MLA_KV_RANK = 256
MLA_SCALE = (MLA_NOPE + MLA_ROPE) ** -0.5
LOG2E = math.log2(math.e)

GLA_HEADS = 4
GLA_DK = 64
GLA_DV = 128
GLA_GATE_RANK = 16
GLA_GATE_NORM = 16.0
HG_HEADS = 4
HG_DF = 128
HG_DV = 128
CHUNK = 64

N_EXPERTS = 16
EC_CAPACITY_FACTOR = 2

LANES = 128
ROW_TILE = 256
ATTN_KEY_BLOCK = 256
SUM_ROWS = 16
ATTN_ROUTER_UNIT = 12
ATTN_LOOKAHEAD = 12
VMEM_LIMIT = 56 * 1024 * 1024


def _params(*sem):
    return pltpu.CompilerParams(dimension_semantics=sem, vmem_limit_bytes=VMEM_LIMIT)


def _dot(a, b):
    return jnp.dot(a, b, preferred_element_type=F32)


def _dot_nt(a, b):
    return lax.dot_general(a, b, (((1,), (1,)), ((), ())), preferred_element_type=F32)


def _dot_tn(a, b):
    return lax.dot_general(a, b, (((0,), (0,)), ((), ())), preferred_element_type=F32)


def _split2(a):
    hi = a.astype(BF16)
    lo = (a - hi.astype(F32)).astype(BF16)
    return hi, lo


def _dot3(a, b, dot=_dot):
    ah, al = _split2(a)
    bh, bl = _split2(b)
    return dot(ah, bh) + (dot(ah, bl) + dot(al, bh))


def _rms(x):
    return x * lax.rsqrt(jnp.mean(x * x, axis=-1, keepdims=True) + EPS)


def _silu(x):
    return x * jax.nn.sigmoid(x)


def _modulate(x, g, shift, scale):
    return _rms(x) * g * (1.0 + scale) + shift


def _stream_specs(d):
    return [pl.BlockSpec((1, ROW_TILE, d), lambda b, i: (b, 0, 0)),
            pl.BlockSpec((1, ROW_TILE, d), lambda b, i: (b, jnp.maximum(i - 1, 0), 0))]


def _stream_block(xc_ref, xl_ref):
    return jnp.where(pl.program_id(1) == 0, xc_ref[0], xl_ref[0])


def _route_rows(y, mod, g2_ref, wrt_ref, h_ref, lg_ref):
    _route_logits(_route_split(y, mod, g2_ref, h_ref), wrt_ref, lg_ref)


def _route_split(y, mod, g2_ref, h_ref):
    hh, hl = _split2(_modulate(y, g2_ref[...], mod[3:4], mod[4:5]))
    h_ref[0] = hh
    return hh, hl


def _route_logits(split, wrt_ref, lg_ref):
    hh, hl = split
    wh, wl = _split2(wrt_ref[...])
    ne = wh.shape[0]
    both = _dot_nt(jnp.concatenate([wh, wl], axis=0), hh)
    lg_ref[0] = both[0:ne] + (_dot_nt(wh, hl) + both[ne:2 * ne])


def _route_specs(bsz, n, d, ne, tm):
    const = lambda b, i: (0, 0)
    return ([pl.BlockSpec((1, d), const), pl.BlockSpec((ne, d), const)],
            [pl.BlockSpec((1, tm, d), lambda b, i: (b, i, 0)), pl.BlockSpec((1, ne, tm), lambda b, i: (b, 0, i))],
            [jax.ShapeDtypeStruct((bsz, n, d), BF16), jax.ShapeDtypeStruct((bsz, ne, n), F32)])


def _mod_spec(d, group=None):
    if group is None:
        return pl.BlockSpec((1, 1, 8, d), lambda b, i: (b, jnp.minimum(i, 1), 0, 0))
    return pl.BlockSpec((1, 1, 8, d), lambda b, i: (b, group, 0, 0))


def _ada_kernel(c_ref, w_ref, b_ref, o_ref):
    o_ref[...] = _dot3(_silu(c_ref[...]), w_ref[...]) + b_ref[...]


def _ada_vectors(cvec, ada_w, ada_b):
    depth, d, n6 = ada_w.shape
    tn = n6 // 4
    rows = cvec.shape[0]
    return pl.pallas_call(
        _ada_kernel,
        grid=(depth, n6 // tn),
        in_specs=[pl.BlockSpec((rows, d), lambda l, j: (0, 0)),
                  pl.BlockSpec((None, d, tn), lambda l, j: (l, 0, j)),
                  pl.BlockSpec((None, 1, tn), lambda l, j: (l, 0, j))],
        out_specs=pl.BlockSpec((None, rows, tn), lambda l, j: (l, 0, j)),
        out_shape=jax.ShapeDtypeStruct((depth, rows, n6), F32),
        compiler_params=_params("arbitrary", "arbitrary"),
        name="ada_vectors",
    )(cvec, ada_w, ada_b.reshape(depth, 1, n6))


def _axial_angles(rows, rot_dim):
    row = jnp.repeat(jnp.arange(rows, dtype=F32), GRID_W)
    col = jnp.tile(jnp.arange(GRID_W, dtype=F32), rows)
    n_freq = rot_dim // 4
    inv = ROPE_BASE ** (-jnp.arange(n_freq, dtype=F32) / n_freq)
    ang = jnp.concatenate([row[:, None] * inv, col[:, None] * inv], axis=-1)
    return jnp.cos(ang), jnp.sin(ang)


def _rope_tables(n_ctx, seq, width, start, group):
    half = width // 2
    cos, sin = _axial_angles(seq // GRID_W, width)
    lane = jnp.arange(LANES) % group - start
    inside = (lane >= 0) & (lane < width)
    idx = jnp.clip(lane, 0, width - 1) % half
    second = inside & (lane >= half)
    firsth = inside & (lane < half)
    c = jnp.where(inside[None, :], cos[:, idx], 1.0)
    s_prev = jnp.where(second[None, :], sin[:, idx], 0.0)
    s_next = jnp.where(firsth[None, :], -sin[:, idx], 0.0)
    ident = jnp.ones((n_ctx, LANES), F32)
    zero = jnp.zeros((n_ctx, LANES), F32)
    return (jnp.concatenate([ident, c], 0), jnp.concatenate([zero, s_prev], 0),
            jnp.concatenate([zero, s_next], 0))


def _rope(x, cos, s_prev, s_next, half):
    return x * cos + pltpu.roll(x, half, 1) * s_prev + pltpu.roll(x, LANES - half, 1) * s_next


def _attn_proj_kernel(xc_ref, xl_ref, mod_ref, g_ref, w_ref, wvt_ref, vab_ref, qn_ref, qup_ref, kvn_ref,
                      kvup_ref, kvupt_ref, vmb_ref, cda_ref, pda_ref, nda_ref, cm_ref, pm_ref, nm_ref,
                      qda_ref, kda_ref, vat_ref, qm_ref, km_ref, vmt_ref):
    mod = mod_ref[0, 0]
    h = _modulate(_stream_block(xc_ref, xl_ref), g_ref[...], mod[0:1], mod[1:2]).astype(BF16)
    proj = _dot(h, w_ref[...])
    vat_ref[0] = (_dot_nt(wvt_ref[...], h) + vab_ref[...]).astype(BF16)
    cda, pda, nda = cda_ref[...], pda_ref[...], nda_ref[...]
    cm, pm, nm = cm_ref[...], pm_ref[...], nm_ref[...]
    n_da = DA_HEADS * LANES
    for hh in range(DA_HEADS):
        sl = slice(LANES * hh, LANES * (hh + 1))
        q = proj[:, LANES * hh:LANES * (hh + 1)]
        k = proj[:, n_da + LANES * hh:n_da + LANES * (hh + 1)]
        qda_ref[0, :, sl] = (_rope(q, cda, pda, nda, DA_QK // 2) * (DA_SCALE * LOG2E)).astype(BF16)
        kda_ref[0, :, sl] = _rope(k, cda, pda, nda, DA_QK // 2).astype(BF16)
    o = 2 * n_da
    cq = proj[:, o:o + MLA_Q_RANK]
    ckv = proj[:, o + MLA_Q_RANK:o + MLA_Q_RANK + MLA_KV_RANK]
    kr = proj[:, o + MLA_Q_RANK + MLA_KV_RANK:o + MLA_Q_RANK + MLA_KV_RANK + LANES]
    qm = _dot((_rms(cq) * qn_ref[...]).astype(BF16), qup_ref[...])
    ckvn = (_rms(ckv) * kvn_ref[...]).astype(BF16)
    kv = _dot(ckvn, kvup_ref[...])
    vmt_ref[0] = (_dot_nt(kvupt_ref[...], ckvn) + vmb_ref[...]).astype(BF16)
    rep = kr + pltpu.roll(kr, 32, 1) + pltpu.roll(kr, 64, 1) + pltpu.roll(kr, 96, 1)
    lane = lax.broadcasted_iota(jnp.int32, (1, LANES), 1)
    rep = jnp.where((lane >= MLA_NOPE) & (lane < MLA_NOPE + MLA_ROPE), rep, 0.0)
    rep = _rope(rep, cm, pm, nm, MLA_ROPE // 2)
    for hh in range(MLA_HEADS):
        sl = slice(LANES * hh, LANES * (hh + 1))
        qm_ref[0, :, sl] = (_rope(qm[:, sl], cm, pm, nm, MLA_ROPE // 2) * (MLA_SCALE * LOG2E)).astype(BF16)
        km_ref[0, :, sl] = (kv[:, sl] + rep).astype(BF16)


def _with_sum_rows(wt, heads):
    dv = wt.shape[0] // heads
    w3 = jnp.pad(wt.reshape(heads, dv, -1), ((0, 0), (0, SUM_ROWS), (0, 0)))
    bias = jnp.pad(jnp.zeros((heads, dv, 1), F32), ((0, 0), (0, SUM_ROWS), (0, 0)), constant_values=1.0)
    return w3.reshape(heads * (dv + SUM_ROWS), -1).astype(BF16), bias.reshape(heads * (dv + SUM_ROWS), 1)


def _attn_project(xc, xl, mod, g, w_in, qn_g, q_up, kvn_g, kv_up, tabs_da, tabs_m):
    bsz, n_ctx, d = xc.shape
    t = n_ctx + xl.shape[1]
    tm = ROW_TILE
    n_da = DA_HEADS * LANES
    nm = MLA_HEADS * LANES
    nv = MLA_HEADS * MLA_V
    w = jnp.concatenate([w_in[:, :2 * n_da], w_in[:, 3 * n_da:]], axis=1)
    w = jnp.pad(w, ((0, 0), (0, (-w.shape[1]) % LANES))).astype(BF16)
    wvt, vab = _with_sum_rows(w_in[:, 2 * n_da:3 * n_da].T, DA_HEADS)
    qup = jnp.pad(q_up.reshape(MLA_Q_RANK, MLA_HEADS, MLA_NOPE + MLA_ROPE),
                  ((0, 0), (0, 0), (0, LANES - MLA_NOPE - MLA_ROPE))).reshape(MLA_Q_RANK, -1).astype(BF16)
    kv3 = kv_up.reshape(MLA_KV_RANK, MLA_HEADS, MLA_NOPE + MLA_V)
    kvup = jnp.pad(kv3[:, :, :MLA_NOPE],
                   ((0, 0), (0, 0), (0, LANES - MLA_NOPE))).reshape(MLA_KV_RANK, -1).astype(BF16)
    kvupt, vmb = _with_sum_rows(kv3[:, :, MLA_NOPE:].reshape(MLA_KV_RANK, -1).T, MLA_HEADS)
    row = lambda b, i: (b, i, 0)
    col = lambda b, i: (b, 0, i)
    const = lambda b, i: (0, 0)
    tab = pl.BlockSpec((tm, LANES), lambda b, i: (i, 0))
    rows_out = lambda n: (pl.BlockSpec((1, tm, n), row), jax.ShapeDtypeStruct((bsz, t, n), BF16))
    cols_out = lambda n: (pl.BlockSpec((1, n, tm), col), jax.ShapeDtypeStruct((bsz, n, t), BF16))
    outs = [rows_out(n_da), rows_out(n_da), cols_out(wvt.shape[0]), rows_out(nm), rows_out(nm),
            cols_out(kvupt.shape[0])]
    return pl.pallas_call(
        _attn_proj_kernel,
        grid=(bsz, t // tm),
        in_specs=_stream_specs(d) + [
            _mod_spec(d),
            pl.BlockSpec((1, d), const),
            pl.BlockSpec(w.shape, const),
            pl.BlockSpec(wvt.shape, const),
            pl.BlockSpec(vab.shape, const),
            pl.BlockSpec((1, MLA_Q_RANK), const),
            pl.BlockSpec(qup.shape, const),
            pl.BlockSpec((1, MLA_KV_RANK), const),
            pl.BlockSpec(kvup.shape, const),
            pl.BlockSpec(kvupt.shape, const),
            pl.BlockSpec(vmb.shape, const),
            tab, tab, tab, tab, tab, tab],
        out_specs=[o[0] for o in outs],
        out_shape=[o[1] for o in outs],
        compiler_params=_params("arbitrary", "arbitrary"),
        name="attn_project",
    )(xc, xl, mod, g.reshape(1, d), w, wvt, vab, qn_g.reshape(1, -1), qup, kvn_g.reshape(1, -1), kvup, kvupt,
      vmb, *tabs_da, *tabs_m)


def _reduce_rows(x, fn):
    n, w = x.shape
    slab = 256
    if n > slab and n % slab == 0:
        x = fn(x.reshape(n // slab, slab, w), axis=0)
        n = slab
    if n > 32 and n % 32 == 0:
        x = fn(x.reshape(n // 32, 32, w), axis=0)
    return fn(x, axis=0, keepdims=True)


def _softmax_parts(s):
    e = jnp.exp2(s - _reduce_rows(s, jnp.max))
    return e.astype(BF16), 1.0 / _reduce_rows(e, jnp.sum)


def _attn_kernel(qda_ref, qm_ref, kda_ref, vat_ref, km_ref, vmt_ref, lam_ref, subg_ref, wout_ref,
                 x_ref, mod_ref, g2_ref, wrt_ref, o_ref, h_ref, lg_ref, att_ref, s_ref, *, lam_init):
    @pl.when(pl.program_id(0) == 0)
    def _():
        att_ref[...] = jnp.zeros_like(att_ref)

    mod = mod_ref[0, 0]
    y = x_ref[0] + mod[2:3] * _dot_tn(att_ref[...], wout_ref[...])
    o_ref[0] = y
    route = _route_split(y, mod, g2_ref, h_ref)

    lamv = lam_ref[...]
    lam = (jnp.exp(jnp.sum(lamv[0:1] * lamv[1:2], axis=1, keepdims=True))
           - jnp.exp(jnp.sum(lamv[2:3] * lamv[3:4], axis=1, keepdims=True)) + lam_init)
    first = lax.broadcasted_iota(jnp.int32, (1, LANES), 1) < LANES // 2

    n_maps = 2 * DA_HEADS + MLA_HEADS
    maps = []
    for i in range(2 * DA_HEADS):
        sl = slice(LANES * (i // 2), LANES * (i // 2 + 1))
        q = qda_ref[0, :, sl].astype(F32)
        q = jnp.where(first, q, 0.0) if i % 2 == 0 else jnp.where(first, 0.0, q)
        rows = DA_V + SUM_ROWS
        maps.append((q.astype(BF16), kda_ref, sl, vat_ref, slice(rows * (i // 2), rows * (i // 2 + 1)), DA_V))
    for i in range(MLA_HEADS):
        sl = slice(LANES * i, LANES * (i + 1))
        rows = MLA_V + SUM_ROWS
        maps.append((qm_ref[0, :, sl], km_ref, sl, vmt_ref, slice(rows * i, rows * (i + 1)), MLA_V))

    kb = min(ATTN_KEY_BLOCK, kda_ref.shape[1])
    n_blocks = kda_ref.shape[1] // kb

    units = [(i, j) for i in range(n_maps) for j in range(n_blocks)]

    def scores(u):
        i, j = units[u]
        q, k_ref, ksl = maps[i][:3]
        return _dot_nt(k_ref[0, j * kb:(j + 1) * kb, ksl], q)

    for u in range(min(ATTN_LOOKAHEAD, len(units))):
        s_ref[u % ATTN_LOOKAHEAD, 0:kb, :] = scores(u)
    heads = []
    for u, (i, j) in enumerate(units):
        s = s_ref[u % ATTN_LOOKAHEAD, 0:kb, :]
        if u + ATTN_LOOKAHEAD < len(units):
            s_ref[u % ATTN_LOOKAHEAD, 0:kb, :] = scores(u + ATTN_LOOKAHEAD)
        if u == min(ATTN_ROUTER_UNIT, len(units) - 1):
            _route_logits(route, wrt_ref, lg_ref)
        v_ref, vsl, dv = maps[i][3:]
        vt = v_ref[0, vsl, j * kb:(j + 1) * kb]
        mb = _reduce_rows(s, jnp.max)
        if j == 0:
            m = mb
            acc = _dot(vt, jnp.exp2((s - m).astype(BF16)))
        else:
            mn = jnp.maximum(m, mb)
            acc = acc * jnp.exp2(m - mn) + _dot(vt, jnp.exp2((s - mn).astype(BF16)))
            m = mn
        if j == n_blocks - 1:
            heads.append(acc[0:dv] * (1.0 / acc[dv:dv + 1]))
    outs = []
    for h in range(DA_HEADS):
        oa = heads[2 * h] - heads[2 * h + 1] * lam
        oa = oa * lax.rsqrt(jnp.mean(oa * oa, axis=0, keepdims=True) + EPS)
        outs.append(oa * subg_ref[...] * (1.0 - lam_init))
    outs.extend(heads[2 * DA_HEADS:])
    att_ref[...] = jnp.concatenate(outs, axis=0).astype(BF16)


def _attention(qda, kda, vat, qm, km, vmt, lamv, subln_g, w_out, x, mod, g2, w_router, group, q_block0,
               n_keys, lam_init):
    bsz, nq, d = x.shape
    tq = ROW_TILE
    per = nq // tq
    n_tiles = bsz * per
    ne = w_router.shape[1]
    cur = lambda s: jnp.minimum(s, n_tiles - 1)
    prev = lambda s: jnp.maximum(s - 1, 0)
    qrow = lambda s: (cur(s) // per, cur(s) % per + q_block0, 0)
    full = lambda s: (cur(s) // per, 0, 0)
    row = lambda s: (prev(s) // per, prev(s) % per, 0)
    const = lambda s: (0, 0)
    wout = w_out.astype(BF16)
    return pl.pallas_call(
        functools.partial(_attn_kernel, lam_init=lam_init),
        grid=(n_tiles + 1,),
        in_specs=[pl.BlockSpec((1, tq, qda.shape[2]), qrow),
                  pl.BlockSpec((1, tq, qm.shape[2]), qrow),
                  pl.BlockSpec((1, n_keys, kda.shape[2]), full),
                  pl.BlockSpec((1, vat.shape[1], n_keys), full),
                  pl.BlockSpec((1, n_keys, km.shape[2]), full),
                  pl.BlockSpec((1, vmt.shape[1], n_keys), full),
                  pl.BlockSpec(lamv.shape, const),
                  pl.BlockSpec((DA_V, 1), const),
                  pl.BlockSpec(wout.shape, const),
                  pl.BlockSpec((1, tq, d), row),
                  pl.BlockSpec((1, 1, 8, d), lambda s: (prev(s) // per, group, 0, 0)),
                  pl.BlockSpec((1, d), const),
                  pl.BlockSpec((ne, d), const)],
        out_specs=[pl.BlockSpec((1, tq, d), row),
                   pl.BlockSpec((1, tq, d), row),
                   pl.BlockSpec((1, ne, tq), lambda s: (prev(s) // per, 0, prev(s) % per))],
        out_shape=[jax.ShapeDtypeStruct((bsz, nq, d), F32),
                   jax.ShapeDtypeStruct((bsz, nq, d), BF16),
                   jax.ShapeDtypeStruct((bsz, ne, nq), F32)],
        scratch_shapes=[pltpu.VMEM((w_out.shape[0], tq), BF16),
                        pltpu.VMEM((ATTN_LOOKAHEAD, min(ATTN_KEY_BLOCK, n_keys), tq), F32)],
        compiler_params=_params("arbitrary"),
        name="attention",
    )(qda, qm, kda, vat, km, vmt, lamv, subln_g.reshape(-1, 1), wout, x, mod, g2.reshape(1, d), w_router.T)


def _kth_largest(a, k):
    rows = a.shape[0]
    lo0 = jnp.full((rows, 1), -1.0, F32)
    hi0 = jnp.max(a, axis=1, keepdims=True)

    def cond(c):
        return c[2] > 0

    def body(c):
        lo, hi, _ = c
        mid = 0.5 * (lo + hi)
        open_ = (mid > lo) & (mid < hi)
        cnt = jnp.sum(jnp.where(a > mid, 1.0, 0.0), axis=1, keepdims=True)
        up = open_ & (cnt >= k)
        down = open_ & (cnt < k)
        lo = jnp.where(up, mid, lo)
        hi = jnp.where(down, mid, hi)
        return lo, hi, jnp.sum(jnp.where(open_, 1.0, 0.0))

    _, hi, _ = lax.while_loop(cond, body, (lo0, hi0, jnp.float32(1.0)))
    return hi


def _select_kernel(lg_ref, slot_ref, aff_ref, tri_ref, *, ne, cap):
    rows, n = lg_ref.shape
    rt = 256
    for r in range(0, n, rt):
        ri = lax.broadcasted_iota(jnp.int32, (rt, n), 0) + r
        ci = lax.broadcasted_iota(jnp.int32, (rt, n), 1)
        tri_ref[r:r + rt, :] = jnp.where(ri < ci, 1.0, 0.0).astype(BF16)

    lg = lg_ref[...].reshape(rows // ne, ne, n)
    e = jnp.exp(lg - jnp.max(lg, axis=1, keepdims=True))
    aff = (e / jnp.sum(e, axis=1, keepdims=True)).reshape(rows, n)
    aff_ref[...] = aff
    thr = _kth_largest(aff, cap)
    gt = aff > thr
    eq = aff == thr
    need = cap - jnp.sum(jnp.where(gt, 1.0, 0.0), axis=1, keepdims=True)
    tri = tri_ref[...]
    before = _dot(jnp.where(eq, 1.0, 0.0).astype(BF16), tri)
    sel = gt | (eq & (before < need))
    pos = _dot(jnp.where(sel, 1.0, 0.0).astype(BF16), tri)
    slot_ref[...] = jnp.where(sel, pos, -1.0).astype(jnp.int32)


def _select(logits, cap):
    bsz, ne, n = logits.shape
    blk = pl.BlockSpec((bsz * ne, n), lambda i: (0, 0))
    slot, aff = pl.pallas_call(
        functools.partial(_select_kernel, ne=ne, cap=cap),
        grid=(1,),
        in_specs=[blk],
        out_specs=[blk, blk],
        out_shape=[jax.ShapeDtypeStruct((bsz * ne, n), jnp.int32),
                   jax.ShapeDtypeStruct((bsz * ne, n), F32)],
        scratch_shapes=[pltpu.VMEM((n, n), BF16)],
        compiler_params=_params("arbitrary"),
        name="moe_select",
    )(logits.reshape(bsz * ne, n))
    return slot.reshape(bsz, ne, n), aff.reshape(bsz, ne, n)


GATHER_EXPERTS = 8


def _gather_kernel(h_ref, slot_ref, aff_ref, xe_ref, gate_ref, onehot_ref, *, cap):
    n = h_ref.shape[1]
    pi = lax.broadcasted_iota(jnp.int32, (cap, n), 0)
    for e in range(GATHER_EXPERTS):
        hit = slot_ref[0, e:e + 1, :] == pi
        onehot_ref[e * cap:(e + 1) * cap, :] = jnp.where(hit, 1.0, 0.0).astype(BF16)
        gate_ref[e] = jnp.sum(jnp.where(hit, aff_ref[0, e:e + 1, :], 0.0), axis=1, keepdims=True)
    xe = _dot(onehot_ref[...], h_ref[0]).astype(BF16)
    xe_ref[...] = xe.reshape(GATHER_EXPERTS, cap, xe.shape[1])


def _gather(h, slot, aff, cap):
    bsz, n, d = h.shape
    ne = slot.shape[1]
    ge = GATHER_EXPERTS
    return pl.pallas_call(
        functools.partial(_gather_kernel, cap=cap),
        grid=(bsz, ne // ge),
        in_specs=[pl.BlockSpec((1, n, d), lambda b, j: (b, 0, 0)),
                  pl.BlockSpec((1, ge, n), lambda b, j: (b, j, 0)),
                  pl.BlockSpec((1, ge, n), lambda b, j: (b, j, 0))],
        out_specs=[pl.BlockSpec((ge, cap, d), lambda b, j: (j, b, 0)),
                   pl.BlockSpec((ge, cap, 1), lambda b, j: (j, b, 0))],
        out_shape=[jax.ShapeDtypeStruct((ne, bsz * cap, d), BF16),
                   jax.ShapeDtypeStruct((ne, bsz * cap, 1), F32)],
        scratch_shapes=[pltpu.VMEM((ge * cap, n), BF16)],
        compiler_params=_params("arbitrary", "arbitrary"),
        name="moe_gather",
    )(h, slot, aff)


def _ffn_kernel(*refs, nseg, rows_per_pass):
    x_refs = refs[:nseg]
    gate_refs = refs[nseg:2 * nseg]
    wg_ref, wu_ref, wd_ref = refs[2 * nseg:2 * nseg + 3]
    y_refs = refs[2 * nseg + 3:3 * nseg + 3]
    acc_refs = refs[3 * nseg + 3:]
    f = pl.program_id(1)
    passes = []
    for x_ref, acc_ref in zip(x_refs, acc_refs):
        m = x_ref.shape[1]
        step = min(rows_per_pass, m)
        passes += [(x_ref, acc_ref, slice(r, r + step)) for r in range(0, m, step)]

    def run(first):
        wg = wg_ref[0].astype(BF16)
        wu = wu_ref[0].astype(BF16)
        wd = wd_ref[0].astype(BF16)
        for x_ref, acc_ref, rows in passes:
            x = x_ref[0, rows, :]
            part = _dot((_silu(_dot(x, wg)) * _dot(x, wu)).astype(BF16), wd)
            if first:
                acc_ref[rows, :] = part
            else:
                acc_ref[rows, :] += part

    @pl.when(f == 0)
    def _():
        run(True)

    @pl.when(f > 0)
    def _():
        run(False)

    @pl.when(f == pl.num_programs(1) - 1)
    def _():
        for y_ref, gate_ref, acc_ref in zip(y_refs, gate_refs, acc_refs):
            y_ref[0] = (acc_ref[...] * gate_ref[0]).astype(BF16)


def _ffn(xes, gates, w_gate, w_up, w_down, layer):
    _, ne, d, ff = w_gate.shape
    tf = 256
    nseg = len(xes)
    return pl.pallas_call(
        functools.partial(_ffn_kernel, nseg=nseg, rows_per_pass=512),
        grid=(ne, ff // tf),
        in_specs=[pl.BlockSpec((1, xe.shape[1], d), lambda e, f: (e, 0, 0)) for xe in xes]
        + [pl.BlockSpec((1, xe.shape[1], 1), lambda e, f: (e, 0, 0)) for xe in xes]
        + [pl.BlockSpec((None, 1, d, tf), lambda e, f: (layer, e, 0, f)),
           pl.BlockSpec((None, 1, d, tf), lambda e, f: (layer, e, 0, f)),
           pl.BlockSpec((None, 1, tf, d), lambda e, f: (layer, e, f, 0))],
        out_specs=[pl.BlockSpec((1, xe.shape[1], d), lambda e, f: (e, 0, 0)) for xe in xes],
        out_shape=[jax.ShapeDtypeStruct(xe.shape, BF16) for xe in xes],
        scratch_shapes=[pltpu.VMEM(xe.shape[1:], F32) for xe in xes],
        compiler_params=_params("arbitrary", "arbitrary"),
        name="moe_ffn",
    )(*xes, *gates, w_gate, w_up, w_down)


def _combine_kernel(*refs, cap, final_norm):
    slot_ref, ye_ref, x_ref, mod_ref = refs[:4]
    if final_norm:
        fg_ref, o_ref, onehot_ref = refs[4:]
    else:
        o_ref, onehot_ref = refs[4:]
    ne = ye_ref.shape[0]
    if cap % LANES == 0:
        pi = lax.broadcasted_iota(jnp.int32, (1, cap), 1)
        for e in range(ne):
            hit = slot_ref[0, :, e:e + 1] == pi
            onehot_ref[:, e * cap:(e + 1) * cap] = jnp.where(hit, 1.0, 0.0).astype(BF16)
    else:
        ei = lax.broadcasted_iota(jnp.int32, (ne, ne * cap), 0)
        li = lax.broadcasted_iota(jnp.int32, (ne, ne * cap), 1)
        spread = jnp.where((li >= ei * cap) & (li < (ei + 1) * cap), 1.0, 0.0).astype(BF16)
        slots = _dot(slot_ref[0].astype(F32).astype(BF16), spread)
        group = jnp.sum(jnp.where(li >= (ei + 1) * cap, 1, 0), axis=0, keepdims=True)
        pos = (li[0:1] - group * cap).astype(F32)
        onehot_ref[...] = jnp.where(slots == pos, 1.0, 0.0).astype(BF16)
    ye = ye_ref[...]
    moe = _dot(onehot_ref[...], ye.reshape(ne * cap, ye.shape[2]))
    y = x_ref[0] + mod_ref[0, 0][5:6] * moe
    if final_norm:
        y = _rms(y) * fg_ref[...]
    o_ref[0] = y


def _combine(slot_t, ye, x, mod, group, cap, final_g):
    bsz, n, d = x.shape
    ne = slot_t.shape[2]
    tm = min(n, 1024)
    final_norm = final_g is not None
    in_specs = [pl.BlockSpec((1, tm, ne), lambda b, i: (b, i, 0)),
                pl.BlockSpec((ne, cap, d), lambda b, i: (0, b, 0)),
                pl.BlockSpec((1, tm, d), lambda b, i: (b, i, 0)),
                _mod_spec(d, group)]
    args = [slot_t, ye, x, mod]
    if final_norm:
        in_specs.append(pl.BlockSpec((1, d), lambda b, i: (0, 0)))
        args.append(final_g.reshape(1, d))
    return pl.pallas_call(
        functools.partial(_combine_kernel, cap=cap, final_norm=final_norm),
        grid=(bsz, n // tm),
        in_specs=in_specs,
        out_specs=pl.BlockSpec((1, tm, d), lambda b, i: (b, i, 0)),
        out_shape=jax.ShapeDtypeStruct((bsz, n, d), F32),
        scratch_shapes=[pltpu.VMEM((tm, ne * cap), BF16)],
        compiler_params=_params("arbitrary", "arbitrary"),
        name="moe_combine",
    )(*args)


def _moe(streams, groups, mod, w_gate, w_up, w_down, layer, final_g=None):
    xs = [s[0] for s in streams]
    routed = []
    for x, h, logits in streams:
        cap = EC_CAPACITY_FACTOR * x.shape[1] // logits.shape[1]
        slot, aff = _select(logits, cap)
        xe, gate = _gather(h, slot, aff, cap)
        routed.append((cap, slot, xe, gate))
    yes = _ffn([r[2] for r in routed], [r[3] for r in routed], w_gate, w_up, w_down, layer)
    return [_combine(jnp.swapaxes(slot, 1, 2), ye, x, mod, group, cap, final_g)
            for x, group, ye, (cap, slot, _, _) in zip(xs, groups, yes, routed)]


def _chunk_masks(tm, direction):
    ri = lax.broadcasted_iota(jnp.int32, (tm, tm), 0)
    ci = lax.broadcasted_iota(jnp.int32, (tm, tm), 1)
    shift = CHUNK.bit_length() - 1
    same = jnp.right_shift(ri, shift) == jnp.right_shift(ci, shift)
    return same & ((ci <= ri) if direction == 0 else (ci >= ri))


def _chunk_logdecay(la, direction):
    cum = jnp.where(_chunk_masks(la.shape[0], direction), 1.0, 0.0).astype(BF16)
    hi, lo = _split2(la * LOG2E)
    return _dot(cum, hi) + _dot(cum, lo)


def _scan_operands(q, k, b, direction, refs):
    q_in_ref, q_mid_ref, k_mid_ref, k_out_ref, decay_ref = refs
    tm, n = q.shape
    nc = tm // CHUNK
    b = b.reshape(nc, CHUNK, n)
    last, mid = (CHUNK - 1, CHUNK // 2) if direction == 0 else (0, CHUNK // 2 - 1)
    b_tot = b[:, last:last + 1]
    b_mid = b[:, mid:mid + 1]
    q = q.reshape(nc, CHUNK, n)
    k = k.reshape(nc, CHUNK, n)
    q_in_ref[direction, 0] = (q * jnp.exp2(b)).astype(BF16).reshape(tm, n)
    q_mid_ref[direction, 0] = (q * jnp.exp2(b - b_mid)).astype(BF16).reshape(tm, n)
    k_mid_ref[direction, 0] = (k * jnp.exp2(b_mid - b)).astype(BF16).reshape(tm, n)
    k_out_ref[direction, 0] = (k * jnp.exp2(b_tot - b)).astype(BF16).reshape(tm, n)
    decay = jnp.exp2(b_tot).reshape(nc, n)
    decay_ref[direction, 0, 0] = jnp.concatenate([decay] * (8 // nc), axis=0)


def _rec_proj_kernel(xc_ref, xl_ref, mod_ref, g_ref, w_ref, gkup_ref, gkb_ref, lbl_ref, *out_refs, layer):
    gla_refs, gv_ref, gg_ref = out_refs[0:5], out_refs[5], out_refs[6]
    hg_refs, hv_ref, hg_ref = out_refs[7:12], out_refs[12], out_refs[13]
    mod = mod_ref[0, 0]
    h = _modulate(_stream_block(xc_ref, xl_ref), g_ref[...], mod[0:1], mod[1:2]).astype(BF16)
    n = GLA_HEADS * LANES
    c0 = 2 * n + LANES
    c1 = c0 + 3 * n
    head = _dot(h, w_ref[:, 0:c0])
    gate = _dot(head[:, 2 * n:c0].astype(BF16), gkup_ref[...]) + gkb_ref[...]
    qk = _dot(h, w_ref[:, c0:c1])
    lbl = lbl_ref[...]
    e = jnp.exp(lbl - jnp.max(lbl, axis=0, keepdims=True))
    sm = e / jnp.sum(e, axis=0, keepdims=True)
    lb = jnp.sum(sm[0:layer + 1], axis=0, keepdims=True) - sm[0:1]
    f = lb + (1.0 - lb) * jax.nn.sigmoid(head[:, 0:2 * n])
    fs = [f[:, dd * n:(dd + 1) * n] for dd in range(2)]
    rest = _dot(h, w_ref[:, c1:])
    b_hg = [_chunk_logdecay(jnp.log(fs[dd]), dd) for dd in range(2)]
    b_gla = [_chunk_logdecay(jax.nn.log_sigmoid(gate[:, dd * n:(dd + 1) * n]) / GLA_GATE_NORM, dd)
             for dd in range(2)]
    hq = _silu(qk[:, 2 * n:3 * n])
    gq = qk[:, 0:n] * GLA_DK ** -0.5
    gk = qk[:, n:2 * n]
    for dd in range(2):
        _scan_operands(hq, 1.0 - fs[dd], b_hg[dd], dd, hg_refs)
    for dd in range(2):
        _scan_operands(gq, gk, b_gla[dd], dd, gla_refs)
    gv_ref[0] = rest[:, 0:n].astype(BF16)
    gg_ref[0] = rest[:, n:2 * n].astype(BF16)
    hv_ref[0] = rest[:, 2 * n:3 * n].astype(BF16)
    hg_ref[0] = rest[:, 3 * n:4 * n].astype(BF16)


def _pad_heads(w, heads, dk):
    lead = w.shape[:-1]
    w3 = w.reshape(lead + (heads, dk))
    return jnp.pad(w3, [(0, 0)] * len(lead) + [(0, 0), (0, LANES - dk)]).reshape(lead + (heads * LANES,))


def _rec_project(xc, xl, mod, g, w_in, gk_up, gk_bias, lb_logits, layer):
    bsz, n_ctx, d = xc.shape
    t = n_ctx + xl.shape[1]
    tm = ROW_TILE
    hk = GLA_HEADS * GLA_DK
    hv = GLA_HEADS * GLA_DV
    hf = HG_HEADS * HG_DF
    sizes = (hk, hk, hv, hv, 2 * GLA_GATE_RANK, hf, 2 * hf, HG_HEADS * HG_DV, HG_HEADS * HG_DV)
    offs = [0]
    for s in sizes:
        offs.append(offs[-1] + s)
    gq, gk, gv, gg, gdown, hq, hff, hi, hgt = [w_in[:, offs[i]:offs[i + 1]] for i in range(9)]
    w = jnp.concatenate([hff, jnp.pad(gdown, ((0, 0), (0, LANES - 2 * GLA_GATE_RANK))),
                         _pad_heads(gq, GLA_HEADS, GLA_DK), _pad_heads(gk, GLA_HEADS, GLA_DK), hq,
                         gv, gg, hi, hgt], axis=1).astype(BF16)
    n = GLA_HEADS * LANES
    ups = [_pad_heads(gk_up[dd], GLA_HEADS, GLA_DK) for dd in range(2)]
    zero = jnp.zeros_like(ups[0])
    gkup = jnp.concatenate([jnp.concatenate([ups[0], zero], axis=1), jnp.concatenate([zero, ups[1]], axis=1),
                            jnp.zeros((LANES - 2 * GLA_GATE_RANK, 2 * n), F32)], axis=0).astype(BF16)
    gkb = jnp.concatenate([_pad_heads(gk_bias[dd], GLA_HEADS, GLA_DK) for dd in range(2)]).reshape(1, 2 * n)
    row = lambda b, i: (b, i, 0)
    drow = lambda b, i: (0, b, i, 0)
    const = lambda b, i: (0, 0)
    one = lambda dt: (pl.BlockSpec((1, tm, n), row), jax.ShapeDtypeStruct((bsz, t, n), dt))
    two = (pl.BlockSpec((2, 1, tm, n), drow), jax.ShapeDtypeStruct((2, bsz, t, n), BF16))
    dec = (pl.BlockSpec((2, 1, 1, 8, n), lambda b, i: (0, b, i, 0, 0)),
           jax.ShapeDtypeStruct((2, bsz, t // tm, 8, n), F32))
    mixer = [two, two, two, two, dec, one(BF16), one(BF16)]
    outs = mixer + mixer
    return pl.pallas_call(
        functools.partial(_rec_proj_kernel, layer=layer),
        grid=(bsz, t // tm),
        in_specs=_stream_specs(d) + [
            _mod_spec(d),
            pl.BlockSpec((1, d), const),
            pl.BlockSpec(w.shape, const),
            pl.BlockSpec(gkup.shape, const),
            pl.BlockSpec(gkb.shape, const),
            pl.BlockSpec((lb_logits.shape[0], 2 * hf), const)],
        out_specs=[o[0] for o in outs],
        out_shape=[o[1] for o in outs],
        compiler_params=_params("arbitrary", "arbitrary"),
        name="rec_project",
    )(xc, xl, mod, g.reshape(1, d), w, gkup, gkb, lb_logits.reshape(lb_logits.shape[0], 2 * hf))


def _scan_kernel(*refs, heads, directions):
    ns = len(directions)
    ins = [refs[6 * i:6 * i + 6] for i in range(ns)]
    o_refs = refs[6 * ns:7 * ns]
    st_ref = refs[7 * ns]
    tm = o_refs[0].shape[1]
    nc = tm // CHUNK

    @pl.when(pl.program_id(1) == 0)
    def _():
        st_ref[...] = jnp.zeros_like(st_ref)

    lanes = [slice(LANES * h, LANES * (h + 1)) for h in range(heads)]
    chunk_rows = [slice(c * CHUNK, (c + 1) * CHUNK) for c in range(nc)]
    units = [(i, h) for i in range(ns) for h in range(heads)]
    vs = {(i, h): ins[i][5][0, :, lanes[h]] for i, h in units}
    scores = {(i, h): _dot_nt(ins[i][1][0, 0, :, lanes[h]], ins[i][2][0, 0, :, lanes[h]]) for i, h in units}
    updates = {(i, h): [_dot_tn(vs[i, h][rows], ins[i][3][0, 0, rows, lanes[h]]) for rows in chunk_rows]
               for i, h in units}
    causal = [_chunk_masks(tm, d) for d in range(2)]
    intra = {(i, h): _dot(jnp.where(causal[directions[i]], scores[i, h], 0.0).astype(BF16), vs[i, h])
             for i, h in units}
    order = [list(range(nc)), list(range(nc - 1, -1, -1))]
    entering = {}
    for i, h in units:
        st = st_ref[i, h]
        for c in order[directions[i]]:
            entering[i, h, c] = st.astype(BF16)
            st = st * ins[i][4][0, 0, 0, c:c + 1, lanes[h]] + updates[i, h][c]
        st_ref[i, h] = st
    for i, h in units:
        for c in order[directions[i]]:
            rows = chunk_rows[c]
            inter = _dot_nt(ins[i][0][0, 0, rows, lanes[h]], entering[i, h, c])
            o_refs[i][0, rows, lanes[h]] = (intra[i, h][rows] + inter).astype(BF16)


def _scan(mixers, n_ctx):
    _, bsz, t, n = mixers[0][0].shape
    heads = n // LANES
    tm = ROW_TILE
    nb = t // tm
    ncb = n_ctx // tm

    def blk(d, s):
        return s if d == 0 else jnp.where(s < ncb, ncb - 1 - s, nb - 1 - (s - ncb))

    in_specs, args, directions = [], [], []
    for q_in, q_mid, k_mid, k_out, decay, v in mixers:
        for d in range(2):
            drow = lambda b, s, d=d: (d, b, blk(d, s), 0)
            in_specs += [pl.BlockSpec((1, 1, tm, n), drow)] * 4
            in_specs += [pl.BlockSpec((1, 1, 1, 8, n), lambda b, s, d=d: (d, b, blk(d, s), 0, 0)),
                         pl.BlockSpec((1, tm, n), lambda b, s, d=d: (b, blk(d, s), 0))]
            args += [q_in, q_mid, k_mid, k_out, decay, v]
            directions.append(d)
    outs = pl.pallas_call(
        functools.partial(_scan_kernel, heads=heads, directions=tuple(directions)),
        grid=(bsz, nb),
        in_specs=in_specs,
        out_specs=[pl.BlockSpec((1, tm, n), lambda b, s, d=d: (b, blk(d, s), 0)) for d in directions],
        out_shape=[jax.ShapeDtypeStruct((bsz, t, n), BF16)] * len(directions),
        scratch_shapes=[pltpu.VMEM((len(directions), heads, LANES, LANES), F32)],
        compiler_params=_params("arbitrary", "arbitrary"),
        name="rec_scan",
    )(*args)
    return [tuple(outs[2 * i:2 * i + 2]) for i in range(len(mixers))]


def _rec_out_kernel(og0_ref, og1_ref, oh0_ref, oh1_ref, gg_ref, hg_ref, gn_ref, hn_ref, wout_ref,
                    x_ref, mod_ref, g2_ref, wrt_ref, o_ref, h_ref, lg_ref):
    parts = []
    for fwd_ref, bwd_ref, gate_ref, n_ref in ((og0_ref, og1_ref, gg_ref, gn_ref),
                                              (oh0_ref, oh1_ref, hg_ref, hn_ref)):
        for h in range(fwd_ref.shape[2] // LANES):
            sl = slice(LANES * h, LANES * (h + 1))
            o = fwd_ref[0, :, sl].astype(F32) + bwd_ref[0, :, sl].astype(F32)
            parts.append(_rms(o) * n_ref[...] * _silu(gate_ref[0, :, sl].astype(F32)))
    a = jnp.concatenate(parts, axis=1).astype(BF16)
    mod = mod_ref[0, 0]
    y = x_ref[0] + mod[2:3] * _dot(a, wout_ref[...])
    o_ref[0] = y
    _route_rows(y, mod, g2_ref, wrt_ref, h_ref, lg_ref)


def _rec_readout(og, oh, gg, hg, gn, hn, w_out, xl, mod, g2, w_router, n_ctx):
    bsz, seq, d = xl.shape
    tm = ROW_TILE
    ncb = n_ctx // tm
    n = gg.shape[2]
    row = lambda b, i: (b, i + ncb, 0)
    const = lambda b, i: (0, 0)
    wout = w_out.astype(BF16)
    r_in, r_out, r_shape = _route_specs(bsz, seq, d, w_router.shape[1], tm)
    return pl.pallas_call(
        _rec_out_kernel,
        grid=(bsz, seq // tm),
        in_specs=[pl.BlockSpec((1, tm, n), row),
                  pl.BlockSpec((1, tm, n), row),
                  pl.BlockSpec((1, tm, n), row),
                  pl.BlockSpec((1, tm, n), row),
                  pl.BlockSpec((1, tm, n), row),
                  pl.BlockSpec((1, tm, n), row),
                  pl.BlockSpec((1, LANES), const),
                  pl.BlockSpec((1, LANES), const),
                  pl.BlockSpec(wout.shape, const),
                  pl.BlockSpec((1, tm, d), lambda b, i: (b, i, 0)),
                  _mod_spec(d, 1)] + r_in,
        out_specs=[pl.BlockSpec((1, tm, d), lambda b, i: (b, i, 0))] + r_out,
        out_shape=[jax.ShapeDtypeStruct((bsz, seq, d), F32)] + r_shape,
        compiler_params=_params("arbitrary", "arbitrary"),
        name="rec_readout",
    )(*og, *oh, gg, hg, gn.reshape(1, -1), hn.reshape(1, -1), wout, xl, mod, g2.reshape(1, d), w_router.T)


def kernel(x, c, ctx, c_ctx, ada_w, ada_b, norm1_g, norm2_g, att_w_in, mla_q_norm_g, mla_q_up, mla_kv_norm_g, mla_kv_up, da_lam_q1, da_lam_k1, da_lam_q2, da_lam_k2, da_subln_g, att_w_out, rec_w_in, gla_gk_up, gla_gk_bias, gla_norm_g, hg_lb_logits, hg_norm_g, rec_w_out, moe_router, moe_w_gate, moe_w_up, moe_w_down, final_norm_g):
    bsz, seq, d = x.shape
    n_ctx = ctx.shape[1]
    depth = ada_w.shape[0]
    assert depth == 2 and n_ctx == ROW_TILE and seq % ROW_TILE == 0 and bsz + 1 <= 16

    cvec = jnp.concatenate([c, c_ctx[None, :], jnp.zeros((16 - bsz - 1, d), F32)], axis=0)
    ada = _ada_vectors(cvec, ada_w, ada_b).reshape(depth, 16, 6, d)
    ada = jnp.pad(ada, ((0, 0), (0, 0), (0, 2), (0, 0)))
    mods = jnp.stack([jnp.broadcast_to(ada[:, bsz][:, None], (depth, bsz, 8, d)), ada[:, :bsz]], axis=2)
    xc, xl = ctx, x

    mod = mods[0]
    tabs_da = _rope_tables(n_ctx, seq, DA_QK, 0, DA_QK)
    tabs_m = _rope_tables(n_ctx, seq, MLA_ROPE, MLA_NOPE, LANES)
    proj = _attn_project(xc, xl, mod, norm1_g[0], att_w_in[0], mla_q_norm_g[0], mla_q_up[0],
                         mla_kv_norm_g[0], mla_kv_up[0], tabs_da, tabs_m)
    lamv = jnp.stack([da_lam_q1[0], da_lam_k1[0], da_lam_q2[0], da_lam_k2[0]])
    lam_init = 0.8 - 0.6 * math.exp(-0.3 * 0)
    sc = _attention(*proj, lamv, da_subln_g[0], att_w_out[0], xc, mod, norm2_g[0], moe_router[0], 0, 0, n_ctx,
                    lam_init)
    sl = _attention(*proj, lamv, da_subln_g[0], att_w_out[0], xl, mod, norm2_g[0], moe_router[0], 1,
                    n_ctx // ROW_TILE, n_ctx + seq, lam_init)
    xc, xl = _moe([sc, sl], [0, 1], mod, moe_w_gate, moe_w_up, moe_w_down, 0)

    mod = mods[1]
    rec = _rec_project(xc, xl, mod, norm1_g[1], rec_w_in[0], gla_gk_up[0], gla_gk_bias[0], hg_lb_logits, 1)
    og, oh = _scan([rec[0:6], rec[7:13]], n_ctx)
    sl = _rec_readout(og, oh, rec[6], rec[13], gla_norm_g[0], hg_norm_g[0], rec_w_out[0], xl, mod, norm2_g[1],
                      moe_router[1], n_ctx)
    (out,) = _moe([sl], [1], mod, moe_w_gate, moe_w_up, moe_w_down, 1, final_g=final_norm_g)
    return out
```

```python
import functools
import math

import jax
import jax.numpy as jnp
from jax import lax
from jax.experimental import pallas as pl
from jax.experimental.pallas import tpu as pltpu

F32 = jnp.float32
BF16 = jnp.bfloat16

EPS = 1e-6
GRID_W = 64
ROPE_BASE = 10000.0

DA_HEADS = 4
DA_QK = 64
DA_V = 2 * DA_QK
DA_SCALE = DA_QK ** -0.5
MLA_HEADS = 8
MLA_NOPE = 64
MLA_ROPE = 32
MLA_V = 64
MLA_Q_RANK = 384
MLA_KV_RANK = 256
MLA_SCALE = (MLA_NOPE + MLA_ROPE) ** -0.5
LOG2E = math.log2(math.e)

GLA_HEADS = 4
GLA_DK = 64
GLA_DV = 128
GLA_GATE_RANK = 16
GLA_GATE_NORM = 16.0
HG_HEADS = 4
HG_DF = 128
HG_DV = 128
CHUNK = 64

N_EXPERTS = 16
EC_CAPACITY_FACTOR = 2

LANES = 128
ROW_TILE = 256
ATTN_KEY_BLOCK = 256
SUM_ROWS = 16
ATTN_ROUTER_UNIT = 12
ATTN_LOOKAHEAD = 12
VMEM_LIMIT = 56 * 1024 * 1024


def _params(*sem):
    return pltpu.CompilerParams(dimension_semantics=sem, vmem_limit_bytes=VMEM_LIMIT)


def _dot(a, b):
    return jnp.dot(a, b, preferred_element_type=F32)


def _dot_nt(a, b):
    return lax.dot_general(a, b, (((1,), (1,)), ((), ())), preferred_element_type=F32)


def _dot_tn(a, b):
    return lax.dot_general(a, b, (((0,), (0,)), ((), ())), preferred_element_type=F32)


def _split2(a):
    hi = a.astype(BF16)
    lo = (a - hi.astype(F32)).astype(BF16)
    return hi, lo


def _dot3(a, b, dot=_dot):
    ah, al = _split2(a)
    bh, bl = _split2(b)
    return dot(ah, bh) + (dot(ah, bl) + dot(al, bh))


def _rms(x):
    return x * lax.rsqrt(jnp.mean(x * x, axis=-1, keepdims=True) + EPS)


def _silu(x):
    return x * jax.nn.sigmoid(x)


def _modulate(x, g, shift, scale):
    return _rms(x) * g * (1.0 + scale) + shift


def _stream_specs(d):
    return [pl.BlockSpec((1, ROW_TILE, d), lambda b, i: (b, 0, 0)),
            pl.BlockSpec((1, ROW_TILE, d), lambda b, i: (b, jnp.maximum(i - 1, 0), 0))]


def _stream_block(xc_ref, xl_ref):
    return jnp.where(pl.program_id(1) == 0, xc_ref[0], xl_ref[0])


def _route_rows(y, mod, g2_ref, wrt_ref, h_ref, lg_ref):
    _route_logits(_route_split(y, mod, g2_ref, h_ref), wrt_ref, lg_ref)


def _route_split(y, mod, g2_ref, h_ref):
    hh, hl = _split2(_modulate(y, g2_ref[...], mod[3:4], mod[4:5]))
    h_ref[0] = hh
    return hh, hl


def _route_logits(split, wrt_ref, lg_ref):
    hh, hl = split
    wh, wl = _split2(wrt_ref[...])
    ne = wh.shape[0]
    both = _dot_nt(jnp.concatenate([wh, wl], axis=0), hh)
    lg_ref[0] = both[0:ne] + (_dot_nt(wh, hl) + both[ne:2 * ne])


def _route_specs(bsz, n, d, ne, tm):
    const = lambda b, i: (0, 0)
    return ([pl.BlockSpec((1, d), const), pl.BlockSpec((ne, d), const)],
            [pl.BlockSpec((1, tm, d), lambda b, i: (b, i, 0)), pl.BlockSpec((1, ne, tm), lambda b, i: (b, 0, i))],
            [jax.ShapeDtypeStruct((bsz, n, d), BF16), jax.ShapeDtypeStruct((bsz, ne, n), F32)])


def _mod_spec(d, group=None):
    if group is None:
        return pl.BlockSpec((1, 1, 8, d), lambda b, i: (b, jnp.minimum(i, 1), 0, 0))
    return pl.BlockSpec((1, 1, 8, d), lambda b, i: (b, group, 0, 0))


def _ada_kernel(c_ref, w_ref, b_ref, o_ref):
    o_ref[...] = _dot3(_silu(c_ref[...]), w_ref[...]) + b_ref[...]


def _ada_vectors(cvec, ada_w, ada_b):
    depth, d, n6 = ada_w.shape
    tn = n6 // 4
    rows = cvec.shape[0]
    return pl.pallas_call(
        _ada_kernel,
        grid=(depth, n6 // tn),
        in_specs=[pl.BlockSpec((rows, d), lambda l, j: (0, 0)),
                  pl.BlockSpec((None, d, tn), lambda l, j: (l, 0, j)),
                  pl.BlockSpec((None, 1, tn), lambda l, j: (l, 0, j))],
        out_specs=pl.BlockSpec((None, rows, tn), lambda l, j: (l, 0, j)),
        out_shape=jax.ShapeDtypeStruct((depth, rows, n6), F32),
        compiler_params=_params("arbitrary", "arbitrary"),
        name="ada_vectors",
    )(cvec, ada_w, ada_b.reshape(depth, 1, n6))


def _axial_angles(rows, rot_dim):
    row = jnp.repeat(jnp.arange(rows, dtype=F32), GRID_W)
    col = jnp.tile(jnp.arange(GRID_W, dtype=F32), rows)
    n_freq = rot_dim // 4
    inv = ROPE_BASE ** (-jnp.arange(n_freq, dtype=F32) / n_freq)
    ang = jnp.concatenate([row[:, None] * inv, col[:, None] * inv], axis=-1)
    return jnp.cos(ang), jnp.sin(ang)


def _rope_tables(n_ctx, seq, width, start, group):
    half = width // 2
    cos, sin = _axial_angles(seq // GRID_W, width)
    lane = jnp.arange(LANES) % group - start
    inside = (lane >= 0) & (lane < width)
    idx = jnp.clip(lane, 0, width - 1) % half
    second = inside & (lane >= half)
    firsth = inside & (lane < half)
    c = jnp.where(inside[None, :], cos[:, idx], 1.0)
    s_prev = jnp.where(second[None, :], sin[:, idx], 0.0)
    s_next = jnp.where(firsth[None, :], -sin[:, idx], 0.0)
    ident = jnp.ones((n_ctx, LANES), F32)
    zero = jnp.zeros((n_ctx, LANES), F32)
    return (jnp.concatenate([ident, c], 0), jnp.concatenate([zero, s_prev], 0),
            jnp.concatenate([zero, s_next], 0))


def _rope(x, cos, s_prev, s_next, half):
    return x * cos + pltpu.roll(x, half, 1) * s_prev + pltpu.roll(x, LANES - half, 1) * s_next


def _attn_proj_kernel(xc_ref, xl_ref, mod_ref, g_ref, w_ref, wvt_ref, vab_ref, qn_ref, qup_ref, kvn_ref,
                      kvup_ref, kvupt_ref, vmb_ref, cda_ref, pda_ref, nda_ref, cm_ref, pm_ref, nm_ref,
                      qda_ref, kda_ref, vat_ref, qm_ref, km_ref, vmt_ref):
    mod = mod_ref[0, 0]
    h = _modulate(_stream_block(xc_ref, xl_ref), g_ref[...], mod[0:1], mod[1:2]).astype(BF16)
    proj = _dot(h, w_ref[...])
    vat_ref[0] = (_dot_nt(wvt_ref[...], h) + vab_ref[...]).astype(BF16)
    cda, pda, nda = cda_ref[...], pda_ref[...], nda_ref[...]
    cm, pm, nm = cm_ref[...], pm_ref[...], nm_ref[...]
    n_da = DA_HEADS * LANES
    for hh in range(DA_HEADS):
        sl = slice(LANES * hh, LANES * (hh + 1))
        q = proj[:, LANES * hh:LANES * (hh + 1)]
        k = proj[:, n_da + LANES * hh:n_da + LANES * (hh + 1)]
        qda_ref[0, :, sl] = (_rope(q, cda, pda, nda, DA_QK // 2) * (DA_SCALE * LOG2E)).astype(BF16)
        kda_ref[0, :, sl] = _rope(k, cda, pda, nda, DA_QK // 2).astype(BF16)
    o = 2 * n_da
    cq = proj[:, o:o + MLA_Q_RANK]
    ckv = proj[:, o + MLA_Q_RANK:o + MLA_Q_RANK + MLA_KV_RANK]
    kr = proj[:, o + MLA_Q_RANK + MLA_KV_RANK:o + MLA_Q_RANK + MLA_KV_RANK + LANES]
    qm = _dot((_rms(cq) * qn_ref[...]).astype(BF16), qup_ref[...])
    ckvn = (_rms(ckv) * kvn_ref[...]).astype(BF16)
    kv = _dot(ckvn, kvup_ref[...])
    vmt_ref[0] = (_dot_nt(kvupt_ref[...], ckvn) + vmb_ref[...]).astype(BF16)
    rep = kr + pltpu.roll(kr, 32, 1) + pltpu.roll(kr, 64, 1) + pltpu.roll(kr, 96, 1)
    lane = lax.broadcasted_iota(jnp.int32, (1, LANES), 1)
    rep = jnp.where((lane >= MLA_NOPE) & (lane < MLA_NOPE + MLA_ROPE), rep, 0.0)
    rep = _rope(rep, cm, pm, nm, MLA_ROPE // 2)
    for hh in range(MLA_HEADS):
        sl = slice(LANES * hh, LANES * (hh + 1))
        qm_ref[0, :, sl] = (_rope(qm[:, sl], cm, pm, nm, MLA_ROPE // 2) * (MLA_SCALE * LOG2E)).astype(BF16)
        km_ref[0, :, sl] = (kv[:, sl] + rep).astype(BF16)


def _with_sum_rows(wt, heads):
    dv = wt.shape[0] // heads
    w3 = jnp.pad(wt.reshape(heads, dv, -1), ((0, 0), (0, SUM_ROWS), (0, 0)))
    bias = jnp.pad(jnp.zeros((heads, dv, 1), F32), ((0, 0), (0, SUM_ROWS), (0, 0)), constant_values=1.0)
    return w3.reshape(heads * (dv + SUM_ROWS), -1).astype(BF16), bias.reshape(heads * (dv + SUM_ROWS), 1)


def _attn_project(xc, xl, mod, g, w_in, qn_g, q_up, kvn_g, kv_up, tabs_da, tabs_m):
    bsz, n_ctx, d = xc.shape
    t = n_ctx + xl.shape[1]
    tm = ROW_TILE
    n_da = DA_HEADS * LANES
    nm = MLA_HEADS * LANES
    nv = MLA_HEADS * MLA_V
    w = jnp.concatenate([w_in[:, :2 * n_da], w_in[:, 3 * n_da:]], axis=1)
    w = jnp.pad(w, ((0, 0), (0, (-w.shape[1]) % LANES))).astype(BF16)
    wvt, vab = _with_sum_rows(w_in[:, 2 * n_da:3 * n_da].T, DA_HEADS)
    qup = jnp.pad(q_up.reshape(MLA_Q_RANK, MLA_HEADS, MLA_NOPE + MLA_ROPE),
                  ((0, 0), (0, 0), (0, LANES - MLA_NOPE - MLA_ROPE))).reshape(MLA_Q_RANK, -1).astype(BF16)
    kv3 = kv_up.reshape(MLA_KV_RANK, MLA_HEADS, MLA_NOPE + MLA_V)
    kvup = jnp.pad(kv3[:, :, :MLA_NOPE],
                   ((0, 0), (0, 0), (0, LANES - MLA_NOPE))).reshape(MLA_KV_RANK, -1).astype(BF16)
    kvupt, vmb = _with_sum_rows(kv3[:, :, MLA_NOPE:].reshape(MLA_KV_RANK, -1).T, MLA_HEADS)
    row = lambda b, i: (b, i, 0)
    col = lambda b, i: (b, 0, i)
    const = lambda b, i: (0, 0)
    tab = pl.BlockSpec((tm, LANES), lambda b, i: (i, 0))
    rows_out = lambda n: (pl.BlockSpec((1, tm, n), row), jax.ShapeDtypeStruct((bsz, t, n), BF16))
    cols_out = lambda n: (pl.BlockSpec((1, n, tm), col), jax.ShapeDtypeStruct((bsz, n, t), BF16))
    outs = [rows_out(n_da), rows_out(n_da), cols_out(wvt.shape[0]), rows_out(nm), rows_out(nm),
            cols_out(kvupt.shape[0])]
    return pl.pallas_call(
        _attn_proj_kernel,
        grid=(bsz, t // tm),
        in_specs=_stream_specs(d) + [
            _mod_spec(d),
            pl.BlockSpec((1, d), const),
            pl.BlockSpec(w.shape, const),
            pl.BlockSpec(wvt.shape, const),
            pl.BlockSpec(vab.shape, const),
            pl.BlockSpec((1, MLA_Q_RANK), const),
            pl.BlockSpec(qup.shape, const),
            pl.BlockSpec((1, MLA_KV_RANK), const),
            pl.BlockSpec(kvup.shape, const),
            pl.BlockSpec(kvupt.shape, const),
            pl.BlockSpec(vmb.shape, const),
            tab, tab, tab, tab, tab, tab],
        out_specs=[o[0] for o in outs],
        out_shape=[o[1] for o in outs],
        compiler_params=_params("arbitrary", "arbitrary"),
        name="attn_project",
    )(xc, xl, mod, g.reshape(1, d), w, wvt, vab, qn_g.reshape(1, -1), qup, kvn_g.reshape(1, -1), kvup, kvupt,
      vmb, *tabs_da, *tabs_m)


def _reduce_rows(x, fn):
    n, w = x.shape
    slab = 256
    if n > slab and n % slab == 0:
        x = fn(x.reshape(n // slab, slab, w), axis=0)
        n = slab
    if n > 32 and n % 32 == 0:
        x = fn(x.reshape(n // 32, 32, w), axis=0)
    return fn(x, axis=0, keepdims=True)


def _softmax_parts(s):
    e = jnp.exp2(s - _reduce_rows(s, jnp.max))
    return e.astype(BF16), 1.0 / _reduce_rows(e, jnp.sum)


def _attn_kernel(qda_ref, qm_ref, kda_ref, vat_ref, km_ref, vmt_ref, lam_ref, subg_ref, wout_ref,
                 x_ref, mod_ref, g2_ref, wrt_ref, o_ref, h_ref, lg_ref, att_ref, s_ref, *, lam_init):
    @pl.when(pl.program_id(0) == 0)
    def _():
        att_ref[...] = jnp.zeros_like(att_ref)

    mod = mod_ref[0, 0]
    y = x_ref[0] + mod[2:3] * _dot_tn(att_ref[...], wout_ref[...])
    o_ref[0] = y
    route = _route_split(y, mod, g2_ref, h_ref)

    lamv = lam_ref[...]
    lam = (jnp.exp(jnp.sum(lamv[0:1] * lamv[1:2], axis=1, keepdims=True))
           - jnp.exp(jnp.sum(lamv[2:3] * lamv[3:4], axis=1, keepdims=True)) + lam_init)
    first = lax.broadcasted_iota(jnp.int32, (1, LANES), 1) < LANES // 2

    n_maps = 2 * DA_HEADS + MLA_HEADS
    maps = []
    for i in range(2 * DA_HEADS):
        sl = slice(LANES * (i // 2), LANES * (i // 2 + 1))
        q = qda_ref[0, :, sl].astype(F32)
        q = jnp.where(first, q, 0.0) if i % 2 == 0 else jnp.where(first, 0.0, q)
        rows = DA_V + SUM_ROWS
        maps.append((q.astype(BF16), kda_ref, sl, vat_ref, slice(rows * (i // 2), rows * (i // 2 + 1)), DA_V))
    for i in range(MLA_HEADS):
        sl = slice(LANES * i, LANES * (i + 1))
        rows = MLA_V + SUM_ROWS
        maps.append((qm_ref[0, :, sl], km_ref, sl, vmt_ref, slice(rows * i, rows * (i + 1)), MLA_V))

    kb = min(ATTN_KEY_BLOCK, kda_ref.shape[1])
    n_blocks = kda_ref.shape[1] // kb

    units = [(i, j) for i in range(n_maps) for j in range(n_blocks)]

    def scores(u):
        i, j = units[u]
        q, k_ref, ksl = maps[i][:3]
        return _dot_nt(k_ref[0, j * kb:(j + 1) * kb, ksl], q)

    for u in range(min(ATTN_LOOKAHEAD, len(units))):
        s_ref[u % ATTN_LOOKAHEAD, 0:kb, :] = scores(u)
    heads = []
    for u, (i, j) in enumerate(units):
        s = s_ref[u % ATTN_LOOKAHEAD, 0:kb, :]
        if u + ATTN_LOOKAHEAD < len(units):
            s_ref[u % ATTN_LOOKAHEAD, 0:kb, :] = scores(u + ATTN_LOOKAHEAD)
        if u == min(ATTN_ROUTER_UNIT, len(units) - 1):
            _route_logits(route, wrt_ref, lg_ref)
        v_ref, vsl, dv = maps[i][3:]
        vt = v_ref[0, vsl, j * kb:(j + 1) * kb]
        mb = _reduce_rows(s, jnp.max)
        if j == 0:
            m = mb
            acc = _dot(vt, jnp.exp2((s - m).astype(BF16)))
        else:
            mn = jnp.maximum(m, mb)
            acc = acc * jnp.exp2(m - mn) + _dot(vt, jnp.exp2((s - mn).astype(BF16)))
            m = mn
        if j == n_blocks - 1:
            heads.append(acc[0:dv] * (1.0 / acc[dv:dv + 1]))
    outs = []
    for h in range(DA_HEADS):
        oa = heads[2 * h] - heads[2 * h + 1] * lam
        oa = oa * lax.rsqrt(jnp.mean(oa * oa, axis=0, keepdims=True) + EPS)
        outs.append(oa * subg_ref[...] * (1.0 - lam_init))
    outs.extend(heads[2 * DA_HEADS:])
    att_ref[...] = jnp.concatenate(outs, axis=0).astype(BF16)


def _attention(qda, kda, vat, qm, km, vmt, lamv, subln_g, w_out, x, mod, g2, w_router, group, q_block0,
               n_keys, lam_init):
    bsz, nq, d = x.shape
    tq = ROW_TILE
    per = nq // tq
    n_tiles = bsz * per
    ne = w_router.shape[1]
    cur = lambda s: jnp.minimum(s, n_tiles - 1)
    prev = lambda s: jnp.maximum(s - 1, 0)
    qrow = lambda s: (cur(s) // per, cur(s) % per + q_block0, 0)
    full = lambda s: (cur(s) // per, 0, 0)
    row = lambda s: (prev(s) // per, prev(s) % per, 0)
    const = lambda s: (0, 0)
    wout = w_out.astype(BF16)
    return pl.pallas_call(
        functools.partial(_attn_kernel, lam_init=lam_init),
        grid=(n_tiles + 1,),
        in_specs=[pl.BlockSpec((1, tq, qda.shape[2]), qrow),
                  pl.BlockSpec((1, tq, qm.shape[2]), qrow),
                  pl.BlockSpec((1, n_keys, kda.shape[2]), full),
                  pl.BlockSpec((1, vat.shape[1], n_keys), full),
                  pl.BlockSpec((1, n_keys, km.shape[2]), full),
                  pl.BlockSpec((1, vmt.shape[1], n_keys), full),
                  pl.BlockSpec(lamv.shape, const),
                  pl.BlockSpec((DA_V, 1), const),
                  pl.BlockSpec(wout.shape, const),
                  pl.BlockSpec((1, tq, d), row),
                  pl.BlockSpec((1, 1, 8, d), lambda s: (prev(s) // per, group, 0, 0)),
                  pl.BlockSpec((1, d), const),
                  pl.BlockSpec((ne, d), const)],
        out_specs=[pl.BlockSpec((1, tq, d), row),
                   pl.BlockSpec((1, tq, d), row),
                   pl.BlockSpec((1, ne, tq), lambda s: (prev(s) // per, 0, prev(s) % per))],
        out_shape=[jax.ShapeDtypeStruct((bsz, nq, d), F32),
                   jax.ShapeDtypeStruct((bsz, nq, d), BF16),
                   jax.ShapeDtypeStruct((bsz, ne, nq), F32)],
        scratch_shapes=[pltpu.VMEM((w_out.shape[0], tq), BF16),
                        pltpu.VMEM((ATTN_LOOKAHEAD, min(ATTN_KEY_BLOCK, n_keys), tq), F32)],
        compiler_params=_params("arbitrary"),
        name="attention",
    )(qda, qm, kda, vat, km, vmt, lamv, subln_g.reshape(-1, 1), wout, x, mod, g2.reshape(1, d), w_router.T)


def _kth_largest(a, k):
    rows = a.shape[0]
    lo0 = jnp.full((rows, 1), -1.0, F32)
    hi0 = jnp.max(a, axis=1, keepdims=True)

    def cond(c):
        return c[2] > 0

    def body(c):
        lo, hi, _ = c
        mid = 0.5 * (lo + hi)
        open_ = (mid > lo) & (mid < hi)
        cnt = jnp.sum(jnp.where(a > mid, 1.0, 0.0), axis=1, keepdims=True)
        up = open_ & (cnt >= k)
        down = open_ & (cnt < k)
        lo = jnp.where(up, mid, lo)
        hi = jnp.where(down, mid, hi)
        return lo, hi, jnp.sum(jnp.where(open_, 1.0, 0.0))

    _, hi, _ = lax.while_loop(cond, body, (lo0, hi0, jnp.float32(1.0)))
    return hi


def _select_kernel(lg_ref, slot_ref, aff_ref, first_ref, tri_ref, *, ne, cap):
    rows, n = lg_ref.shape
    rt = 256
    for r in range(0, n, rt):
        ri = lax.broadcasted_iota(jnp.int32, (rt, n), 0) + r
        ci = lax.broadcasted_iota(jnp.int32, (rt, n), 1)
        tri_ref[r:r + rt, :] = jnp.where(ri < ci, 1.0, 0.0).astype(BF16)

    lg = lg_ref[...].reshape(rows // ne, ne, n)
    e = jnp.exp(lg - jnp.max(lg, axis=1, keepdims=True))
    aff = (e / jnp.sum(e, axis=1, keepdims=True)).reshape(rows, n)
    aff_ref[...] = aff
    thr = _kth_largest(aff, cap)
    gt = aff > thr
    eq = aff == thr
    need = cap - jnp.sum(jnp.where(gt, 1.0, 0.0), axis=1, keepdims=True)
    tri = tri_ref[...]
    before = _dot(jnp.where(eq, 1.0, 0.0).astype(BF16), tri)
    sel = gt | (eq & (before < need))
    pos = _dot(jnp.where(sel, 1.0, 0.0).astype(BF16), tri)
    slot_ref[...] = jnp.where(sel, pos, -1.0).astype(jnp.int32)
    ti = lax.broadcasted_iota(jnp.int32, (n, LANES), 0)
    bi = lax.broadcasted_iota(jnp.int32, (n, LANES), 1)
    before_block = jnp.where(ti < bi * GATHER_TOKEN_BLOCK, 1.0, 0.0).astype(BF16)
    first_ref[...] = _dot(jnp.where(sel, 1.0, 0.0).astype(BF16), before_block).astype(jnp.int32)


def _select(logits, cap):
    bsz, ne, n = logits.shape
    blk = pl.BlockSpec((bsz * ne, n), lambda i: (0, 0))
    slot, aff, first = pl.pallas_call(
        functools.partial(_select_kernel, ne=ne, cap=cap),
        grid=(1,),
        in_specs=[blk],
        out_specs=[blk, blk, pl.BlockSpec((bsz * ne, LANES), lambda i: (0, 0))],
        out_shape=[jax.ShapeDtypeStruct((bsz * ne, n), jnp.int32),
                   jax.ShapeDtypeStruct((bsz * ne, n), F32),
                   jax.ShapeDtypeStruct((bsz * ne, LANES), jnp.int32)],
        scratch_shapes=[pltpu.VMEM((n, n), BF16)],
        compiler_params=_params("arbitrary"),
        name="moe_select",
    )(logits.reshape(bsz * ne, n))
    return slot.reshape(bsz, ne, n), aff.reshape(bsz, ne, n), first


GATHER_EXPERTS = 8


GATHER_TOKEN_BLOCK = 256
GATHER_WINDOW = 64


def _gather_dense(h_ref, slot_ref, aff_ref, xe_ref, gate_ref, onehot_ref, cap):
    n = h_ref.shape[1]
    pi = lax.broadcasted_iota(jnp.int32, (cap, n), 0)
    for e in range(GATHER_EXPERTS):
        hit = slot_ref[0, e:e + 1, :] == pi
        onehot_ref[e * cap:(e + 1) * cap, :] = jnp.where(hit, 1.0, 0.0).astype(BF16)
        gate_ref[e] = jnp.sum(jnp.where(hit, aff_ref[0, e:e + 1, :], 0.0), axis=1, keepdims=True)
    xe = _dot(onehot_ref[...], h_ref[0]).astype(BF16)
    xe_ref[...] = xe.reshape(GATHER_EXPERTS, cap, xe.shape[1])


def _gather_kernel(first_ref, h_ref, slot_ref, aff_ref, xe_ref, gate_ref, onehot_ref, *scratch, cap, windowed):
    if not windowed:
        _gather_dense(h_ref, slot_ref, aff_ref, xe_ref, gate_ref, onehot_ref, cap)
        return
    win_ref, acc_ref, gacc_ref = scratch
    n = h_ref.shape[1]
    tb, win, ge = GATHER_TOKEN_BLOCK, GATHER_WINDOW, GATHER_EXPERTS
    nblk = n // tb
    row0 = (pl.program_id(0) * pl.num_programs(1) + pl.program_id(1)) * ge
    starts, fits = {}, None
    for e in range(ge):
        for j in range(nblk):
            lo = first_ref[(row0 + e) * (nblk + 1) + j]
            hi = first_ref[(row0 + e) * (nblk + 1) + j + 1]
            start = pl.multiple_of(jnp.minimum((lo // 16) * 16, cap - win), 16)
            starts[e, j] = start
            fit = hi - start <= win
            fits = fit if fits is None else jnp.logical_and(fits, fit)

    @pl.when(fits)
    def _():
        acc_ref[...] = jnp.zeros_like(acc_ref)
        gacc_ref[...] = jnp.zeros_like(gacc_ref)
        ri = lax.broadcasted_iota(jnp.int32, (win, tb), 0)
        for j in range(nblk):
            cols = slice(j * tb, (j + 1) * tb)
            for e in range(ge):
                hit = slot_ref[0, e:e + 1, cols] == ri + starts[e, j]
                win_ref[e * win:(e + 1) * win, :] = jnp.where(hit, 1.0, 0.0).astype(BF16)
                gacc_ref[e, pl.ds(starts[e, j], win), :] += jnp.sum(
                    jnp.where(hit, aff_ref[0, e:e + 1, cols], 0.0), axis=1, keepdims=True)
            x = _dot(win_ref[...], h_ref[0, cols, :])
            for e in range(ge):
                acc_ref[e, pl.ds(starts[e, j], win), :] += x[e * win:(e + 1) * win]
        xe_ref[...] = acc_ref[...].astype(BF16)
        gate_ref[...] = gacc_ref[...]

    @pl.when(jnp.logical_not(fits))
    def _():
        _gather_dense(h_ref, slot_ref, aff_ref, xe_ref, gate_ref, onehot_ref, cap)


def _gather(h, slot, aff, first, cap):
    bsz, n, d = h.shape
    ne = slot.shape[1]
    ge = GATHER_EXPERTS
    nblk = n // GATHER_TOKEN_BLOCK
    windowed = cap > GATHER_WINDOW and nblk > 1
    scratch = [pltpu.VMEM((ge * cap, n), BF16)]
    if windowed:
        scratch += [pltpu.VMEM((ge * GATHER_WINDOW, GATHER_TOKEN_BLOCK), BF16), pltpu.VMEM((ge, cap, d), F32),
                    pltpu.VMEM((ge, cap, 1), F32)]
    grid_spec = pltpu.PrefetchScalarGridSpec(
        num_scalar_prefetch=1,
        grid=(bsz, ne // ge),
        in_specs=[pl.BlockSpec((1, n, d), lambda b, j, first: (b, 0, 0)),
                  pl.BlockSpec((1, ge, n), lambda b, j, first: (b, j, 0)),
                  pl.BlockSpec((1, ge, n), lambda b, j, first: (b, j, 0))],
        out_specs=[pl.BlockSpec((ge, cap, d), lambda b, j, first: (j, b, 0)),
                   pl.BlockSpec((ge, cap, 1), lambda b, j, first: (j, b, 0))],
        scratch_shapes=scratch)
    return pl.pallas_call(
        functools.partial(_gather_kernel, cap=cap, windowed=windowed),
        grid_spec=grid_spec,
        out_shape=[jax.ShapeDtypeStruct((ne, bsz * cap, d), BF16),
                   jax.ShapeDtypeStruct((ne, bsz * cap, 1), F32)],
        compiler_params=_params("arbitrary", "arbitrary"),
        name="moe_gather",
    )(first[:, :nblk + 1].reshape(-1), h, slot, aff)


def _ffn_kernel(*refs, nseg, rows_per_pass):
    x_refs = refs[:nseg]
    gate_refs = refs[nseg:2 * nseg]
    wg_ref, wu_ref, wd_ref = refs[2 * nseg:2 * nseg + 3]
    y_refs = refs[2 * nseg + 3:3 * nseg + 3]
    acc_refs = refs[3 * nseg + 3:]
    f = pl.program_id(1)
    passes = []
    for x_ref, acc_ref in zip(x_refs, acc_refs):
        m = x_ref.shape[1]
        step = min(rows_per_pass, m)
        passes += [(x_ref, acc_ref, slice(r, r + step)) for r in range(0, m, step)]

    def run(first):
        wg = wg_ref[0].astype(BF16)
        wu = wu_ref[0].astype(BF16)
        wd = wd_ref[0].astype(BF16)
        for x_ref, acc_ref, rows in passes:
            x = x_ref[0, rows, :]
            part = _dot((_silu(_dot(x, wg)) * _dot(x, wu)).astype(BF16), wd)
            if first:
                acc_ref[rows, :] = part
            else:
                acc_ref[rows, :] += part

    @pl.when(f == 0)
    def _():
        run(True)

    @pl.when(f > 0)
    def _():
        run(False)

    @pl.when(f == pl.num_programs(1) - 1)
    def _():
        for y_ref, gate_ref, acc_ref in zip(y_refs, gate_refs, acc_refs):
            y_ref[0] = (acc_ref[...] * gate_ref[0]).astype(BF16)


def _ffn(xes, gates, w_gate, w_up, w_down, layer):
    _, ne, d, ff = w_gate.shape
    tf = 256
    nseg = len(xes)
    return pl.pallas_call(
        functools.partial(_ffn_kernel, nseg=nseg, rows_per_pass=512),
        grid=(ne, ff // tf),
        in_specs=[pl.BlockSpec((1, xe.shape[1], d), lambda e, f: (e, 0, 0)) for xe in xes]
        + [pl.BlockSpec((1, xe.shape[1], 1), lambda e, f: (e, 0, 0)) for xe in xes]
        + [pl.BlockSpec((None, 1, d, tf), lambda e, f: (layer, e, 0, f)),
           pl.BlockSpec((None, 1, d, tf), lambda e, f: (layer, e, 0, f)),
           pl.BlockSpec((None, 1, tf, d), lambda e, f: (layer, e, f, 0))],
        out_specs=[pl.BlockSpec((1, xe.shape[1], d), lambda e, f: (e, 0, 0)) for xe in xes],
        out_shape=[jax.ShapeDtypeStruct(xe.shape, BF16) for xe in xes],
        scratch_shapes=[pltpu.VMEM(xe.shape[1:], F32) for xe in xes],
        compiler_params=_params("arbitrary", "arbitrary"),
        name="moe_ffn",
    )(*xes, *gates, w_gate, w_up, w_down)


def _combine_kernel(*refs, cap, final_norm):
    slot_ref, ye_ref, x_ref, mod_ref = refs[:4]
    if final_norm:
        fg_ref, o_ref, onehot_ref = refs[4:]
    else:
        o_ref, onehot_ref = refs[4:]
    ne = ye_ref.shape[0]
    if cap % LANES == 0:
        pi = lax.broadcasted_iota(jnp.int32, (1, cap), 1)
        for e in range(ne):
            hit = slot_ref[0, :, e:e + 1] == pi
            onehot_ref[:, e * cap:(e + 1) * cap] = jnp.where(hit, 1.0, 0.0).astype(BF16)
    else:
        ei = lax.broadcasted_iota(jnp.int32, (ne, ne * cap), 0)
        li = lax.broadcasted_iota(jnp.int32, (ne, ne * cap), 1)
        spread = jnp.where((li >= ei * cap) & (li < (ei + 1) * cap), 1.0, 0.0).astype(BF16)
        slots = _dot(slot_ref[0].astype(F32).astype(BF16), spread)
        group = jnp.sum(jnp.where(li >= (ei + 1) * cap, 1, 0), axis=0, keepdims=True)
        pos = (li[0:1] - group * cap).astype(F32)
        onehot_ref[...] = jnp.where(slots == pos, 1.0, 0.0).astype(BF16)
    ye = ye_ref[...]
    moe = _dot(onehot_ref[...], ye.reshape(ne * cap, ye.shape[2]))
    y = x_ref[0] + mod_ref[0, 0][5:6] * moe
    if final_norm:
        y = _rms(y) * fg_ref[...]
    o_ref[0] = y


def _combine(slot_t, ye, x, mod, group, cap, final_g):
    bsz, n, d = x.shape
    ne = slot_t.shape[2]
    tm = min(n, 1024)
    final_norm = final_g is not None
    in_specs = [pl.BlockSpec((1, tm, ne), lambda b, i: (b, i, 0)),
                pl.BlockSpec((ne, cap, d), lambda b, i: (0, b, 0)),
                pl.BlockSpec((1, tm, d), lambda b, i: (b, i, 0)),
                _mod_spec(d, group)]
    args = [slot_t, ye, x, mod]
    if final_norm:
        in_specs.append(pl.BlockSpec((1, d), lambda b, i: (0, 0)))
        args.append(final_g.reshape(1, d))
    return pl.pallas_call(
        functools.partial(_combine_kernel, cap=cap, final_norm=final_norm),
        grid=(bsz, n // tm),
        in_specs=in_specs,
        out_specs=pl.BlockSpec((1, tm, d), lambda b, i: (b, i, 0)),
        out_shape=jax.ShapeDtypeStruct((bsz, n, d), F32),
        scratch_shapes=[pltpu.VMEM((tm, ne * cap), BF16)],
        compiler_params=_params("arbitrary", "arbitrary"),
        name="moe_combine",
    )(*args)


def _moe(streams, groups, mod, w_gate, w_up, w_down, layer, final_g=None):
    xs = [s[0] for s in streams]
    routed = []
    for x, h, logits in streams:
        cap = EC_CAPACITY_FACTOR * x.shape[1] // logits.shape[1]
        slot, aff, first = _select(logits, cap)
        xe, gate = _gather(h, slot, aff, first, cap)
        routed.append((cap, slot, xe, gate))
    yes = _ffn([r[2] for r in routed], [r[3] for r in routed], w_gate, w_up, w_down, layer)
    return [_combine(jnp.swapaxes(slot, 1, 2), ye, x, mod, group, cap, final_g)
            for x, group, ye, (cap, slot, _, _) in zip(xs, groups, yes, routed)]


def _chunk_masks(tm, direction):
    ri = lax.broadcasted_iota(jnp.int32, (tm, tm), 0)
    ci = lax.broadcasted_iota(jnp.int32, (tm, tm), 1)
    shift = CHUNK.bit_length() - 1
    same = jnp.right_shift(ri, shift) == jnp.right_shift(ci, shift)
    return same & ((ci <= ri) if direction == 0 else (ci >= ri))


def _chunk_logdecay(la, direction):
    cum = jnp.where(_chunk_masks(la.shape[0], direction), 1.0, 0.0).astype(BF16)
    hi, lo = _split2(la * LOG2E)
    return _dot(cum, hi) + _dot(cum, lo)


def _scan_operands(q, k, b, direction, refs):
    q_in_ref, q_mid_ref, k_mid_ref, k_out_ref, decay_ref = refs
    tm, n = q.shape
    nc = tm // CHUNK
    b = b.reshape(nc, CHUNK, n)
    last, mid = (CHUNK - 1, CHUNK // 2) if direction == 0 else (0, CHUNK // 2 - 1)
    b_tot = b[:, last:last + 1]
    b_mid = b[:, mid:mid + 1]
    q = q.reshape(nc, CHUNK, n)
    k = k.reshape(nc, CHUNK, n)
    q_in_ref[direction, 0] = (q * jnp.exp2(b)).astype(BF16).reshape(tm, n)
    q_mid_ref[direction, 0] = (q * jnp.exp2(b - b_mid)).astype(BF16).reshape(tm, n)
    k_mid_ref[direction, 0] = (k * jnp.exp2(b_mid - b)).astype(BF16).reshape(tm, n)
    k_out_ref[direction, 0] = (k * jnp.exp2(b_tot - b)).astype(BF16).reshape(tm, n)
    decay = jnp.exp2(b_tot).reshape(nc, n)
    decay_ref[direction, 0, 0] = jnp.concatenate([decay] * (8 // nc), axis=0)


def _rec_proj_kernel(xc_ref, xl_ref, mod_ref, g_ref, w_ref, gkup_ref, gkb_ref, lbl_ref, *out_refs, layer):
    gla_refs, gv_ref, gg_ref = out_refs[0:5], out_refs[5], out_refs[6]
    hg_refs, hv_ref, hg_ref = out_refs[7:12], out_refs[12], out_refs[13]
    mod = mod_ref[0, 0]
    h = _modulate(_stream_block(xc_ref, xl_ref), g_ref[...], mod[0:1], mod[1:2]).astype(BF16)
    n = GLA_HEADS * LANES
    c0 = 2 * n + LANES
    c1 = c0 + 3 * n
    head = _dot(h, w_ref[:, 0:c0])
    gate = _dot(head[:, 2 * n:c0].astype(BF16), gkup_ref[...]) + gkb_ref[...]
    qk = _dot(h, w_ref[:, c0:c1])
    lbl = lbl_ref[...]
    e = jnp.exp(lbl - jnp.max(lbl, axis=0, keepdims=True))
    sm = e / jnp.sum(e, axis=0, keepdims=True)
    lb = jnp.sum(sm[0:layer + 1], axis=0, keepdims=True) - sm[0:1]
    f = lb + (1.0 - lb) * jax.nn.sigmoid(head[:, 0:2 * n])
    fs = [f[:, dd * n:(dd + 1) * n] for dd in range(2)]
    rest = _dot(h, w_ref[:, c1:])
    b_hg = [_chunk_logdecay(jnp.log(fs[dd]), dd) for dd in range(2)]
    b_gla = [_chunk_logdecay(jax.nn.log_sigmoid(gate[:, dd * n:(dd + 1) * n]) / GLA_GATE_NORM, dd)
             for dd in range(2)]
    hq = _silu(qk[:, 2 * n:3 * n])
    gq = qk[:, 0:n] * GLA_DK ** -0.5
    gk = qk[:, n:2 * n]
    for dd in range(2):
        _scan_operands(hq, 1.0 - fs[dd], b_hg[dd], dd, hg_refs)
    for dd in range(2):
        _scan_operands(gq, gk, b_gla[dd], dd, gla_refs)
    gv_ref[0] = rest[:, 0:n].astype(BF16)
    gg_ref[0] = rest[:, n:2 * n].astype(BF16)
    hv_ref[0] = rest[:, 2 * n:3 * n].astype(BF16)
    hg_ref[0] = rest[:, 3 * n:4 * n].astype(BF16)


def _pad_heads(w, heads, dk):
    lead = w.shape[:-1]
    w3 = w.reshape(lead + (heads, dk))
    return jnp.pad(w3, [(0, 0)] * len(lead) + [(0, 0), (0, LANES - dk)]).reshape(lead + (heads * LANES,))


def _rec_project(xc, xl, mod, g, w_in, gk_up, gk_bias, lb_logits, layer):
    bsz, n_ctx, d = xc.shape
    t = n_ctx + xl.shape[1]
    tm = ROW_TILE
    hk = GLA_HEADS * GLA_DK
    hv = GLA_HEADS * GLA_DV
    hf = HG_HEADS * HG_DF
    sizes = (hk, hk, hv, hv, 2 * GLA_GATE_RANK, hf, 2 * hf, HG_HEADS * HG_DV, HG_HEADS * HG_DV)
    offs = [0]
    for s in sizes:
        offs.append(offs[-1] + s)
    gq, gk, gv, gg, gdown, hq, hff, hi, hgt = [w_in[:, offs[i]:offs[i + 1]] for i in range(9)]
    w = jnp.concatenate([hff, jnp.pad(gdown, ((0, 0), (0, LANES - 2 * GLA_GATE_RANK))),
                         _pad_heads(gq, GLA_HEADS, GLA_DK), _pad_heads(gk, GLA_HEADS, GLA_DK), hq,
                         gv, gg, hi, hgt], axis=1).astype(BF16)
    n = GLA_HEADS * LANES
    ups = [_pad_heads(gk_up[dd], GLA_HEADS, GLA_DK) for dd in range(2)]
    zero = jnp.zeros_like(ups[0])
    gkup = jnp.concatenate([jnp.concatenate([ups[0], zero], axis=1), jnp.concatenate([zero, ups[1]], axis=1),
                            jnp.zeros((LANES - 2 * GLA_GATE_RANK, 2 * n), F32)], axis=0).astype(BF16)
    gkb = jnp.concatenate([_pad_heads(gk_bias[dd], GLA_HEADS, GLA_DK) for dd in range(2)]).reshape(1, 2 * n)
    row = lambda b, i: (b, i, 0)
    drow = lambda b, i: (0, b, i, 0)
    const = lambda b, i: (0, 0)
    one = lambda dt: (pl.BlockSpec((1, tm, n), row), jax.ShapeDtypeStruct((bsz, t, n), dt))
    two = (pl.BlockSpec((2, 1, tm, n), drow), jax.ShapeDtypeStruct((2, bsz, t, n), BF16))
    dec = (pl.BlockSpec((2, 1, 1, 8, n), lambda b, i: (0, b, i, 0, 0)),
           jax.ShapeDtypeStruct((2, bsz, t // tm, 8, n), F32))
    mixer = [two, two, two, two, dec, one(BF16), one(BF16)]
    outs = mixer + mixer
    return pl.pallas_call(
        functools.partial(_rec_proj_kernel, layer=layer),
        grid=(bsz, t // tm),
        in_specs=_stream_specs(d) + [
            _mod_spec(d),
            pl.BlockSpec((1, d), const),
            pl.BlockSpec(w.shape, const),
            pl.BlockSpec(gkup.shape, const),
            pl.BlockSpec(gkb.shape, const),
            pl.BlockSpec((lb_logits.shape[0], 2 * hf), const)],
        out_specs=[o[0] for o in outs],
        out_shape=[o[1] for o in outs],
        compiler_params=_params("arbitrary", "arbitrary"),
        name="rec_project",
    )(xc, xl, mod, g.reshape(1, d), w, gkup, gkb, lb_logits.reshape(lb_logits.shape[0], 2 * hf))


def _scan_kernel(*refs, heads, directions):
    ns = len(directions)
    ins = [refs[6 * i:6 * i + 6] for i in range(ns)]
    o_refs = refs[6 * ns:7 * ns]
    st_ref = refs[7 * ns]
    tm = o_refs[0].shape[1]
    nc = tm // CHUNK

    @pl.when(pl.program_id(1) == 0)
    def _():
        st_ref[...] = jnp.zeros_like(st_ref)

    lanes = [slice(LANES * h, LANES * (h + 1)) for h in range(heads)]
    chunk_rows = [slice(c * CHUNK, (c + 1) * CHUNK) for c in range(nc)]
    units = [(i, h) for i in range(ns) for h in range(heads)]
    vs = {(i, h): ins[i][5][0, :, lanes[h]] for i, h in units}
    scores = {(i, h): _dot_nt(ins[i][1][0, 0, :, lanes[h]], ins[i][2][0, 0, :, lanes[h]]) for i, h in units}
    updates = {(i, h): [_dot_tn(vs[i, h][rows], ins[i][3][0, 0, rows, lanes[h]]) for rows in chunk_rows]
               for i, h in units}
    causal = [_chunk_masks(tm, d) for d in range(2)]
    intra = {(i, h): _dot(jnp.where(causal[directions[i]], scores[i, h], 0.0).astype(BF16), vs[i, h])
             for i, h in units}
    order = [list(range(nc)), list(range(nc - 1, -1, -1))]
    entering = {}
    for i, h in units:
        st = st_ref[i, h]
        for c in order[directions[i]]:
            entering[i, h, c] = st.astype(BF16)
            st = st * ins[i][4][0, 0, 0, c:c + 1, lanes[h]] + updates[i, h][c]
        st_ref[i, h] = st
    for i, h in units:
        for c in order[directions[i]]:
            rows = chunk_rows[c]
            inter = _dot_nt(ins[i][0][0, 0, rows, lanes[h]], entering[i, h, c])
            o_refs[i][0, rows, lanes[h]] = (intra[i, h][rows] + inter).astype(BF16)


def _scan(mixers, n_ctx):
    _, bsz, t, n = mixers[0][0].shape
    heads = n // LANES
    tm = ROW_TILE
    nb = t // tm
    ncb = n_ctx // tm

    def blk(d, s):
        return s if d == 0 else jnp.where(s < ncb, ncb - 1 - s, nb - 1 - (s - ncb))

    in_specs, args, directions = [], [], []
    for q_in, q_mid, k_mid, k_out, decay, v in mixers:
        for d in range(2):
            drow = lambda b, s, d=d: (d, b, blk(d, s), 0)
            in_specs += [pl.BlockSpec((1, 1, tm, n), drow)] * 4
            in_specs += [pl.BlockSpec((1, 1, 1, 8, n), lambda b, s, d=d: (d, b, blk(d, s), 0, 0)),
                         pl.BlockSpec((1, tm, n), lambda b, s, d=d: (b, blk(d, s), 0))]
            args += [q_in, q_mid, k_mid, k_out, decay, v]
            directions.append(d)
    outs = pl.pallas_call(
        functools.partial(_scan_kernel, heads=heads, directions=tuple(directions)),
        grid=(bsz, nb),
        in_specs=in_specs,
        out_specs=[pl.BlockSpec((1, tm, n), lambda b, s, d=d: (b, blk(d, s), 0)) for d in directions],
        out_shape=[jax.ShapeDtypeStruct((bsz, t, n), BF16)] * len(directions),
        scratch_shapes=[pltpu.VMEM((len(directions), heads, LANES, LANES), F32)],
        compiler_params=_params("arbitrary", "arbitrary"),
        name="rec_scan",
    )(*args)
    return [tuple(outs[2 * i:2 * i + 2]) for i in range(len(mixers))]


def _rec_out_kernel(og0_ref, og1_ref, oh0_ref, oh1_ref, gg_ref, hg_ref, gn_ref, hn_ref, wout_ref,
                    x_ref, mod_ref, g2_ref, wrt_ref, o_ref, h_ref, lg_ref):
    parts = []
    for fwd_ref, bwd_ref, gate_ref, n_ref in ((og0_ref, og1_ref, gg_ref, gn_ref),
                                              (oh0_ref, oh1_ref, hg_ref, hn_ref)):
        for h in range(fwd_ref.shape[2] // LANES):
            sl = slice(LANES * h, LANES * (h + 1))
            o = fwd_ref[0, :, sl].astype(F32) + bwd_ref[0, :, sl].astype(F32)
            parts.append(_rms(o) * n_ref[...] * _silu(gate_ref[0, :, sl].astype(F32)))
    a = jnp.concatenate(parts, axis=1).astype(BF16)
    mod = mod_ref[0, 0]
    y = x_ref[0] + mod[2:3] * _dot(a, wout_ref[...])
    o_ref[0] = y
    _route_rows(y, mod, g2_ref, wrt_ref, h_ref, lg_ref)


def _rec_readout(og, oh, gg, hg, gn, hn, w_out, xl, mod, g2, w_router, n_ctx):
    bsz, seq, d = xl.shape
    tm = ROW_TILE
    ncb = n_ctx // tm
    n = gg.shape[2]
    row = lambda b, i: (b, i + ncb, 0)
    const = lambda b, i: (0, 0)
    wout = w_out.astype(BF16)
    r_in, r_out, r_shape = _route_specs(bsz, seq, d, w_router.shape[1], tm)
    return pl.pallas_call(
        _rec_out_kernel,
        grid=(bsz, seq // tm),
        in_specs=[pl.BlockSpec((1, tm, n), row),
                  pl.BlockSpec((1, tm, n), row),
                  pl.BlockSpec((1, tm, n), row),
                  pl.BlockSpec((1, tm, n), row),
                  pl.BlockSpec((1, tm, n), row),
                  pl.BlockSpec((1, tm, n), row),
                  pl.BlockSpec((1, LANES), const),
                  pl.BlockSpec((1, LANES), const),
                  pl.BlockSpec(wout.shape, const),
                  pl.BlockSpec((1, tm, d), lambda b, i: (b, i, 0)),
                  _mod_spec(d, 1)] + r_in,
        out_specs=[pl.BlockSpec((1, tm, d), lambda b, i: (b, i, 0))] + r_out,
        out_shape=[jax.ShapeDtypeStruct((bsz, seq, d), F32)] + r_shape,
        compiler_params=_params("arbitrary", "arbitrary"),
        name="rec_readout",
    )(*og, *oh, gg, hg, gn.reshape(1, -1), hn.reshape(1, -1), wout, xl, mod, g2.reshape(1, d), w_router.T)


def kernel(x, c, ctx, c_ctx, ada_w, ada_b, norm1_g, norm2_g, att_w_in, mla_q_norm_g, mla_q_up, mla_kv_norm_g, mla_kv_up, da_lam_q1, da_lam_k1, da_lam_q2, da_lam_k2, da_subln_g, att_w_out, rec_w_in, gla_gk_up, gla_gk_bias, gla_norm_g, hg_lb_logits, hg_norm_g, rec_w_out, moe_router, moe_w_gate, moe_w_up, moe_w_down, final_norm_g):
    bsz, seq, d = x.shape
    n_ctx = ctx.shape[1]
    depth = ada_w.shape[0]
    assert depth == 2 and n_ctx == ROW_TILE and seq % ROW_TILE == 0 and bsz + 1 <= 16

    cvec = jnp.concatenate([c, c_ctx[None, :], jnp.zeros((16 - bsz - 1, d), F32)], axis=0)
    ada = _ada_vectors(cvec, ada_w, ada_b).reshape(depth, 16, 6, d)
    ada = jnp.pad(ada, ((0, 0), (0, 0), (0, 2), (0, 0)))
    mods = jnp.stack([jnp.broadcast_to(ada[:, bsz][:, None], (depth, bsz, 8, d)), ada[:, :bsz]], axis=2)
    xc, xl = ctx, x

    mod = mods[0]
    tabs_da = _rope_tables(n_ctx, seq, DA_QK, 0, DA_QK)
    tabs_m = _rope_tables(n_ctx, seq, MLA_ROPE, MLA_NOPE, LANES)
    proj = _attn_project(xc, xl, mod, norm1_g[0], att_w_in[0], mla_q_norm_g[0], mla_q_up[0],
                         mla_kv_norm_g[0], mla_kv_up[0], tabs_da, tabs_m)
    lamv = jnp.stack([da_lam_q1[0], da_lam_k1[0], da_lam_q2[0], da_lam_k2[0]])
    lam_init = 0.8 - 0.6 * math.exp(-0.3 * 0)
    sc = _attention(*proj, lamv, da_subln_g[0], att_w_out[0], xc, mod, norm2_g[0], moe_router[0], 0, 0, n_ctx,
                    lam_init)
    sl = _attention(*proj, lamv, da_subln_g[0], att_w_out[0], xl, mod, norm2_g[0], moe_router[0], 1,
                    n_ctx // ROW_TILE, n_ctx + seq, lam_init)
    xc, xl = _moe([sc, sl], [0, 1], mod, moe_w_gate, moe_w_up, moe_w_down, 0)

    mod = mods[1]
    rec = _rec_project(xc, xl, mod, norm1_g[1], rec_w_in[0], gla_gk_up[0], gla_gk_bias[0], hg_lb_logits, 1)
    og, oh = _scan([rec[0:6], rec[7:13]], n_ctx)
    sl = _rec_readout(og, oh, rec[6], rec[13], gla_norm_g[0], hg_norm_g[0], rec_w_out[0], xl, mod, norm2_g[1],
                      moe_router[1], n_ctx)
    (out,) = _moe([sl], [1], mod, moe_w_gate, moe_w_up, moe_w_down, 1, final_g=final_norm_g)
    return out
```

```python
import functools
import math

import jax
import jax.numpy as jnp
from jax import lax
from jax.experimental import pallas as pl
from jax.experimental.pallas import tpu as pltpu

F32 = jnp.float32
BF16 = jnp.bfloat16

EPS = 1e-6
GRID_W = 64
ROPE_BASE = 10000.0

DA_HEADS = 4
DA_QK = 64
DA_V = 2 * DA_QK
DA_SCALE = DA_QK ** -0.5
MLA_HEADS = 8
MLA_NOPE = 64
MLA_ROPE = 32
MLA_V = 64
MLA_Q_RANK = 384
MLA_KV_RANK = 256
MLA_SCALE = (MLA_NOPE + MLA_ROPE) ** -0.5
LOG2E = math.log2(math.e)

GLA_HEADS = 4
GLA_DK = 64
GLA_DV = 128
GLA_GATE_RANK = 16
GLA_GATE_NORM = 16.0
HG_HEADS = 4
HG_DF = 128
HG_DV = 128
CHUNK = 64

N_EXPERTS = 16
EC_CAPACITY_FACTOR = 2

LANES = 128
ROW_TILE = 256
ATTN_KEY_BLOCK = 256
SUM_ROWS = 16
ATTN_ROUTER_UNIT = 12
ATTN_LOOKAHEAD = 12
VMEM_LIMIT = 56 * 1024 * 1024


def _params(*sem):
    return pltpu.CompilerParams(dimension_semantics=sem, vmem_limit_bytes=VMEM_LIMIT)


def _dot(a, b):
    return jnp.dot(a, b, preferred_element_type=F32)


def _dot_nt(a, b):
    return lax.dot_general(a, b, (((1,), (1,)), ((), ())), preferred_element_type=F32)


def _dot_tn(a, b):
    return lax.dot_general(a, b, (((0,), (0,)), ((), ())), preferred_element_type=F32)


def _split2(a):
    hi = a.astype(BF16)
    lo = (a - hi.astype(F32)).astype(BF16)
    return hi, lo


def _dot3(a, b, dot=_dot):
    ah, al = _split2(a)
    bh, bl = _split2(b)
    return dot(ah, bh) + (dot(ah, bl) + dot(al, bh))


def _rms(x):
    return x * lax.rsqrt(jnp.mean(x * x, axis=-1, keepdims=True) + EPS)


def _silu(x):
    return x * jax.nn.sigmoid(x)


def _modulate(x, g, shift, scale):
    return _rms(x) * g * (1.0 + scale) + shift


def _stream_specs(d):
    return [pl.BlockSpec((1, ROW_TILE, d), lambda b, i: (b, 0, 0)),
            pl.BlockSpec((1, ROW_TILE, d), lambda b, i: (b, jnp.maximum(i - 1, 0), 0))]


def _stream_block(xc_ref, xl_ref):
    return jnp.where(pl.program_id(1) == 0, xc_ref[0], xl_ref[0])


def _route_rows(y, mod, g2_ref, wrt_ref, h_ref, lg_ref):
    _route_logits(_route_split(y, mod, g2_ref, h_ref), wrt_ref, lg_ref)


def _route_split(y, mod, g2_ref, h_ref):
    hh, hl = _split2(_modulate(y, g2_ref[...], mod[3:4], mod[4:5]))
    h_ref[0] = hh
    return hh, hl


def _route_logits(split, wrt_ref, lg_ref):
    hh, hl = split
    wh, wl = _split2(wrt_ref[...])
    ne = wh.shape[0]
    both = _dot_nt(jnp.concatenate([wh, wl], axis=0), hh)
    lg_ref[0] = both[0:ne] + (_dot_nt(wh, hl) + both[ne:2 * ne])


def _route_specs(bsz, n, d, ne, tm):
    const = lambda b, i: (0, 0)
    return ([pl.BlockSpec((1, d), const), pl.BlockSpec((ne, d), const)],
            [pl.BlockSpec((1, tm, d), lambda b, i: (b, i, 0)), pl.BlockSpec((1, ne, tm), lambda b, i: (b, 0, i))],
            [jax.ShapeDtypeStruct((bsz, n, d), BF16), jax.ShapeDtypeStruct((bsz, ne, n), F32)])


def _mod_spec(d, group=None):
    if group is None:
        return pl.BlockSpec((1, 1, 8, d), lambda b, i: (b, jnp.minimum(i, 1), 0, 0))
    return pl.BlockSpec((1, 1, 8, d), lambda b, i: (b, group, 0, 0))


def _ada_kernel(c_ref, w_ref, b_ref, o_ref):
    o_ref[...] = _dot3(_silu(c_ref[...]), w_ref[...]) + b_ref[...]


def _ada_vectors(cvec, ada_w, ada_b):
    depth, d, n6 = ada_w.shape
    tn = n6 // 4
    rows = cvec.shape[0]
    return pl.pallas_call(
        _ada_kernel,
        grid=(depth, n6 // tn),
        in_specs=[pl.BlockSpec((rows, d), lambda l, j: (0, 0)),
                  pl.BlockSpec((None, d, tn), lambda l, j: (l, 0, j)),
                  pl.BlockSpec((None, 1, tn), lambda l, j: (l, 0, j))],
        out_specs=pl.BlockSpec((None, rows, tn), lambda l, j: (l, 0, j)),
        out_shape=jax.ShapeDtypeStruct((depth, rows, n6), F32),
        compiler_params=_params("arbitrary", "arbitrary"),
        name="ada_vectors",
    )(cvec, ada_w, ada_b.reshape(depth, 1, n6))


def _axial_angles(rows, rot_dim):
    row = jnp.repeat(jnp.arange(rows, dtype=F32), GRID_W)
    col = jnp.tile(jnp.arange(GRID_W, dtype=F32), rows)
    n_freq = rot_dim // 4
    inv = ROPE_BASE ** (-jnp.arange(n_freq, dtype=F32) / n_freq)
    ang = jnp.concatenate([row[:, None] * inv, col[:, None] * inv], axis=-1)
    return jnp.cos(ang), jnp.sin(ang)


def _rope_tables(n_ctx, seq, width, start, group):
    half = width // 2
    cos, sin = _axial_angles(seq // GRID_W, width)
    lane = jnp.arange(LANES) % group - start
    inside = (lane >= 0) & (lane < width)
    idx = jnp.clip(lane, 0, width - 1) % half
    second = inside & (lane >= half)
    firsth = inside & (lane < half)
    c = jnp.where(inside[None, :], cos[:, idx], 1.0)
    s_prev = jnp.where(second[None, :], sin[:, idx], 0.0)
    s_next = jnp.where(firsth[None, :], -sin[:, idx], 0.0)
    ident = jnp.ones((n_ctx, LANES), F32)
    zero = jnp.zeros((n_ctx, LANES), F32)
    return (jnp.concatenate([ident, c], 0), jnp.concatenate([zero, s_prev], 0),
            jnp.concatenate([zero, s_next], 0))


def _rope(x, cos, s_prev, s_next, half):
    return x * cos + pltpu.roll(x, half, 1) * s_prev + pltpu.roll(x, LANES - half, 1) * s_next


def _attn_proj_kernel(xc_ref, xl_ref, mod_ref, g_ref, w_ref, wvt_ref, vab_ref, qn_ref, qup_ref, kvn_ref,
                      kvup_ref, kvupt_ref, vmb_ref, cda_ref, pda_ref, nda_ref, cm_ref, pm_ref, nm_ref,
                      qda_ref, kda_ref, vat_ref, qm_ref, km_ref, vmt_ref):
    mod = mod_ref[0, 0]
    h = _modulate(_stream_block(xc_ref, xl_ref), g_ref[...], mod[0:1], mod[1:2]).astype(BF16)
    proj = _dot(h, w_ref[...])
    vat_ref[0] = (_dot_nt(wvt_ref[...], h) + vab_ref[...]).astype(BF16)
    cda, pda, nda = cda_ref[...], pda_ref[...], nda_ref[...]
    cm, pm, nm = cm_ref[...], pm_ref[...], nm_ref[...]
    n_da = DA_HEADS * LANES
    for hh in range(DA_HEADS):
        sl = slice(LANES * hh, LANES * (hh + 1))
        q = proj[:, LANES * hh:LANES * (hh + 1)]
        k = proj[:, n_da + LANES * hh:n_da + LANES * (hh + 1)]
        qda_ref[0, :, sl] = (_rope(q, cda, pda, nda, DA_QK // 2) * (DA_SCALE * LOG2E)).astype(BF16)
        kda_ref[0, :, sl] = _rope(k, cda, pda, nda, DA_QK // 2).astype(BF16)
    o = 2 * n_da
    cq = proj[:, o:o + MLA_Q_RANK]
    ckv = proj[:, o + MLA_Q_RANK:o + MLA_Q_RANK + MLA_KV_RANK]
    kr = proj[:, o + MLA_Q_RANK + MLA_KV_RANK:o + MLA_Q_RANK + MLA_KV_RANK + LANES]
    qm = _dot((_rms(cq) * qn_ref[...]).astype(BF16), qup_ref[...])
    ckvn = (_rms(ckv) * kvn_ref[...]).astype(BF16)
    kv = _dot(ckvn, kvup_ref[...])
    vmt_ref[0] = (_dot_nt(kvupt_ref[...], ckvn) + vmb_ref[...]).astype(BF16)
    rep = kr + pltpu.roll(kr, 32, 1) + pltpu.roll(kr, 64, 1) + pltpu.roll(kr, 96, 1)
    lane = lax.broadcasted_iota(jnp.int32, (1, LANES), 1)
    rep = jnp.where((lane >= MLA_NOPE) & (lane < MLA_NOPE + MLA_ROPE), rep, 0.0)
    rep = _rope(rep, cm, pm, nm, MLA_ROPE // 2)
    for hh in range(MLA_HEADS):
        sl = slice(LANES * hh, LANES * (hh + 1))
        qm_ref[0, :, sl] = (_rope(qm[:, sl], cm, pm, nm, MLA_ROPE // 2) * (MLA_SCALE * LOG2E)).astype(BF16)
        km_ref[0, :, sl] = (kv[:, sl] + rep).astype(BF16)


def _with_sum_rows(wt, heads):
    dv = wt.shape[0] // heads
    w3 = jnp.pad(wt.reshape(heads, dv, -1), ((0, 0), (0, SUM_ROWS), (0, 0)))
    bias = jnp.pad(jnp.zeros((heads, dv, 1), F32), ((0, 0), (0, SUM_ROWS), (0, 0)), constant_values=1.0)
    return w3.reshape(heads * (dv + SUM_ROWS), -1).astype(BF16), bias.reshape(heads * (dv + SUM_ROWS), 1)


def _attn_project(xc, xl, mod, g, w_in, qn_g, q_up, kvn_g, kv_up, tabs_da, tabs_m):
    bsz, n_ctx, d = xc.shape
    t = n_ctx + xl.shape[1]
    tm = ROW_TILE
    n_da = DA_HEADS * LANES
    nm = MLA_HEADS * LANES
    nv = MLA_HEADS * MLA_V
    w = jnp.concatenate([w_in[:, :2 * n_da], w_in[:, 3 * n_da:]], axis=1)
    w = jnp.pad(w, ((0, 0), (0, (-w.shape[1]) % LANES))).astype(BF16)
    wvt, vab = _with_sum_rows(w_in[:, 2 * n_da:3 * n_da].T, DA_HEADS)
    qup = jnp.pad(q_up.reshape(MLA_Q_RANK, MLA_HEADS, MLA_NOPE + MLA_ROPE),
                  ((0, 0), (0, 0), (0, LANES - MLA_NOPE - MLA_ROPE))).reshape(MLA_Q_RANK, -1).astype(BF16)
    kv3 = kv_up.reshape(MLA_KV_RANK, MLA_HEADS, MLA_NOPE + MLA_V)
    kvup = jnp.pad(kv3[:, :, :MLA_NOPE],
                   ((0, 0), (0, 0), (0, LANES - MLA_NOPE))).reshape(MLA_KV_RANK, -1).astype(BF16)
    kvupt, vmb = _with_sum_rows(kv3[:, :, MLA_NOPE:].reshape(MLA_KV_RANK, -1).T, MLA_HEADS)
    row = lambda b, i: (b, i, 0)
    col = lambda b, i: (b, 0, i)
    const = lambda b, i: (0, 0)
    tab = pl.BlockSpec((tm, LANES), lambda b, i: (i, 0))
    rows_out = lambda n: (pl.BlockSpec((1, tm, n), row), jax.ShapeDtypeStruct((bsz, t, n), BF16))
    cols_out = lambda n: (pl.BlockSpec((1, n, tm), col), jax.ShapeDtypeStruct((bsz, n, t), BF16))
    outs = [rows_out(n_da), rows_out(n_da), cols_out(wvt.shape[0]), rows_out(nm), rows_out(nm),
            cols_out(kvupt.shape[0])]
    return pl.pallas_call(
        _attn_proj_kernel,
        grid=(bsz, t // tm),
        in_specs=_stream_specs(d) + [
            _mod_spec(d),
            pl.BlockSpec((1, d), const),
            pl.BlockSpec(w.shape, const),
            pl.BlockSpec(wvt.shape, const),
            pl.BlockSpec(vab.shape, const),
            pl.BlockSpec((1, MLA_Q_RANK), const),
            pl.BlockSpec(qup.shape, const),
            pl.BlockSpec((1, MLA_KV_RANK), const),
            pl.BlockSpec(kvup.shape, const),
            pl.BlockSpec(kvupt.shape, const),
            pl.BlockSpec(vmb.shape, const),
            tab, tab, tab, tab, tab, tab],
        out_specs=[o[0] for o in outs],
        out_shape=[o[1] for o in outs],
        compiler_params=_params("arbitrary", "arbitrary"),
        name="attn_project",
    )(xc, xl, mod, g.reshape(1, d), w, wvt, vab, qn_g.reshape(1, -1), qup, kvn_g.reshape(1, -1), kvup, kvupt,
      vmb, *tabs_da, *tabs_m)


def _reduce_rows(x, fn):
    n, w = x.shape
    slab = 256
    if n > slab and n % slab == 0:
        x = fn(x.reshape(n // slab, slab, w), axis=0)
        n = slab
    if n > 32 and n % 32 == 0:
        x = fn(x.reshape(n // 32, 32, w), axis=0)
    return fn(x, axis=0, keepdims=True)


def _softmax_parts(s):
    e = jnp.exp2(s - _reduce_rows(s, jnp.max))
    return e.astype(BF16), 1.0 / _reduce_rows(e, jnp.sum)


def _attn_kernel(qda_ref, qm_ref, kda_ref, vat_ref, km_ref, vmt_ref, lam_ref, subg_ref, wout_ref,
                 x_ref, mod_ref, g2_ref, wrt_ref, o_ref, h_ref, lg_ref, att_ref, s_ref, *, lam_init):
    @pl.when(pl.program_id(0) == 0)
    def _():
        att_ref[...] = jnp.zeros_like(att_ref)

    mod = mod_ref[0, 0]
    y = x_ref[0] + mod[2:3] * _dot_tn(att_ref[...], wout_ref[...])
    o_ref[0] = y
    route = _route_split(y, mod, g2_ref, h_ref)

    lamv = lam_ref[...]
    lam = (jnp.exp(jnp.sum(lamv[0:1] * lamv[1:2], axis=1, keepdims=True))
           - jnp.exp(jnp.sum(lamv[2:3] * lamv[3:4], axis=1, keepdims=True)) + lam_init)
    first = lax.broadcasted_iota(jnp.int32, (1, LANES), 1) < LANES // 2

    n_maps = 2 * DA_HEADS + MLA_HEADS
    maps = []
    for i in range(2 * DA_HEADS):
        sl = slice(LANES * (i // 2), LANES * (i // 2 + 1))
        q = qda_ref[0, :, sl].astype(F32)
        q = jnp.where(first, q, 0.0) if i % 2 == 0 else jnp.where(first, 0.0, q)
        rows = DA_V + SUM_ROWS
        maps.append((q.astype(BF16), kda_ref, sl, vat_ref, slice(rows * (i // 2), rows * (i // 2 + 1)), DA_V))
    for i in range(MLA_HEADS):
        sl = slice(LANES * i, LANES * (i + 1))
        rows = MLA_V + SUM_ROWS
        maps.append((qm_ref[0, :, sl], km_ref, sl, vmt_ref, slice(rows * i, rows * (i + 1)), MLA_V))

    kb = min(ATTN_KEY_BLOCK, kda_ref.shape[1])
    n_blocks = kda_ref.shape[1] // kb

    units = [(i, j) for i in range(n_maps) for j in range(n_blocks)]

    def scores(u):
        i, j = units[u]
        q, k_ref, ksl = maps[i][:3]
        return _dot_nt(k_ref[0, j * kb:(j + 1) * kb, ksl], q)

    for u in range(min(ATTN_LOOKAHEAD, len(units))):
        s_ref[u % ATTN_LOOKAHEAD, 0:kb, :] = scores(u)
    heads = []
    for u, (i, j) in enumerate(units):
        s = s_ref[u % ATTN_LOOKAHEAD, 0:kb, :]
        if u + ATTN_LOOKAHEAD < len(units):
            s_ref[u % ATTN_LOOKAHEAD, 0:kb, :] = scores(u + ATTN_LOOKAHEAD)
        if u == min(ATTN_ROUTER_UNIT, len(units) - 1):
            _route_logits(route, wrt_ref, lg_ref)
        v_ref, vsl, dv = maps[i][3:]
        vt = v_ref[0, vsl, j * kb:(j + 1) * kb]
        mb = _reduce_rows(s, jnp.max)
        if j == 0:
            m = mb
            acc = _dot(vt, jnp.exp2((s - m).astype(BF16)))
        else:
            mn = jnp.maximum(m, mb)
            acc = acc * jnp.exp2(m - mn) + _dot(vt, jnp.exp2((s - mn).astype(BF16)))
            m = mn
        if j == n_blocks - 1:
            heads.append(acc[0:dv] * (1.0 / acc[dv:dv + 1]))
    outs = []
    for h in range(DA_HEADS):
        oa = heads[2 * h] - heads[2 * h + 1] * lam
        oa = oa * lax.rsqrt(jnp.mean(oa * oa, axis=0, keepdims=True) + EPS)
        outs.append(oa * subg_ref[...] * (1.0 - lam_init))
    outs.extend(heads[2 * DA_HEADS:])
    att_ref[...] = jnp.concatenate(outs, axis=0).astype(BF16)


def _attention(qda, kda, vat, qm, km, vmt, lamv, subln_g, w_out, x, mod, g2, w_router, group, q_block0,
               n_keys, lam_init):
    bsz, nq, d = x.shape
    tq = ROW_TILE
    per = nq // tq
    n_tiles = bsz * per
    ne = w_router.shape[1]
    cur = lambda s: jnp.minimum(s, n_tiles - 1)
    prev = lambda s: jnp.maximum(s - 1, 0)
    qrow = lambda s: (cur(s) // per, cur(s) % per + q_block0, 0)
    full = lambda s: (cur(s) // per, 0, 0)
    row = lambda s: (prev(s) // per, prev(s) % per, 0)
    const = lambda s: (0, 0)
    wout = w_out.astype(BF16)
    return pl.pallas_call(
        functools.partial(_attn_kernel, lam_init=lam_init),
        grid=(n_tiles + 1,),
        in_specs=[pl.BlockSpec((1, tq, qda.shape[2]), qrow),
                  pl.BlockSpec((1, tq, qm.shape[2]), qrow),
                  pl.BlockSpec((1, n_keys, kda.shape[2]), full),
                  pl.BlockSpec((1, vat.shape[1], n_keys), full),
                  pl.BlockSpec((1, n_keys, km.shape[2]), full),
                  pl.BlockSpec((1, vmt.shape[1], n_keys), full),
                  pl.BlockSpec(lamv.shape, const),
                  pl.BlockSpec((DA_V, 1), const),
                  pl.BlockSpec(wout.shape, const),
                  pl.BlockSpec((1, tq, d), row),
                  pl.BlockSpec((1, 1, 8, d), lambda s: (prev(s) // per, group, 0, 0)),
                  pl.BlockSpec((1, d), const),
                  pl.BlockSpec((ne, d), const)],
        out_specs=[pl.BlockSpec((1, tq, d), row),
                   pl.BlockSpec((1, tq, d), row),
                   pl.BlockSpec((1, ne, tq), lambda s: (prev(s) // per, 0, prev(s) % per))],
        out_shape=[jax.ShapeDtypeStruct((bsz, nq, d), F32),
                   jax.ShapeDtypeStruct((bsz, nq, d), BF16),
                   jax.ShapeDtypeStruct((bsz, ne, nq), F32)],
        scratch_shapes=[pltpu.VMEM((w_out.shape[0], tq), BF16),
                        pltpu.VMEM((ATTN_LOOKAHEAD, min(ATTN_KEY_BLOCK, n_keys), tq), F32)],
        compiler_params=_params("arbitrary"),
        name="attention",
    )(qda, qm, kda, vat, km, vmt, lamv, subln_g.reshape(-1, 1), wout, x, mod, g2.reshape(1, d), w_router.T)


def _kth_largest(a, k):
    rows = a.shape[0]
    lo0 = jnp.full((rows, 1), -1.0, F32)
    hi0 = jnp.max(a, axis=1, keepdims=True)

    def cond(c):
        return c[2] > 0

    def body(c):
        lo, hi, _ = c
        mid = 0.5 * (lo + hi)
        open_ = (mid > lo) & (mid < hi)
        cnt = jnp.sum(jnp.where(a > mid, 1.0, 0.0), axis=1, keepdims=True)
        up = open_ & (cnt >= k)
        down = open_ & (cnt < k)
        lo = jnp.where(up, mid, lo)
        hi = jnp.where(down, mid, hi)
        return lo, hi, jnp.sum(jnp.where(open_, 1.0, 0.0))

    _, hi, _ = lax.while_loop(cond, body, (lo0, hi0, jnp.float32(1.0)))
    return hi


def _select_kernel(lg_ref, slot_ref, aff_ref, first_ref, tri_ref, *, ne, cap):
    rows, n = lg_ref.shape
    rt = 256
    for r in range(0, n, rt):
        ri = lax.broadcasted_iota(jnp.int32, (rt, n), 0) + r
        ci = lax.broadcasted_iota(jnp.int32, (rt, n), 1)
        tri_ref[r:r + rt, :] = jnp.where(ri < ci, 1.0, 0.0).astype(BF16)

    lg = lg_ref[...].reshape(rows // ne, ne, n)
    e = jnp.exp(lg - jnp.max(lg, axis=1, keepdims=True))
    aff = (e / jnp.sum(e, axis=1, keepdims=True)).reshape(rows, n)
    aff_ref[...] = aff
    thr = _kth_largest(aff, cap)
    gt = aff > thr
    eq = aff == thr
    need = cap - jnp.sum(jnp.where(gt, 1.0, 0.0), axis=1, keepdims=True)
    tri = tri_ref[...]
    before = _dot(jnp.where(eq, 1.0, 0.0).astype(BF16), tri)
    sel = gt | (eq & (before < need))
    pos = _dot(jnp.where(sel, 1.0, 0.0).astype(BF16), tri)
    slot_ref[...] = jnp.where(sel, pos, -1.0).astype(jnp.int32)
    ti = lax.broadcasted_iota(jnp.int32, (n, LANES), 0)
    bi = lax.broadcasted_iota(jnp.int32, (n, LANES), 1)
    before_block = jnp.where(ti < bi * GATHER_TOKEN_BLOCK, 1.0, 0.0).astype(BF16)
    first_ref[...] = _dot(jnp.where(sel, 1.0, 0.0).astype(BF16), before_block).astype(jnp.int32)


def _select(logits, cap):
    bsz, ne, n = logits.shape
    blk = pl.BlockSpec((bsz * ne, n), lambda i: (0, 0))
    slot, aff, first = pl.pallas_call(
        functools.partial(_select_kernel, ne=ne, cap=cap),
        grid=(1,),
        in_specs=[blk],
        out_specs=[blk, blk, pl.BlockSpec((bsz * ne, LANES), lambda i: (0, 0))],
        out_shape=[jax.ShapeDtypeStruct((bsz * ne, n), jnp.int32),
                   jax.ShapeDtypeStruct((bsz * ne, n), F32),
                   jax.ShapeDtypeStruct((bsz * ne, LANES), jnp.int32)],
        scratch_shapes=[pltpu.VMEM((n, n), BF16)],
        compiler_params=_params("arbitrary"),
        name="moe_select",
    )(logits.reshape(bsz * ne, n))
    return slot.reshape(bsz, ne, n), aff.reshape(bsz, ne, n), first


GATHER_EXPERTS = 8


GATHER_TOKEN_BLOCK = 256
GATHER_WINDOW = 64


def _gather_dense(h_ref, slot_ref, aff_ref, xe_ref, gate_ref, onehot_ref, cap):
    n = h_ref.shape[1]
    pi = lax.broadcasted_iota(jnp.int32, (cap, n), 0)
    for e in range(GATHER_EXPERTS):
        hit = slot_ref[0, e:e + 1, :] == pi
        onehot_ref[e * cap:(e + 1) * cap, :] = jnp.where(hit, 1.0, 0.0).astype(BF16)
        gate_ref[e] = jnp.sum(jnp.where(hit, aff_ref[0, e:e + 1, :], 0.0), axis=1, keepdims=True)
    xe = _dot(onehot_ref[...], h_ref[0]).astype(BF16)
    xe_ref[...] = xe.reshape(GATHER_EXPERTS, cap, xe.shape[1])


def _gather_kernel(first_ref, h_ref, slot_ref, aff_ref, xe_ref, gate_ref, onehot_ref, *scratch, cap, windowed):
    if not windowed:
        _gather_dense(h_ref, slot_ref, aff_ref, xe_ref, gate_ref, onehot_ref, cap)
        return
    win_ref, acc_ref, gacc_ref = scratch
    n = h_ref.shape[1]
    tb, win, ge = GATHER_TOKEN_BLOCK, GATHER_WINDOW, GATHER_EXPERTS
    nblk = n // tb
    row0 = (pl.program_id(0) * pl.num_programs(1) + pl.program_id(1)) * ge
    starts, fits = {}, None
    for e in range(ge):
        for j in range(nblk):
            lo = first_ref[(row0 + e) * (nblk + 1) + j]
            hi = first_ref[(row0 + e) * (nblk + 1) + j + 1]
            start = pl.multiple_of(jnp.minimum((lo // 16) * 16, cap - win), 16)
            starts[e, j] = start
            fit = hi - start <= win
            fits = fit if fits is None else jnp.logical_and(fits, fit)

    @pl.when(fits)
    def _():
        acc_ref[...] = jnp.zeros_like(acc_ref)
        gacc_ref[...] = jnp.zeros_like(gacc_ref)
        ri = lax.broadcasted_iota(jnp.int32, (win, tb), 0)
        for j in range(nblk):
            cols = slice(j * tb, (j + 1) * tb)
            for e in range(ge):
                hit = slot_ref[0, e:e + 1, cols] == ri + starts[e, j]
                win_ref[e * win:(e + 1) * win, :] = jnp.where(hit, 1.0, 0.0).astype(BF16)
                gacc_ref[e, pl.ds(starts[e, j], win), :] += jnp.sum(
                    jnp.where(hit, aff_ref[0, e:e + 1, cols], 0.0), axis=1, keepdims=True)
            x = _dot(win_ref[...], h_ref[0, cols, :])
            for e in range(ge):
                acc_ref[e, pl.ds(starts[e, j], win), :] += x[e * win:(e + 1) * win]
        xe_ref[...] = acc_ref[...].astype(BF16)
        gate_ref[...] = gacc_ref[...]

    @pl.when(jnp.logical_not(fits))
    def _():
        _gather_dense(h_ref, slot_ref, aff_ref, xe_ref, gate_ref, onehot_ref, cap)


def _gather(h, slot, aff, first, cap):
    bsz, n, d = h.shape
    ne = slot.shape[1]
    ge = GATHER_EXPERTS
    nblk = n // GATHER_TOKEN_BLOCK
    windowed = cap > GATHER_WINDOW and nblk > 1
    scratch = [pltpu.VMEM((ge * cap, n), BF16)]
    if windowed:
        scratch += [pltpu.VMEM((ge * GATHER_WINDOW, GATHER_TOKEN_BLOCK), BF16), pltpu.VMEM((ge, cap, d), F32),
                    pltpu.VMEM((ge, cap, 1), F32)]
    grid_spec = pltpu.PrefetchScalarGridSpec(
        num_scalar_prefetch=1,
        grid=(bsz, ne // ge),
        in_specs=[pl.BlockSpec((1, n, d), lambda b, j, first: (b, 0, 0)),
                  pl.BlockSpec((1, ge, n), lambda b, j, first: (b, j, 0)),
                  pl.BlockSpec((1, ge, n), lambda b, j, first: (b, j, 0))],
        out_specs=[pl.BlockSpec((ge, cap, d), lambda b, j, first: (j, b, 0)),
                   pl.BlockSpec((ge, cap, 1), lambda b, j, first: (j, b, 0))],
        scratch_shapes=scratch)
    return pl.pallas_call(
        functools.partial(_gather_kernel, cap=cap, windowed=windowed),
        grid_spec=grid_spec,
        out_shape=[jax.ShapeDtypeStruct((ne, bsz * cap, d), BF16),
                   jax.ShapeDtypeStruct((ne, bsz * cap, 1), F32)],
        compiler_params=_params("arbitrary", "arbitrary"),
        name="moe_gather",
    )(first[:, :nblk + 1].reshape(-1), h, slot, aff)


def _ffn_kernel(*refs, nseg, rows_per_pass):
    x_refs = refs[:nseg]
    gate_refs = refs[nseg:2 * nseg]
    wg_ref, wu_ref, wd_ref = refs[2 * nseg:2 * nseg + 3]
    y_refs = refs[2 * nseg + 3:3 * nseg + 3]
    acc_refs = refs[3 * nseg + 3:]
    f = pl.program_id(1)
    passes = []
    for x_ref, acc_ref in zip(x_refs, acc_refs):
        m = x_ref.shape[1]
        step = min(rows_per_pass, m)
        passes += [(x_ref, acc_ref, slice(r, r + step)) for r in range(0, m, step)]

    def run(first):
        wg = wg_ref[0].astype(BF16)
        wu = wu_ref[0].astype(BF16)
        wd = wd_ref[0].astype(BF16)
        for x_ref, acc_ref, rows in passes:
            x = x_ref[0, rows, :]
            part = _dot((_silu(_dot(x, wg)) * _dot(x, wu)).astype(BF16), wd)
            if first:
                acc_ref[rows, :] = part
            else:
                acc_ref[rows, :] += part

    @pl.when(f == 0)
    def _():
        run(True)

    @pl.when(f > 0)
    def _():
        run(False)

    @pl.when(f == pl.num_programs(1) - 1)
    def _():
        for y_ref, gate_ref, acc_ref in zip(y_refs, gate_refs, acc_refs):
            y_ref[0] = (acc_ref[...] * gate_ref[0]).astype(BF16)


def _ffn(xes, gates, w_gate, w_up, w_down, layer):
    _, ne, d, ff = w_gate.shape
    tf = 256
    nseg = len(xes)
    return pl.pallas_call(
        functools.partial(_ffn_kernel, nseg=nseg, rows_per_pass=512),
        grid=(ne, ff // tf),
        in_specs=[pl.BlockSpec((1, xe.shape[1], d), lambda e, f: (e, 0, 0)) for xe in xes]
        + [pl.BlockSpec((1, xe.shape[1], 1), lambda e, f: (e, 0, 0)) for xe in xes]
        + [pl.BlockSpec((None, 1, d, tf), lambda e, f: (layer, e, 0, f)),
           pl.BlockSpec((None, 1, d, tf), lambda e, f: (layer, e, 0, f)),
           pl.BlockSpec((None, 1, tf, d), lambda e, f: (layer, e, f, 0))],
        out_specs=[pl.BlockSpec((1, xe.shape[1], d), lambda e, f: (e, 0, 0)) for xe in xes],
        out_shape=[jax.ShapeDtypeStruct(xe.shape, BF16) for xe in xes],
        scratch_shapes=[pltpu.VMEM(xe.shape[1:], F32) for xe in xes],
        compiler_params=_params("arbitrary", "arbitrary"),
        name="moe_ffn",
    )(*xes, *gates, w_gate, w_up, w_down)


def _combine_dense(slot_ref, ye_ref, onehot_ref, cap):
    ne = ye_ref.shape[0]
    if cap % LANES == 0:
        pi = lax.broadcasted_iota(jnp.int32, (1, cap), 1)
        for e in range(ne):
            hit = slot_ref[0, :, e:e + 1] == pi
            onehot_ref[:, e * cap:(e + 1) * cap] = jnp.where(hit, 1.0, 0.0).astype(BF16)
    else:
        ei = lax.broadcasted_iota(jnp.int32, (ne, ne * cap), 0)
        li = lax.broadcasted_iota(jnp.int32, (ne, ne * cap), 1)
        spread = jnp.where((li >= ei * cap) & (li < (ei + 1) * cap), 1.0, 0.0).astype(BF16)
        slots = _dot(slot_ref[0].astype(F32).astype(BF16), spread)
        group = jnp.sum(jnp.where(li >= (ei + 1) * cap, 1, 0), axis=0, keepdims=True)
        pos = (li[0:1] - group * cap).astype(F32)
        onehot_ref[...] = jnp.where(slots == pos, 1.0, 0.0).astype(BF16)
    ye = ye_ref[...]
    return _dot(onehot_ref[...], ye.reshape(ne * cap, ye.shape[2]))


def _combine_kernel(first_ref, slot_ref, ye_ref, x_ref, mod_ref, *rest, cap, final_norm, windowed):
    if final_norm:
        fg_ref, o_ref, onehot_ref = rest[:3]
    else:
        o_ref, onehot_ref = rest[:2]
    ne = ye_ref.shape[0]
    gain = mod_ref[0, 0][5:6]

    def finish(rows, moe):
        y = x_ref[0, rows, :] + gain * moe
        if final_norm:
            y = _rms(y) * fg_ref[...]
        o_ref[0, rows, :] = y

    tm = x_ref.shape[1]
    if not windowed:
        finish(slice(0, tm), _combine_dense(slot_ref, ye_ref, onehot_ref, cap))
        return
    gwin_ref, ywin_ref = rest[-2:]
    tb, win = GATHER_TOKEN_BLOCK, GATHER_WINDOW
    per_step = tm // tb
    nblk = per_step * pl.num_programs(1)
    starts, fits = {}, None
    for e in range(ne):
        for jj in range(per_step):
            at = (pl.program_id(0) * ne + e) * (nblk + 1) + pl.program_id(1) * per_step + jj
            lo, hi = first_ref[at], first_ref[at + 1]
            start = pl.multiple_of(jnp.minimum((lo // 16) * 16, cap - win), 16)
            starts[e, jj] = start
            fit = hi - start <= win
            fits = fit if fits is None else jnp.logical_and(fits, fit)

    @pl.when(fits)
    def _():
        lane = lax.broadcasted_iota(jnp.int32, (1, LANES), 1)
        low = lane < win
        for jj in range(per_step):
            rows = slice(jj * tb, (jj + 1) * tb)
            for e in range(ne):
                ywin_ref[e * win:(e + 1) * win, :] = ye_ref[e, pl.ds(starts[e, jj], win), :]
            for k in range(ne * win // LANES):
                e0, e1 = 2 * k, 2 * k + 1
                want = jnp.where(low, starts[e0, jj] + lane, starts[e1, jj] + lane - win)
                have = jnp.where(low, slot_ref[0, rows, e0:e0 + 1], slot_ref[0, rows, e1:e1 + 1])
                gwin_ref[:, k * LANES:(k + 1) * LANES] = jnp.where(have == want, 1.0, 0.0).astype(BF16)
            finish(rows, _dot(gwin_ref[...], ywin_ref[...]))

    @pl.when(jnp.logical_not(fits))
    def _():
        finish(slice(0, tm), _combine_dense(slot_ref, ye_ref, onehot_ref, cap))


def _combine(slot_t, ye, x, mod, group, first, cap, final_g):
    bsz, n, d = x.shape
    ne = slot_t.shape[2]
    tm = min(n, 1024)
    nblk = n // GATHER_TOKEN_BLOCK
    windowed = cap > GATHER_WINDOW and nblk > 1 and 2 * GATHER_WINDOW == LANES
    final_norm = final_g is not None
    in_specs = [pl.BlockSpec((1, tm, ne), lambda b, i, first: (b, i, 0)),
                pl.BlockSpec((ne, cap, d), lambda b, i, first: (0, b, 0)),
                pl.BlockSpec((1, tm, d), lambda b, i, first: (b, i, 0)),
                pl.BlockSpec((1, 1, 8, d), lambda b, i, first: (b, group, 0, 0))]
    args = [slot_t, ye, x, mod]
    if final_norm:
        in_specs.append(pl.BlockSpec((1, d), lambda b, i, first: (0, 0)))
        args.append(final_g.reshape(1, d))
    scratch = [pltpu.VMEM((tm, ne * cap), BF16)]
    if windowed:
        scratch += [pltpu.VMEM((GATHER_TOKEN_BLOCK, ne * GATHER_WINDOW), BF16),
                    pltpu.VMEM((ne * GATHER_WINDOW, d), BF16)]
    grid_spec = pltpu.PrefetchScalarGridSpec(
        num_scalar_prefetch=1,
        grid=(bsz, n // tm),
        in_specs=in_specs,
        out_specs=pl.BlockSpec((1, tm, d), lambda b, i, first: (b, i, 0)),
        scratch_shapes=scratch)
    return pl.pallas_call(
        functools.partial(_combine_kernel, cap=cap, final_norm=final_norm, windowed=windowed),
        grid_spec=grid_spec,
        out_shape=jax.ShapeDtypeStruct((bsz, n, d), F32),
        compiler_params=_params("arbitrary", "arbitrary"),
        name="moe_combine",
    )(first[:, :nblk + 1].reshape(-1), *args)


def _moe(streams, groups, mod, w_gate, w_up, w_down, layer, final_g=None):
    xs = [s[0] for s in streams]
    routed = []
    for x, h, logits in streams:
        cap = EC_CAPACITY_FACTOR * x.shape[1] // logits.shape[1]
        slot, aff, first = _select(logits, cap)
        xe, gate = _gather(h, slot, aff, first, cap)
        routed.append((cap, slot, xe, gate, first))
    yes = _ffn([r[2] for r in routed], [r[3] for r in routed], w_gate, w_up, w_down, layer)
    return [_combine(jnp.swapaxes(slot, 1, 2), ye, x, mod, group, first, cap, final_g)
            for x, group, ye, (cap, slot, _, _, first) in zip(xs, groups, yes, routed)]


def _chunk_masks(tm, direction):
    ri = lax.broadcasted_iota(jnp.int32, (tm, tm), 0)
    ci = lax.broadcasted_iota(jnp.int32, (tm, tm), 1)
    shift = CHUNK.bit_length() - 1
    same = jnp.right_shift(ri, shift) == jnp.right_shift(ci, shift)
    return same & ((ci <= ri) if direction == 0 else (ci >= ri))


def _chunk_logdecay(la, direction):
    cum = jnp.where(_chunk_masks(la.shape[0], direction), 1.0, 0.0).astype(BF16)
    hi, lo = _split2(la * LOG2E)
    return _dot(cum, hi) + _dot(cum, lo)


def _scan_operands(q, k, b, direction, refs):
    q_in_ref, q_mid_ref, k_mid_ref, k_out_ref, decay_ref = refs
    tm, n = q.shape
    nc = tm // CHUNK
    b = b.reshape(nc, CHUNK, n)
    last, mid = (CHUNK - 1, CHUNK // 2) if direction == 0 else (0, CHUNK // 2 - 1)
    b_tot = b[:, last:last + 1]
    b_mid = b[:, mid:mid + 1]
    q = q.reshape(nc, CHUNK, n)
    k = k.reshape(nc, CHUNK, n)
    q_in_ref[direction, 0] = (q * jnp.exp2(b)).astype(BF16).reshape(tm, n)
    q_mid_ref[direction, 0] = (q * jnp.exp2(b - b_mid)).astype(BF16).reshape(tm, n)
    k_mid_ref[direction, 0] = (k * jnp.exp2(b_mid - b)).astype(BF16).reshape(tm, n)
    k_out_ref[direction, 0] = (k * jnp.exp2(b_tot - b)).astype(BF16).reshape(tm, n)
    decay = jnp.exp2(b_tot).reshape(nc, n)
    decay_ref[direction, 0, 0] = jnp.concatenate([decay] * (8 // nc), axis=0)


def _rec_proj_kernel(xc_ref, xl_ref, mod_ref, g_ref, w_ref, gkup_ref, gkb_ref, lbl_ref, *out_refs, layer):
    gla_refs, gv_ref, gg_ref = out_refs[0:5], out_refs[5], out_refs[6]
    hg_refs, hv_ref, hg_ref = out_refs[7:12], out_refs[12], out_refs[13]
    mod = mod_ref[0, 0]
    h = _modulate(_stream_block(xc_ref, xl_ref), g_ref[...], mod[0:1], mod[1:2]).astype(BF16)
    n = GLA_HEADS * LANES
    c0 = 2 * n + LANES
    c1 = c0 + 3 * n
    head = _dot(h, w_ref[:, 0:c0])
    gate = _dot(head[:, 2 * n:c0].astype(BF16), gkup_ref[...]) + gkb_ref[...]
    qk = _dot(h, w_ref[:, c0:c1])
    lbl = lbl_ref[...]
    e = jnp.exp(lbl - jnp.max(lbl, axis=0, keepdims=True))
    sm = e / jnp.sum(e, axis=0, keepdims=True)
    lb = jnp.sum(sm[0:layer + 1], axis=0, keepdims=True) - sm[0:1]
    f = lb + (1.0 - lb) * jax.nn.sigmoid(head[:, 0:2 * n])
    fs = [f[:, dd * n:(dd + 1) * n] for dd in range(2)]
    rest = _dot(h, w_ref[:, c1:])
    b_hg = [_chunk_logdecay(jnp.log(fs[dd]), dd) for dd in range(2)]
    b_gla = [_chunk_logdecay(jax.nn.log_sigmoid(gate[:, dd * n:(dd + 1) * n]) / GLA_GATE_NORM, dd)
             for dd in range(2)]
    hq = _silu(qk[:, 2 * n:3 * n])
    gq = qk[:, 0:n] * GLA_DK ** -0.5
    gk = qk[:, n:2 * n]
    for dd in range(2):
        _scan_operands(hq, 1.0 - fs[dd], b_hg[dd], dd, hg_refs)
    for dd in range(2):
        _scan_operands(gq, gk, b_gla[dd], dd, gla_refs)
    gv_ref[0] = rest[:, 0:n].astype(BF16)
    gg_ref[0] = rest[:, n:2 * n].astype(BF16)
    hv_ref[0] = rest[:, 2 * n:3 * n].astype(BF16)
    hg_ref[0] = rest[:, 3 * n:4 * n].astype(BF16)


def _pad_heads(w, heads, dk):
    lead = w.shape[:-1]
    w3 = w.reshape(lead + (heads, dk))
    return jnp.pad(w3, [(0, 0)] * len(lead) + [(0, 0), (0, LANES - dk)]).reshape(lead + (heads * LANES,))


def _rec_project(xc, xl, mod, g, w_in, gk_up, gk_bias, lb_logits, layer):
    bsz, n_ctx, d = xc.shape
    t = n_ctx + xl.shape[1]
    tm = ROW_TILE
    hk = GLA_HEADS * GLA_DK
    hv = GLA_HEADS * GLA_DV
    hf = HG_HEADS * HG_DF
    sizes = (hk, hk, hv, hv, 2 * GLA_GATE_RANK, hf, 2 * hf, HG_HEADS * HG_DV, HG_HEADS * HG_DV)
    offs = [0]
    for s in sizes:
        offs.append(offs[-1] + s)
    gq, gk, gv, gg, gdown, hq, hff, hi, hgt = [w_in[:, offs[i]:offs[i + 1]] for i in range(9)]
    w = jnp.concatenate([hff, jnp.pad(gdown, ((0, 0), (0, LANES - 2 * GLA_GATE_RANK))),
                         _pad_heads(gq, GLA_HEADS, GLA_DK), _pad_heads(gk, GLA_HEADS, GLA_DK), hq,
                         gv, gg, hi, hgt], axis=1).astype(BF16)
    n = GLA_HEADS * LANES
    ups = [_pad_heads(gk_up[dd], GLA_HEADS, GLA_DK) for dd in range(2)]
    zero = jnp.zeros_like(ups[0])
    gkup = jnp.concatenate([jnp.concatenate([ups[0], zero], axis=1), jnp.concatenate([zero, ups[1]], axis=1),
                            jnp.zeros((LANES - 2 * GLA_GATE_RANK, 2 * n), F32)], axis=0).astype(BF16)
    gkb = jnp.concatenate([_pad_heads(gk_bias[dd], GLA_HEADS, GLA_DK) for dd in range(2)]).reshape(1, 2 * n)
    row = lambda b, i: (b, i, 0)
    drow = lambda b, i: (0, b, i, 0)
    const = lambda b, i: (0, 0)
    one = lambda dt: (pl.BlockSpec((1, tm, n), row), jax.ShapeDtypeStruct((bsz, t, n), dt))
    two = (pl.BlockSpec((2, 1, tm, n), drow), jax.ShapeDtypeStruct((2, bsz, t, n), BF16))
    dec = (pl.BlockSpec((2, 1, 1, 8, n), lambda b, i: (0, b, i, 0, 0)),
           jax.ShapeDtypeStruct((2, bsz, t // tm, 8, n), F32))
    mixer = [two, two, two, two, dec, one(BF16), one(BF16)]
    outs = mixer + mixer
    return pl.pallas_call(
        functools.partial(_rec_proj_kernel, layer=layer),
        grid=(bsz, t // tm),
        in_specs=_stream_specs(d) + [
            _mod_spec(d),
            pl.BlockSpec((1, d), const),
            pl.BlockSpec(w.shape, const),
            pl.BlockSpec(gkup.shape, const),
            pl.BlockSpec(gkb.shape, const),
            pl.BlockSpec((lb_logits.shape[0], 2 * hf), const)],
        out_specs=[o[0] for o in outs],
        out_shape=[o[1] for o in outs],
        compiler_params=_params("arbitrary", "arbitrary"),
        name="rec_project",
    )(xc, xl, mod, g.reshape(1, d), w, gkup, gkb, lb_logits.reshape(lb_logits.shape[0], 2 * hf))


def _scan_kernel(*refs, heads, directions):
    ns = len(directions)
    ins = [refs[6 * i:6 * i + 6] for i in range(ns)]
    o_refs = refs[6 * ns:7 * ns]
    st_ref = refs[7 * ns]
    tm = o_refs[0].shape[1]
    nc = tm // CHUNK

    @pl.when(pl.program_id(1) == 0)
    def _():
        st_ref[...] = jnp.zeros_like(st_ref)

    lanes = [slice(LANES * h, LANES * (h + 1)) for h in range(heads)]
    chunk_rows = [slice(c * CHUNK, (c + 1) * CHUNK) for c in range(nc)]
    units = [(i, h) for i in range(ns) for h in range(heads)]
    vs = {(i, h): ins[i][5][0, :, lanes[h]] for i, h in units}
    scores = {(i, h): _dot_nt(ins[i][1][0, 0, :, lanes[h]], ins[i][2][0, 0, :, lanes[h]]) for i, h in units}
    updates = {(i, h): [_dot_tn(vs[i, h][rows], ins[i][3][0, 0, rows, lanes[h]]) for rows in chunk_rows]
               for i, h in units}
    causal = [_chunk_masks(tm, d) for d in range(2)]
    intra = {(i, h): _dot(jnp.where(causal[directions[i]], scores[i, h], 0.0).astype(BF16), vs[i, h])
             for i, h in units}
    order = [list(range(nc)), list(range(nc - 1, -1, -1))]
    entering = {}
    for i, h in units:
        st = st_ref[i, h]
        for c in order[directions[i]]:
            entering[i, h, c] = st.astype(BF16)
            st = st * ins[i][4][0, 0, 0, c:c + 1, lanes[h]] + updates[i, h][c]
        st_ref[i, h] = st
    for i, h in units:
        for c in order[directions[i]]:
            rows = chunk_rows[c]
            inter = _dot_nt(ins[i][0][0, 0, rows, lanes[h]], entering[i, h, c])
            o_refs[i][0, rows, lanes[h]] = (intra[i, h][rows] + inter).astype(BF16)


def _scan(mixers, n_ctx):
    _, bsz, t, n = mixers[0][0].shape
    heads = n // LANES
    tm = ROW_TILE
    nb = t // tm
    ncb = n_ctx // tm

    def blk(d, s):
        return s if d == 0 else jnp.where(s < ncb, ncb - 1 - s, nb - 1 - (s - ncb))

    in_specs, args, directions = [], [], []
    for q_in, q_mid, k_mid, k_out, decay, v in mixers:
        for d in range(2):
            drow = lambda b, s, d=d: (d, b, blk(d, s), 0)
            in_specs += [pl.BlockSpec((1, 1, tm, n), drow)] * 4
            in_specs += [pl.BlockSpec((1, 1, 1, 8, n), lambda b, s, d=d: (d, b, blk(d, s), 0, 0)),
                         pl.BlockSpec((1, tm, n), lambda b, s, d=d: (b, blk(d, s), 0))]
            args += [q_in, q_mid, k_mid, k_out, decay, v]
            directions.append(d)
    outs = pl.pallas_call(
        functools.partial(_scan_kernel, heads=heads, directions=tuple(directions)),
        grid=(bsz, nb),
        in_specs=in_specs,
        out_specs=[pl.BlockSpec((1, tm, n), lambda b, s, d=d: (b, blk(d, s), 0)) for d in directions],
        out_shape=[jax.ShapeDtypeStruct((bsz, t, n), BF16)] * len(directions),
        scratch_shapes=[pltpu.VMEM((len(directions), heads, LANES, LANES), F32)],
        compiler_params=_params("arbitrary", "arbitrary"),
        name="rec_scan",
    )(*args)
    return [tuple(outs[2 * i:2 * i + 2]) for i in range(len(mixers))]


def _rec_out_kernel(og0_ref, og1_ref, oh0_ref, oh1_ref, gg_ref, hg_ref, gn_ref, hn_ref, wout_ref,
                    x_ref, mod_ref, g2_ref, wrt_ref, o_ref, h_ref, lg_ref):
    parts = []
    for fwd_ref, bwd_ref, gate_ref, n_ref in ((og0_ref, og1_ref, gg_ref, gn_ref),
                                              (oh0_ref, oh1_ref, hg_ref, hn_ref)):
        for h in range(fwd_ref.shape[2] // LANES):
            sl = slice(LANES * h, LANES * (h + 1))
            o = fwd_ref[0, :, sl].astype(F32) + bwd_ref[0, :, sl].astype(F32)
            parts.append(_rms(o) * n_ref[...] * _silu(gate_ref[0, :, sl].astype(F32)))
    a = jnp.concatenate(parts, axis=1).astype(BF16)
    mod = mod_ref[0, 0]
    y = x_ref[0] + mod[2:3] * _dot(a, wout_ref[...])
    o_ref[0] = y
    _route_rows(y, mod, g2_ref, wrt_ref, h_ref, lg_ref)


def _rec_readout(og, oh, gg, hg, gn, hn, w_out, xl, mod, g2, w_router, n_ctx):
    bsz, seq, d = xl.shape
    tm = ROW_TILE
    ncb = n_ctx // tm
    n = gg.shape[2]
    row = lambda b, i: (b, i + ncb, 0)
    const = lambda b, i: (0, 0)
    wout = w_out.astype(BF16)
    r_in, r_out, r_shape = _route_specs(bsz, seq, d, w_router.shape[1], tm)
    return pl.pallas_call(
        _rec_out_kernel,
        grid=(bsz, seq // tm),
        in_specs=[pl.BlockSpec((1, tm, n), row),
                  pl.BlockSpec((1, tm, n), row),
                  pl.BlockSpec((1, tm, n), row),
                  pl.BlockSpec((1, tm, n), row),
                  pl.BlockSpec((1, tm, n), row),
                  pl.BlockSpec((1, tm, n), row),
                  pl.BlockSpec((1, LANES), const),
                  pl.BlockSpec((1, LANES), const),
                  pl.BlockSpec(wout.shape, const),
                  pl.BlockSpec((1, tm, d), lambda b, i: (b, i, 0)),
                  _mod_spec(d, 1)] + r_in,
        out_specs=[pl.BlockSpec((1, tm, d), lambda b, i: (b, i, 0))] + r_out,
        out_shape=[jax.ShapeDtypeStruct((bsz, seq, d), F32)] + r_shape,
        compiler_params=_params("arbitrary", "arbitrary"),
        name="rec_readout",
    )(*og, *oh, gg, hg, gn.reshape(1, -1), hn.reshape(1, -1), wout, xl, mod, g2.reshape(1, d), w_router.T)


def kernel(x, c, ctx, c_ctx, ada_w, ada_b, norm1_g, norm2_g, att_w_in, mla_q_norm_g, mla_q_up, mla_kv_norm_g, mla_kv_up, da_lam_q1, da_lam_k1, da_lam_q2, da_lam_k2, da_subln_g, att_w_out, rec_w_in, gla_gk_up, gla_gk_bias, gla_norm_g, hg_lb_logits, hg_norm_g, rec_w_out, moe_router, moe_w_gate, moe_w_up, moe_w_down, final_norm_g):
    bsz, seq, d = x.shape
    n_ctx = ctx.shape[1]
    depth = ada_w.shape[0]
    assert depth == 2 and n_ctx == ROW_TILE and seq % ROW_TILE == 0 and bsz + 1 <= 16

    cvec = jnp.concatenate([c, c_ctx[None, :], jnp.zeros((16 - bsz - 1, d), F32)], axis=0)
    ada = _ada_vectors(cvec, ada_w, ada_b).reshape(depth, 16, 6, d)
    ada = jnp.pad(ada, ((0, 0), (0, 0), (0, 2), (0, 0)))
    mods = jnp.stack([jnp.broadcast_to(ada[:, bsz][:, None], (depth, bsz, 8, d)), ada[:, :bsz]], axis=2)
    xc, xl = ctx, x

    mod = mods[0]
    tabs_da = _rope_tables(n_ctx, seq, DA_QK, 0, DA_QK)
    tabs_m = _rope_tables(n_ctx, seq, MLA_ROPE, MLA_NOPE, LANES)
    proj = _attn_project(xc, xl, mod, norm1_g[0], att_w_in[0], mla_q_norm_g[0], mla_q_up[0],
                         mla_kv_norm_g[0], mla_kv_up[0], tabs_da, tabs_m)
    lamv = jnp.stack([da_lam_q1[0], da_lam_k1[0], da_lam_q2[0], da_lam_k2[0]])
    lam_init = 0.8 - 0.6 * math.exp(-0.3 * 0)
    sc = _attention(*proj, lamv, da_subln_g[0], att_w_out[0], xc, mod, norm2_g[0], moe_router[0], 0, 0, n_ctx,
                    lam_init)
    sl = _attention(*proj, lamv, da_subln_g[0], att_w_out[0], xl, mod, norm2_g[0], moe_router[0], 1,
                    n_ctx // ROW_TILE, n_ctx + seq, lam_init)
    xc, xl = _moe([sc, sl], [0, 1], mod, moe_w_gate, moe_w_up, moe_w_down, 0)

    mod = mods[1]
    rec = _rec_project(xc, xl, mod, norm1_g[1], rec_w_in[0], gla_gk_up[0], gla_gk_bias[0], hg_lb_logits, 1)
    og, oh = _scan([rec[0:6], rec[7:13]], n_ctx)
    sl = _rec_readout(og, oh, rec[6], rec[13], gla_norm_g[0], hg_norm_g[0], rec_w_out[0], xl, mod, norm2_g[1],
                      moe_router[1], n_ctx)
    (out,) = _moe([sl], [1], mod, moe_w_gate, moe_w_up, moe_w_down, 1, final_g=final_norm_g)
    return out
```

```python
import functools
import math

import jax
import jax.numpy as jnp
from jax import lax
from jax.experimental import pallas as pl
from jax.experimental.pallas import tpu as pltpu

F32 = jnp.float32
BF16 = jnp.bfloat16

EPS = 1e-6
GRID_W = 64
ROPE_BASE = 10000.0

DA_HEADS = 4
DA_QK = 64
DA_V = 2 * DA_QK
DA_SCALE = DA_QK ** -0.5
MLA_HEADS = 8
MLA_NOPE = 64
MLA_ROPE = 32
MLA_V = 64
MLA_Q_RANK = 384
MLA_KV_RANK = 256
MLA_SCALE = (MLA_NOPE + MLA_ROPE) ** -0.5
LOG2E = math.log2(math.e)

GLA_HEADS = 4
GLA_DK = 64
GLA_DV = 128
GLA_GATE_RANK = 16
GLA_GATE_NORM = 16.0
HG_HEADS = 4
HG_DF = 128
HG_DV = 128
CHUNK = 64

N_EXPERTS = 16
EC_CAPACITY_FACTOR = 2

LANES = 128
ROW_TILE = 256
ATTN_KEY_BLOCK = 256
SUM_ROWS = 16
ATTN_ROUTER_UNIT = 12
ATTN_LOOKAHEAD = 12
VMEM_LIMIT = 56 * 1024 * 1024


def _params(*sem):
    return pltpu.CompilerParams(dimension_semantics=sem, vmem_limit_bytes=VMEM_LIMIT)


def _dot(a, b):
    return jnp.dot(a, b, preferred_element_type=F32)


def _dot_nt(a, b):
    return lax.dot_general(a, b, (((1,), (1,)), ((), ())), preferred_element_type=F32)


def _dot_tn(a, b):
    return lax.dot_general(a, b, (((0,), (0,)), ((), ())), preferred_element_type=F32)


def _split2(a):
    hi = a.astype(BF16)
    lo = (a - hi.astype(F32)).astype(BF16)
    return hi, lo


def _dot3(a, b, dot=_dot):
    ah, al = _split2(a)
    bh, bl = _split2(b)
    return dot(ah, bh) + (dot(ah, bl) + dot(al, bh))


def _rms(x):
    return x * lax.rsqrt(jnp.mean(x * x, axis=-1, keepdims=True) + EPS)


def _silu(x):
    return x * jax.nn.sigmoid(x)


def _modulate(x, g, shift, scale):
    return _rms(x) * g * (1.0 + scale) + shift


def _stream_specs(d):
    return [pl.BlockSpec((1, ROW_TILE, d), lambda b, i: (b, 0, 0)),
            pl.BlockSpec((1, ROW_TILE, d), lambda b, i: (b, jnp.maximum(i - 1, 0), 0))]


def _stream_block(xc_ref, xl_ref):
    return jnp.where(pl.program_id(1) == 0, xc_ref[0], xl_ref[0])


def _route_rows(y, mod, g2_ref, wrt_ref, h_ref, lg_ref):
    _route_logits(_route_split(y, mod, g2_ref, h_ref), wrt_ref, lg_ref)


def _route_split(y, mod, g2_ref, h_ref):
    hh, hl = _split2(_modulate(y, g2_ref[...], mod[3:4], mod[4:5]))
    h_ref[0] = hh
    return hh, hl


def _route_logits(split, wrt_ref, lg_ref):
    hh, hl = split
    wh, wl = _split2(wrt_ref[...])
    ne = wh.shape[0]
    both = _dot_nt(jnp.concatenate([wh, wl], axis=0), hh)
    lg_ref[0] = both[0:ne] + (_dot_nt(wh, hl) + both[ne:2 * ne])


def _route_specs(bsz, n, d, ne, tm):
    const = lambda b, i: (0, 0)
    return ([pl.BlockSpec((1, d), const), pl.BlockSpec((ne, d), const)],
            [pl.BlockSpec((1, tm, d), lambda b, i: (b, i, 0)), pl.BlockSpec((1, ne, tm), lambda b, i: (b, 0, i))],
            [jax.ShapeDtypeStruct((bsz, n, d), BF16), jax.ShapeDtypeStruct((bsz, ne, n), F32)])


def _mod_spec(d, group=None):
    if group is None:
        return pl.BlockSpec((1, 1, 8, d), lambda b, i: (b, jnp.minimum(i, 1), 0, 0))
    return pl.BlockSpec((1, 1, 8, d), lambda b, i: (b, group, 0, 0))


def _ada_kernel(c_ref, w_ref, b_ref, o_ref):
    o_ref[...] = _dot3(_silu(c_ref[...]), w_ref[...]) + b_ref[...]


def _ada_vectors(cvec, ada_w, ada_b):
    depth, d, n6 = ada_w.shape
    tn = n6 // 4
    rows = cvec.shape[0]
    return pl.pallas_call(
        _ada_kernel,
        grid=(depth, n6 // tn),
        in_specs=[pl.BlockSpec((rows, d), lambda l, j: (0, 0)),
                  pl.BlockSpec((None, d, tn), lambda l, j: (l, 0, j)),
                  pl.BlockSpec((None, 1, tn), lambda l, j: (l, 0, j))],
        out_specs=pl.BlockSpec((None, rows, tn), lambda l, j: (l, 0, j)),
        out_shape=jax.ShapeDtypeStruct((depth, rows, n6), F32),
        compiler_params=_params("arbitrary", "arbitrary"),
        name="ada_vectors",
    )(cvec, ada_w, ada_b.reshape(depth, 1, n6))


def _axial_angles(rows, rot_dim):
    row = jnp.repeat(jnp.arange(rows, dtype=F32), GRID_W)
    col = jnp.tile(jnp.arange(GRID_W, dtype=F32), rows)
    n_freq = rot_dim // 4
    inv = ROPE_BASE ** (-jnp.arange(n_freq, dtype=F32) / n_freq)
    ang = jnp.concatenate([row[:, None] * inv, col[:, None] * inv], axis=-1)
    return jnp.cos(ang), jnp.sin(ang)


def _rope_tables(n_ctx, seq, width, start, group):
    half = width // 2
    cos, sin = _axial_angles(seq // GRID_W, width)
    lane = jnp.arange(LANES) % group - start
    inside = (lane >= 0) & (lane < width)
    idx = jnp.clip(lane, 0, width - 1) % half
    second = inside & (lane >= half)
    firsth = inside & (lane < half)
    c = jnp.where(inside[None, :], cos[:, idx], 1.0)
    s_prev = jnp.where(second[None, :], sin[:, idx], 0.0)
    s_next = jnp.where(firsth[None, :], -sin[:, idx], 0.0)
    ident = jnp.ones((n_ctx, LANES), F32)
    zero = jnp.zeros((n_ctx, LANES), F32)
    return (jnp.concatenate([ident, c], 0), jnp.concatenate([zero, s_prev], 0),
            jnp.concatenate([zero, s_next], 0))


def _rope(x, cos, s_prev, s_next, half):
    return x * cos + pltpu.roll(x, half, 1) * s_prev + pltpu.roll(x, LANES - half, 1) * s_next


def _attn_proj_kernel(xc_ref, xl_ref, mod_ref, g_ref, w_ref, wvt_ref, vab_ref, qn_ref, qup_ref, kvn_ref,
                      kvup_ref, kvupt_ref, vmb_ref, cda_ref, pda_ref, nda_ref, cm_ref, pm_ref, nm_ref,
                      qda_ref, kda_ref, vat_ref, qm_ref, km_ref, vmt_ref):
    mod = mod_ref[0, 0]
    h = _modulate(_stream_block(xc_ref, xl_ref), g_ref[...], mod[0:1], mod[1:2]).astype(BF16)
    proj = _dot(h, w_ref[...])
    vat_ref[0] = (_dot_nt(wvt_ref[...], h) + vab_ref[...]).astype(BF16)
    cda, pda, nda = cda_ref[...], pda_ref[...], nda_ref[...]
    cm, pm, nm = cm_ref[...], pm_ref[...], nm_ref[...]
    n_da = DA_HEADS * LANES
    for hh in range(DA_HEADS):
        sl = slice(LANES * hh, LANES * (hh + 1))
        q = proj[:, LANES * hh:LANES * (hh + 1)]
        k = proj[:, n_da + LANES * hh:n_da + LANES * (hh + 1)]
        qda_ref[0, :, sl] = (_rope(q, cda, pda, nda, DA_QK // 2) * (DA_SCALE * LOG2E)).astype(BF16)
        kda_ref[0, :, sl] = _rope(k, cda, pda, nda, DA_QK // 2).astype(BF16)
    o = 2 * n_da
    cq = proj[:, o:o + MLA_Q_RANK]
    ckv = proj[:, o + MLA_Q_RANK:o + MLA_Q_RANK + MLA_KV_RANK]
    kr = proj[:, o + MLA_Q_RANK + MLA_KV_RANK:o + MLA_Q_RANK + MLA_KV_RANK + LANES]
    qm = _dot((_rms(cq) * qn_ref[...]).astype(BF16), qup_ref[...])
    ckvn = (_rms(ckv) * kvn_ref[...]).astype(BF16)
    kv = _dot(ckvn, kvup_ref[...])
    vmt_ref[0] = (_dot_nt(kvupt_ref[...], ckvn) + vmb_ref[...]).astype(BF16)
    rep = kr + pltpu.roll(kr, 32, 1) + pltpu.roll(kr, 64, 1) + pltpu.roll(kr, 96, 1)
    lane = lax.broadcasted_iota(jnp.int32, (1, LANES), 1)
    rep = jnp.where((lane >= MLA_NOPE) & (lane < MLA_NOPE + MLA_ROPE), rep, 0.0)
    rep = _rope(rep, cm, pm, nm, MLA_ROPE // 2)
    for hh in range(MLA_HEADS):
        sl = slice(LANES * hh, LANES * (hh + 1))
        qm_ref[0, :, sl] = (_rope(qm[:, sl], cm, pm, nm, MLA_ROPE // 2) * (MLA_SCALE * LOG2E)).astype(BF16)
        km_ref[0, :, sl] = (kv[:, sl] + rep).astype(BF16)


def _with_sum_rows(wt, heads):
    dv = wt.shape[0] // heads
    w3 = jnp.pad(wt.reshape(heads, dv, -1), ((0, 0), (0, SUM_ROWS), (0, 0)))
    bias = jnp.pad(jnp.zeros((heads, dv, 1), F32), ((0, 0), (0, SUM_ROWS), (0, 0)), constant_values=1.0)
    return w3.reshape(heads * (dv + SUM_ROWS), -1).astype(BF16), bias.reshape(heads * (dv + SUM_ROWS), 1)


def _attn_project(xc, xl, mod, g, w_in, qn_g, q_up, kvn_g, kv_up, tabs_da, tabs_m):
    bsz, n_ctx, d = xc.shape
    t = n_ctx + xl.shape[1]
    tm = ROW_TILE
    n_da = DA_HEADS * LANES
    nm = MLA_HEADS * LANES
    nv = MLA_HEADS * MLA_V
    w = jnp.concatenate([w_in[:, :2 * n_da], w_in[:, 3 * n_da:]], axis=1)
    w = jnp.pad(w, ((0, 0), (0, (-w.shape[1]) % LANES))).astype(BF16)
    wvt, vab = _with_sum_rows(w_in[:, 2 * n_da:3 * n_da].T, DA_HEADS)
    qup = jnp.pad(q_up.reshape(MLA_Q_RANK, MLA_HEADS, MLA_NOPE + MLA_ROPE),
                  ((0, 0), (0, 0), (0, LANES - MLA_NOPE - MLA_ROPE))).reshape(MLA_Q_RANK, -1).astype(BF16)
    kv3 = kv_up.reshape(MLA_KV_RANK, MLA_HEADS, MLA_NOPE + MLA_V)
    kvup = jnp.pad(kv3[:, :, :MLA_NOPE],
                   ((0, 0), (0, 0), (0, LANES - MLA_NOPE))).reshape(MLA_KV_RANK, -1).astype(BF16)
    kvupt, vmb = _with_sum_rows(kv3[:, :, MLA_NOPE:].reshape(MLA_KV_RANK, -1).T, MLA_HEADS)
    row = lambda b, i: (b, i, 0)
    col = lambda b, i: (b, 0, i)
    const = lambda b, i: (0, 0)
    tab = pl.BlockSpec((tm, LANES), lambda b, i: (i, 0))
    rows_out = lambda n: (pl.BlockSpec((1, tm, n), row), jax.ShapeDtypeStruct((bsz, t, n), BF16))
    cols_out = lambda n: (pl.BlockSpec((1, n, tm), col), jax.ShapeDtypeStruct((bsz, n, t), BF16))
    outs = [rows_out(n_da), rows_out(n_da), cols_out(wvt.shape[0]), rows_out(nm), rows_out(nm),
            cols_out(kvupt.shape[0])]
    return pl.pallas_call(
        _attn_proj_kernel,
        grid=(bsz, t // tm),
        in_specs=_stream_specs(d) + [
            _mod_spec(d),
            pl.BlockSpec((1, d), const),
            pl.BlockSpec(w.shape, const),
            pl.BlockSpec(wvt.shape, const),
            pl.BlockSpec(vab.shape, const),
            pl.BlockSpec((1, MLA_Q_RANK), const),
            pl.BlockSpec(qup.shape, const),
            pl.BlockSpec((1, MLA_KV_RANK), const),
            pl.BlockSpec(kvup.shape, const),
            pl.BlockSpec(kvupt.shape, const),
            pl.BlockSpec(vmb.shape, const),
            tab, tab, tab, tab, tab, tab],
        out_specs=[o[0] for o in outs],
        out_shape=[o[1] for o in outs],
        compiler_params=_params("arbitrary", "arbitrary"),
        name="attn_project",
    )(xc, xl, mod, g.reshape(1, d), w, wvt, vab, qn_g.reshape(1, -1), qup, kvn_g.reshape(1, -1), kvup, kvupt,
      vmb, *tabs_da, *tabs_m)


def _reduce_rows(x, fn):
    n, w = x.shape
    slab = 256
    if n > slab and n % slab == 0:
        x = fn(x.reshape(n // slab, slab, w), axis=0)
        n = slab
    if n > 32 and n % 32 == 0:
        x = fn(x.reshape(n // 32, 32, w), axis=0)
    return fn(x, axis=0, keepdims=True)


def _softmax_parts(s):
    e = jnp.exp2(s - _reduce_rows(s, jnp.max))
    return e.astype(BF16), 1.0 / _reduce_rows(e, jnp.sum)


def _attn_kernel(qda_ref, qm_ref, kda_ref, vat_ref, km_ref, vmt_ref, lam_ref, subg_ref, wout_ref,
                 x_ref, mod_ref, g2_ref, wrt_ref, o_ref, h_ref, lg_ref, att_ref, s_ref, *, lam_init):
    @pl.when(pl.program_id(0) == 0)
    def _():
        att_ref[...] = jnp.zeros_like(att_ref)

    mod = mod_ref[0, 0]
    y = x_ref[0] + mod[2:3] * _dot_tn(att_ref[...], wout_ref[...])
    o_ref[0] = y
    route = _route_split(y, mod, g2_ref, h_ref)

    lamv = lam_ref[...]
    lam = (jnp.exp(jnp.sum(lamv[0:1] * lamv[1:2], axis=1, keepdims=True))
           - jnp.exp(jnp.sum(lamv[2:3] * lamv[3:4], axis=1, keepdims=True)) + lam_init)
    first = lax.broadcasted_iota(jnp.int32, (1, LANES), 1) < LANES // 2

    n_maps = 2 * DA_HEADS + MLA_HEADS
    maps = []
    for i in range(2 * DA_HEADS):
        sl = slice(LANES * (i // 2), LANES * (i // 2 + 1))
        q = qda_ref[0, :, sl].astype(F32)
        q = jnp.where(first, q, 0.0) if i % 2 == 0 else jnp.where(first, 0.0, q)
        rows = DA_V + SUM_ROWS
        maps.append((q.astype(BF16), kda_ref, sl, vat_ref, slice(rows * (i // 2), rows * (i // 2 + 1)), DA_V))
    for i in range(MLA_HEADS):
        sl = slice(LANES * i, LANES * (i + 1))
        rows = MLA_V + SUM_ROWS
        maps.append((qm_ref[0, :, sl], km_ref, sl, vmt_ref, slice(rows * i, rows * (i + 1)), MLA_V))

    kb = min(ATTN_KEY_BLOCK, kda_ref.shape[1])
    n_blocks = kda_ref.shape[1] // kb

    units = [(i, j) for i in range(n_maps) for j in range(n_blocks)]

    def scores(u):
        i, j = units[u]
        q, k_ref, ksl = maps[i][:3]
        return _dot_nt(k_ref[0, j * kb:(j + 1) * kb, ksl], q)

    for u in range(min(ATTN_LOOKAHEAD, len(units))):
        s_ref[u % ATTN_LOOKAHEAD, 0:kb, :] = scores(u)
    heads = []
    for u, (i, j) in enumerate(units):
        s = s_ref[u % ATTN_LOOKAHEAD, 0:kb, :]
        if u + ATTN_LOOKAHEAD < len(units):
            s_ref[u % ATTN_LOOKAHEAD, 0:kb, :] = scores(u + ATTN_LOOKAHEAD)
        if u == min(ATTN_ROUTER_UNIT, len(units) - 1):
            _route_logits(route, wrt_ref, lg_ref)
        v_ref, vsl, dv = maps[i][3:]
        vt = v_ref[0, vsl, j * kb:(j + 1) * kb]
        mb = _reduce_rows(s, jnp.max)
        if j == 0:
            m = mb
            acc = _dot(vt, jnp.exp2((s - m).astype(BF16)))
        else:
            mn = jnp.maximum(m, mb)
            acc = acc * jnp.exp2(m - mn) + _dot(vt, jnp.exp2((s - mn).astype(BF16)))
            m = mn
        if j == n_blocks - 1:
            heads.append(acc[0:dv] * (1.0 / acc[dv:dv + 1]))
    outs = []
    for h in range(DA_HEADS):
        oa = heads[2 * h] - heads[2 * h + 1] * lam
        oa = oa * lax.rsqrt(jnp.mean(oa * oa, axis=0, keepdims=True) + EPS)
        outs.append(oa * subg_ref[...] * (1.0 - lam_init))
    outs.extend(heads[2 * DA_HEADS:])
    att_ref[...] = jnp.concatenate(outs, axis=0).astype(BF16)


def _attention(qda, kda, vat, qm, km, vmt, lamv, subln_g, w_out, x, mod, g2, w_router, group, q_block0,
               n_keys, lam_init):
    bsz, nq, d = x.shape
    tq = ROW_TILE
    per = nq // tq
    n_tiles = bsz * per
    ne = w_router.shape[1]
    cur = lambda s: jnp.minimum(s, n_tiles - 1)
    prev = lambda s: jnp.maximum(s - 1, 0)
    qrow = lambda s: (cur(s) // per, cur(s) % per + q_block0, 0)
    full = lambda s: (cur(s) // per, 0, 0)
    row = lambda s: (prev(s) // per, prev(s) % per, 0)
    const = lambda s: (0, 0)
    wout = w_out.astype(BF16)
    return pl.pallas_call(
        functools.partial(_attn_kernel, lam_init=lam_init),
        grid=(n_tiles + 1,),
        in_specs=[pl.BlockSpec((1, tq, qda.shape[2]), qrow),
                  pl.BlockSpec((1, tq, qm.shape[2]), qrow),
                  pl.BlockSpec((1, n_keys, kda.shape[2]), full),
                  pl.BlockSpec((1, vat.shape[1], n_keys), full),
                  pl.BlockSpec((1, n_keys, km.shape[2]), full),
                  pl.BlockSpec((1, vmt.shape[1], n_keys), full),
                  pl.BlockSpec(lamv.shape, const),
                  pl.BlockSpec((DA_V, 1), const),
                  pl.BlockSpec(wout.shape, const),
                  pl.BlockSpec((1, tq, d), row),
                  pl.BlockSpec((1, 1, 8, d), lambda s: (prev(s) // per, group, 0, 0)),
                  pl.BlockSpec((1, d), const),
                  pl.BlockSpec((ne, d), const)],
        out_specs=[pl.BlockSpec((1, tq, d), row),
                   pl.BlockSpec((1, tq, d), row),
                   pl.BlockSpec((1, ne, tq), lambda s: (prev(s) // per, 0, prev(s) % per))],
        out_shape=[jax.ShapeDtypeStruct((bsz, nq, d), F32),
                   jax.ShapeDtypeStruct((bsz, nq, d), BF16),
                   jax.ShapeDtypeStruct((bsz, ne, nq), F32)],
        scratch_shapes=[pltpu.VMEM((w_out.shape[0], tq), BF16),
                        pltpu.VMEM((ATTN_LOOKAHEAD, min(ATTN_KEY_BLOCK, n_keys), tq), F32)],
        compiler_params=_params("arbitrary"),
        name="attention",
    )(qda, qm, kda, vat, km, vmt, lamv, subln_g.reshape(-1, 1), wout, x, mod, g2.reshape(1, d), w_router.T)


def _kth_largest(a, k):
    rows = a.shape[0]
    lo0 = jnp.full((rows, 1), -1.0, F32)
    hi0 = jnp.max(a, axis=1, keepdims=True)

    def cond(c):
        return c[2] > 0

    def body(c):
        lo, hi, _ = c
        mid = 0.5 * (lo + hi)
        open_ = (mid > lo) & (mid < hi)
        cnt = jnp.sum(jnp.where(a > mid, 1.0, 0.0), axis=1, keepdims=True)
        up = open_ & (cnt >= k)
        down = open_ & (cnt < k)
        lo = jnp.where(up, mid, lo)
        hi = jnp.where(down, mid, hi)
        return lo, hi, jnp.sum(jnp.where(open_, 1.0, 0.0))

    _, hi, _ = lax.while_loop(cond, body, (lo0, hi0, jnp.float32(1.0)))
    return hi


def _select_kernel(lg_ref, slot_ref, aff_ref, first_ref, tri_ref, *, ne, cap):
    rows, n = lg_ref.shape
    rt = 256
    for r in range(0, n, rt):
        ri = lax.broadcasted_iota(jnp.int32, (rt, n), 0) + r
        ci = lax.broadcasted_iota(jnp.int32, (rt, n), 1)
        tri_ref[r:r + rt, :] = jnp.where(ri < ci, 1.0, 0.0).astype(BF16)

    lg = lg_ref[...].reshape(rows // ne, ne, n)
    e = jnp.exp(lg - jnp.max(lg, axis=1, keepdims=True))
    aff = (e / jnp.sum(e, axis=1, keepdims=True)).reshape(rows, n)
    aff_ref[...] = aff
    thr = _kth_largest(aff, cap)
    gt = aff > thr
    eq = aff == thr
    need = cap - jnp.sum(jnp.where(gt, 1.0, 0.0), axis=1, keepdims=True)
    tri = tri_ref[...]
    before = _dot(jnp.where(eq, 1.0, 0.0).astype(BF16), tri)
    sel = gt | (eq & (before < need))
    pos = _dot(jnp.where(sel, 1.0, 0.0).astype(BF16), tri)
    slot_ref[...] = jnp.where(sel, pos, -1.0).astype(jnp.int32)
    ti = lax.broadcasted_iota(jnp.int32, (n, LANES), 0)
    bi = lax.broadcasted_iota(jnp.int32, (n, LANES), 1)
    before_block = jnp.where(ti < bi * GATHER_TOKEN_BLOCK, 1.0, 0.0).astype(BF16)
    first_ref[...] = _dot(jnp.where(sel, 1.0, 0.0).astype(BF16), before_block).astype(jnp.int32)


def _select(logits, cap):
    bsz, ne, n = logits.shape
    blk = pl.BlockSpec((bsz * ne, n), lambda i: (0, 0))
    slot, aff, first = pl.pallas_call(
        functools.partial(_select_kernel, ne=ne, cap=cap),
        grid=(1,),
        in_specs=[blk],
        out_specs=[blk, blk, pl.BlockSpec((bsz * ne, LANES), lambda i: (0, 0))],
        out_shape=[jax.ShapeDtypeStruct((bsz * ne, n), jnp.int32),
                   jax.ShapeDtypeStruct((bsz * ne, n), F32),
                   jax.ShapeDtypeStruct((bsz * ne, LANES), jnp.int32)],
        scratch_shapes=[pltpu.VMEM((n, n), BF16)],
        compiler_params=_params("arbitrary"),
        name="moe_select",
    )(logits.reshape(bsz * ne, n))
    return slot.reshape(bsz, ne, n), aff.reshape(bsz, ne, n), first


GATHER_EXPERTS = 8


GATHER_TOKEN_BLOCK = 256
GATHER_WINDOW = 64


def _gather_dense(h_ref, slot_ref, aff_ref, xe_ref, gate_ref, onehot_ref, cap):
    n = h_ref.shape[1]
    pi = lax.broadcasted_iota(jnp.int32, (cap, n), 0)
    for e in range(GATHER_EXPERTS):
        hit = slot_ref[0, e:e + 1, :] == pi
        onehot_ref[e * cap:(e + 1) * cap, :] = jnp.where(hit, 1.0, 0.0).astype(BF16)
        gate_ref[e] = jnp.sum(jnp.where(hit, aff_ref[0, e:e + 1, :], 0.0), axis=1, keepdims=True)
    xe = _dot(onehot_ref[...], h_ref[0]).astype(BF16)
    xe_ref[...] = xe.reshape(GATHER_EXPERTS, cap, xe.shape[1])


def _gather_kernel(first_ref, h_ref, slot_ref, aff_ref, xe_ref, gate_ref, onehot_ref, *scratch, cap, windowed):
    if not windowed:
        _gather_dense(h_ref, slot_ref, aff_ref, xe_ref, gate_ref, onehot_ref, cap)
        return
    win_ref, acc_ref, gacc_ref = scratch
    n = h_ref.shape[1]
    tb, win, ge = GATHER_TOKEN_BLOCK, GATHER_WINDOW, GATHER_EXPERTS
    nblk = n // tb
    row0 = (pl.program_id(0) * pl.num_programs(1) + pl.program_id(1)) * ge
    starts, fits = {}, None
    for e in range(ge):
        for j in range(nblk):
            lo = first_ref[(row0 + e) * (nblk + 1) + j]
            hi = first_ref[(row0 + e) * (nblk + 1) + j + 1]
            start = pl.multiple_of(jnp.minimum((lo // 16) * 16, cap - win), 16)
            starts[e, j] = start
            fit = hi - start <= win
            fits = fit if fits is None else jnp.logical_and(fits, fit)

    @pl.when(fits)
    def _():
        acc_ref[...] = jnp.zeros_like(acc_ref)
        gacc_ref[...] = jnp.zeros_like(gacc_ref)
        ri = lax.broadcasted_iota(jnp.int32, (win, tb), 0)
        for j in range(nblk):
            cols = slice(j * tb, (j + 1) * tb)
            for e in range(ge):
                hit = slot_ref[0, e:e + 1, cols] == ri + starts[e, j]
                win_ref[e * win:(e + 1) * win, :] = jnp.where(hit, 1.0, 0.0).astype(BF16)
                gacc_ref[e, pl.ds(starts[e, j], win), :] += jnp.sum(
                    jnp.where(hit, aff_ref[0, e:e + 1, cols], 0.0), axis=1, keepdims=True)
            x = _dot(win_ref[...], h_ref[0, cols, :])
            for e in range(ge):
                acc_ref[e, pl.ds(starts[e, j], win), :] += x[e * win:(e + 1) * win]
        xe_ref[...] = acc_ref[...].astype(BF16)
        gate_ref[...] = gacc_ref[...]

    @pl.when(jnp.logical_not(fits))
    def _():
        _gather_dense(h_ref, slot_ref, aff_ref, xe_ref, gate_ref, onehot_ref, cap)


def _gather(h, slot, aff, first, cap):
    bsz, n, d = h.shape
    ne = slot.shape[1]
    ge = GATHER_EXPERTS
    nblk = n // GATHER_TOKEN_BLOCK
    windowed = cap > GATHER_WINDOW and nblk > 1
    scratch = [pltpu.VMEM((ge * cap, n), BF16)]
    if windowed:
        scratch += [pltpu.VMEM((ge * GATHER_WINDOW, GATHER_TOKEN_BLOCK), BF16), pltpu.VMEM((ge, cap, d), F32),
                    pltpu.VMEM((ge, cap, 1), F32)]
    grid_spec = pltpu.PrefetchScalarGridSpec(
        num_scalar_prefetch=1,
        grid=(bsz, ne // ge),
        in_specs=[pl.BlockSpec((1, n, d), lambda b, j, first: (b, 0, 0)),
                  pl.BlockSpec((1, ge, n), lambda b, j, first: (b, j, 0)),
                  pl.BlockSpec((1, ge, n), lambda b, j, first: (b, j, 0))],
        out_specs=[pl.BlockSpec((ge, cap, d), lambda b, j, first: (j, b, 0)),
                   pl.BlockSpec((ge, cap, 1), lambda b, j, first: (j, b, 0))],
        scratch_shapes=scratch)
    return pl.pallas_call(
        functools.partial(_gather_kernel, cap=cap, windowed=windowed),
        grid_spec=grid_spec,
        out_shape=[jax.ShapeDtypeStruct((ne, bsz * cap, d), BF16),
                   jax.ShapeDtypeStruct((ne, bsz * cap, 1), F32)],
        compiler_params=_params("arbitrary", "arbitrary"),
        name="moe_gather",
    )(first[:, :nblk + 1].reshape(-1), h, slot, aff)


def _ffn_kernel(*refs, nseg, rows_per_pass):
    x_refs = refs[:nseg]
    gate_refs = refs[nseg:2 * nseg]
    wg_ref, wu_ref, wd_ref = refs[2 * nseg:2 * nseg + 3]
    y_refs = refs[2 * nseg + 3:3 * nseg + 3]
    acc_refs = refs[3 * nseg + 3:]
    f = pl.program_id(1)
    passes = []
    for x_ref, acc_ref in zip(x_refs, acc_refs):
        m = x_ref.shape[1]
        step = min(rows_per_pass, m)
        passes += [(x_ref, acc_ref, slice(r, r + step)) for r in range(0, m, step)]

    def run(first):
        wg = wg_ref[0].astype(BF16)
        wu = wu_ref[0].astype(BF16)
        wd = wd_ref[0].astype(BF16)
        for x_ref, acc_ref, rows in passes:
            x = x_ref[0, rows, :]
            part = _dot((_silu(_dot(x, wg)) * _dot(x, wu)).astype(BF16), wd)
            if first:
                acc_ref[rows, :] = part
            else:
                acc_ref[rows, :] += part

    @pl.when(f == 0)
    def _():
        run(True)

    @pl.when(f > 0)
    def _():
        run(False)

    @pl.when(f == pl.num_programs(1) - 1)
    def _():
        for y_ref, gate_ref, acc_ref in zip(y_refs, gate_refs, acc_refs):
            y_ref[0] = (acc_ref[...] * gate_ref[0]).astype(BF16)


def _ffn(xes, gates, w_gate, w_up, w_down, layer):
    _, ne, d, ff = w_gate.shape
    tf = 256
    nseg = len(xes)
    return pl.pallas_call(
        functools.partial(_ffn_kernel, nseg=nseg, rows_per_pass=512),
        grid=(ne, ff // tf),
        in_specs=[pl.BlockSpec((1, xe.shape[1], d), lambda e, f: (e, 0, 0)) for xe in xes]
        + [pl.BlockSpec((1, xe.shape[1], 1), lambda e, f: (e, 0, 0)) for xe in xes]
        + [pl.BlockSpec((None, 1, d, tf), lambda e, f: (layer, e, 0, f)),
           pl.BlockSpec((None, 1, d, tf), lambda e, f: (layer, e, 0, f)),
           pl.BlockSpec((None, 1, tf, d), lambda e, f: (layer, e, f, 0))],
        out_specs=[pl.BlockSpec((1, xe.shape[1], d), lambda e, f: (e, 0, 0)) for xe in xes],
        out_shape=[jax.ShapeDtypeStruct(xe.shape, BF16) for xe in xes],
        scratch_shapes=[pltpu.VMEM(xe.shape[1:], F32) for xe in xes],
        compiler_params=_params("arbitrary", "arbitrary"),
        name="moe_ffn",
    )(*xes, *gates, w_gate, w_up, w_down)


def _combine_dense(slot_ref, ye_ref, onehot_ref, cap):
    ne = ye_ref.shape[0]
    if cap % LANES == 0:
        pi = lax.broadcasted_iota(jnp.int32, (1, cap), 1)
        for e in range(ne):
            hit = slot_ref[0, :, e:e + 1] == pi
            onehot_ref[:, e * cap:(e + 1) * cap] = jnp.where(hit, 1.0, 0.0).astype(BF16)
    else:
        ei = lax.broadcasted_iota(jnp.int32, (ne, ne * cap), 0)
        li = lax.broadcasted_iota(jnp.int32, (ne, ne * cap), 1)
        spread = jnp.where((li >= ei * cap) & (li < (ei + 1) * cap), 1.0, 0.0).astype(BF16)
        slots = _dot(slot_ref[0].astype(F32).astype(BF16), spread)
        group = jnp.sum(jnp.where(li >= (ei + 1) * cap, 1, 0), axis=0, keepdims=True)
        pos = (li[0:1] - group * cap).astype(F32)
        onehot_ref[...] = jnp.where(slots == pos, 1.0, 0.0).astype(BF16)
    ye = ye_ref[...]
    return _dot(onehot_ref[...], ye.reshape(ne * cap, ye.shape[2]))


def _combine_kernel(first_ref, slot_ref, ye_ref, x_ref, mod_ref, *rest, cap, final_norm, windowed):
    if final_norm:
        fg_ref, o_ref, onehot_ref = rest[:3]
    else:
        o_ref, onehot_ref = rest[:2]
    ne = ye_ref.shape[0]
    gain = mod_ref[0, 0][5:6]

    def finish(rows, moe, norm=final_norm):
        y = x_ref[0, rows, :] + gain * moe
        if norm:
            y = _rms(y) * fg_ref[...]
        o_ref[0, rows, :] = y

    tm = x_ref.shape[1]
    if not windowed:
        finish(slice(0, tm), _combine_dense(slot_ref, ye_ref, onehot_ref, cap))
        return
    gwin_ref, ywin_ref = rest[-2:]
    tb, win = GATHER_TOKEN_BLOCK, GATHER_WINDOW
    per_step = tm // tb
    nblk = per_step * pl.num_programs(1)
    starts, fits = {}, None
    for e in range(ne):
        for jj in range(per_step):
            at = (pl.program_id(0) * ne + e) * (nblk + 1) + pl.program_id(1) * per_step + jj
            lo, hi = first_ref[at], first_ref[at + 1]
            start = pl.multiple_of(jnp.minimum((lo // 16) * 16, cap - win), 16)
            starts[e, jj] = start
            fit = hi - start <= win
            fits = fit if fits is None else jnp.logical_and(fits, fit)

    @pl.when(fits)
    def _():
        lane = lax.broadcasted_iota(jnp.int32, (1, LANES), 1)
        low = lane < win
        for jj in range(per_step):
            rows = slice(jj * tb, (jj + 1) * tb)
            for e in range(ne):
                ywin_ref[e * win:(e + 1) * win, :] = ye_ref[e, pl.ds(starts[e, jj], win), :]
            for k in range(ne * win // LANES):
                e0, e1 = 2 * k, 2 * k + 1
                want = jnp.where(low, starts[e0, jj] + lane, starts[e1, jj] + lane - win)
                have = jnp.where(low, slot_ref[0, rows, e0:e0 + 1], slot_ref[0, rows, e1:e1 + 1])
                gwin_ref[:, k * LANES:(k + 1) * LANES] = jnp.where(have == want, 1.0, 0.0).astype(BF16)
            finish(rows, _dot(gwin_ref[...], ywin_ref[...]), norm=False)
        if final_norm:
            o_ref[0] = _rms(o_ref[0]) * fg_ref[...]

    @pl.when(jnp.logical_not(fits))
    def _():
        finish(slice(0, tm), _combine_dense(slot_ref, ye_ref, onehot_ref, cap))


def _combine(slot_t, ye, x, mod, group, first, cap, final_g):
    bsz, n, d = x.shape
    ne = slot_t.shape[2]
    tm = min(n, 1024)
    nblk = n // GATHER_TOKEN_BLOCK
    windowed = cap > GATHER_WINDOW and nblk > 1 and 2 * GATHER_WINDOW == LANES
    final_norm = final_g is not None
    in_specs = [pl.BlockSpec((1, tm, ne), lambda b, i, first: (b, i, 0)),
                pl.BlockSpec((ne, cap, d), lambda b, i, first: (0, b, 0)),
                pl.BlockSpec((1, tm, d), lambda b, i, first: (b, i, 0)),
                pl.BlockSpec((1, 1, 8, d), lambda b, i, first: (b, group, 0, 0))]
    args = [slot_t, ye, x, mod]
    if final_norm:
        in_specs.append(pl.BlockSpec((1, d), lambda b, i, first: (0, 0)))
        args.append(final_g.reshape(1, d))
    scratch = [pltpu.VMEM((tm, ne * cap), BF16)]
    if windowed:
        scratch += [pltpu.VMEM((GATHER_TOKEN_BLOCK, ne * GATHER_WINDOW), BF16),
                    pltpu.VMEM((ne * GATHER_WINDOW, d), BF16)]
    grid_spec = pltpu.PrefetchScalarGridSpec(
        num_scalar_prefetch=1,
        grid=(bsz, n // tm),
        in_specs=in_specs,
        out_specs=pl.BlockSpec((1, tm, d), lambda b, i, first: (b, i, 0)),
        scratch_shapes=scratch)
    return pl.pallas_call(
        functools.partial(_combine_kernel, cap=cap, final_norm=final_norm, windowed=windowed),
        grid_spec=grid_spec,
        out_shape=jax.ShapeDtypeStruct((bsz, n, d), F32),
        compiler_params=_params("arbitrary", "arbitrary"),
        name="moe_combine",
    )(first[:, :nblk + 1].reshape(-1), *args)


def _moe(streams, groups, mod, w_gate, w_up, w_down, layer, final_g=None):
    xs = [s[0] for s in streams]
    routed = []
    for x, h, logits in streams:
        cap = EC_CAPACITY_FACTOR * x.shape[1] // logits.shape[1]
        slot, aff, first = _select(logits, cap)
        xe, gate = _gather(h, slot, aff, first, cap)
        routed.append((cap, slot, xe, gate, first))
    yes = _ffn([r[2] for r in routed], [r[3] for r in routed], w_gate, w_up, w_down, layer)
    return [_combine(jnp.swapaxes(slot, 1, 2), ye, x, mod, group, first, cap, final_g)
            for x, group, ye, (cap, slot, _, _, first) in zip(xs, groups, yes, routed)]


def _chunk_masks(tm, direction):
    ri = lax.broadcasted_iota(jnp.int32, (tm, tm), 0)
    ci = lax.broadcasted_iota(jnp.int32, (tm, tm), 1)
    shift = CHUNK.bit_length() - 1
    same = jnp.right_shift(ri, shift) == jnp.right_shift(ci, shift)
    return same & ((ci <= ri) if direction == 0 else (ci >= ri))


def _chunk_logdecay(la, direction):
    cum = jnp.where(_chunk_masks(la.shape[0], direction), 1.0, 0.0).astype(BF16)
    hi, lo = _split2(la * LOG2E)
    return _dot(cum, hi) + _dot(cum, lo)


def _scan_operands(q, k, b, direction, refs):
    q_in_ref, q_mid_ref, k_mid_ref, k_out_ref, decay_ref = refs
    tm, n = q.shape
    nc = tm // CHUNK
    b = b.reshape(nc, CHUNK, n)
    last, mid = (CHUNK - 1, CHUNK // 2) if direction == 0 else (0, CHUNK // 2 - 1)
    b_tot = b[:, last:last + 1]
    b_mid = b[:, mid:mid + 1]
    q = q.reshape(nc, CHUNK, n)
    k = k.reshape(nc, CHUNK, n)
    q_in_ref[direction, 0] = (q * jnp.exp2(b)).astype(BF16).reshape(tm, n)
    q_mid_ref[direction, 0] = (q * jnp.exp2(b - b_mid)).astype(BF16).reshape(tm, n)
    k_mid_ref[direction, 0] = (k * jnp.exp2(b_mid - b)).astype(BF16).reshape(tm, n)
    k_out_ref[direction, 0] = (k * jnp.exp2(b_tot - b)).astype(BF16).reshape(tm, n)
    decay = jnp.exp2(b_tot).reshape(nc, n)
    decay_ref[direction, 0, 0] = jnp.concatenate([decay] * (8 // nc), axis=0)


def _rec_proj_kernel(xc_ref, xl_ref, mod_ref, g_ref, w_ref, gkup_ref, gkb_ref, lbl_ref, *out_refs, layer):
    gla_refs, gv_ref, gg_ref = out_refs[0:5], out_refs[5], out_refs[6]
    hg_refs, hv_ref, hg_ref = out_refs[7:12], out_refs[12], out_refs[13]
    mod = mod_ref[0, 0]
    h = _modulate(_stream_block(xc_ref, xl_ref), g_ref[...], mod[0:1], mod[1:2]).astype(BF16)
    n = GLA_HEADS * LANES
    c0 = 2 * n + LANES
    c1 = c0 + 3 * n
    head = _dot(h, w_ref[:, 0:c0])
    gate = _dot(head[:, 2 * n:c0].astype(BF16), gkup_ref[...]) + gkb_ref[...]
    qk = _dot(h, w_ref[:, c0:c1])
    lbl = lbl_ref[...]
    e = jnp.exp(lbl - jnp.max(lbl, axis=0, keepdims=True))
    sm = e / jnp.sum(e, axis=0, keepdims=True)
    lb = jnp.sum(sm[0:layer + 1], axis=0, keepdims=True) - sm[0:1]
    f = lb + (1.0 - lb) * jax.nn.sigmoid(head[:, 0:2 * n])
    fs = [f[:, dd * n:(dd + 1) * n] for dd in range(2)]
    rest = _dot(h, w_ref[:, c1:])
    b_hg = [_chunk_logdecay(jnp.log(fs[dd]), dd) for dd in range(2)]
    b_gla = [_chunk_logdecay(jax.nn.log_sigmoid(gate[:, dd * n:(dd + 1) * n]) / GLA_GATE_NORM, dd)
             for dd in range(2)]
    hq = _silu(qk[:, 2 * n:3 * n])
    gq = qk[:, 0:n] * GLA_DK ** -0.5
    gk = qk[:, n:2 * n]
    for dd in range(2):
        _scan_operands(hq, 1.0 - fs[dd], b_hg[dd], dd, hg_refs)
    for dd in range(2):
        _scan_operands(gq, gk, b_gla[dd], dd, gla_refs)
    gv_ref[0] = rest[:, 0:n].astype(BF16)
    gg_ref[0] = rest[:, n:2 * n].astype(BF16)
    hv_ref[0] = rest[:, 2 * n:3 * n].astype(BF16)
    hg_ref[0] = rest[:, 3 * n:4 * n].astype(BF16)


def _pad_heads(w, heads, dk):
    lead = w.shape[:-1]
    w3 = w.reshape(lead + (heads, dk))
    return jnp.pad(w3, [(0, 0)] * len(lead) + [(0, 0), (0, LANES - dk)]).reshape(lead + (heads * LANES,))


def _rec_project(xc, xl, mod, g, w_in, gk_up, gk_bias, lb_logits, layer):
    bsz, n_ctx, d = xc.shape
    t = n_ctx + xl.shape[1]
    tm = ROW_TILE
    hk = GLA_HEADS * GLA_DK
    hv = GLA_HEADS * GLA_DV
    hf = HG_HEADS * HG_DF
    sizes = (hk, hk, hv, hv, 2 * GLA_GATE_RANK, hf, 2 * hf, HG_HEADS * HG_DV, HG_HEADS * HG_DV)
    offs = [0]
    for s in sizes:
        offs.append(offs[-1] + s)
    gq, gk, gv, gg, gdown, hq, hff, hi, hgt = [w_in[:, offs[i]:offs[i + 1]] for i in range(9)]
    w = jnp.concatenate([hff, jnp.pad(gdown, ((0, 0), (0, LANES - 2 * GLA_GATE_RANK))),
                         _pad_heads(gq, GLA_HEADS, GLA_DK), _pad_heads(gk, GLA_HEADS, GLA_DK), hq,
                         gv, gg, hi, hgt], axis=1).astype(BF16)
    n = GLA_HEADS * LANES
    ups = [_pad_heads(gk_up[dd], GLA_HEADS, GLA_DK) for dd in range(2)]
    zero = jnp.zeros_like(ups[0])
    gkup = jnp.concatenate([jnp.concatenate([ups[0], zero], axis=1), jnp.concatenate([zero, ups[1]], axis=1),
                            jnp.zeros((LANES - 2 * GLA_GATE_RANK, 2 * n), F32)], axis=0).astype(BF16)
    gkb = jnp.concatenate([_pad_heads(gk_bias[dd], GLA_HEADS, GLA_DK) for dd in range(2)]).reshape(1, 2 * n)
    row = lambda b, i: (b, i, 0)
    drow = lambda b, i: (0, b, i, 0)
    const = lambda b, i: (0, 0)
    one = lambda dt: (pl.BlockSpec((1, tm, n), row), jax.ShapeDtypeStruct((bsz, t, n), dt))
    two = (pl.BlockSpec((2, 1, tm, n), drow), jax.ShapeDtypeStruct((2, bsz, t, n), BF16))
    dec = (pl.BlockSpec((2, 1, 1, 8, n), lambda b, i: (0, b, i, 0, 0)),
           jax.ShapeDtypeStruct((2, bsz, t // tm, 8, n), F32))
    mixer = [two, two, two, two, dec, one(BF16), one(BF16)]
    outs = mixer + mixer
    return pl.pallas_call(
        functools.partial(_rec_proj_kernel, layer=layer),
        grid=(bsz, t // tm),
        in_specs=_stream_specs(d) + [
            _mod_spec(d),
            pl.BlockSpec((1, d), const),
            pl.BlockSpec(w.shape, const),
            pl.BlockSpec(gkup.shape, const),
            pl.BlockSpec(gkb.shape, const),
            pl.BlockSpec((lb_logits.shape[0], 2 * hf), const)],
        out_specs=[o[0] for o in outs],
        out_shape=[o[1] for o in outs],
        compiler_params=_params("arbitrary", "arbitrary"),
        name="rec_project",
    )(xc, xl, mod, g.reshape(1, d), w, gkup, gkb, lb_logits.reshape(lb_logits.shape[0], 2 * hf))


def _scan_kernel(*refs, heads, directions):
    ns = len(directions)
    ins = [refs[6 * i:6 * i + 6] for i in range(ns)]
    o_refs = refs[6 * ns:7 * ns]
    st_ref = refs[7 * ns]
    tm = o_refs[0].shape[1]
    nc = tm // CHUNK

    @pl.when(pl.program_id(1) == 0)
    def _():
        st_ref[...] = jnp.zeros_like(st_ref)

    lanes = [slice(LANES * h, LANES * (h + 1)) for h in range(heads)]
    chunk_rows = [slice(c * CHUNK, (c + 1) * CHUNK) for c in range(nc)]
    units = [(i, h) for i in range(ns) for h in range(heads)]
    vs = {(i, h): ins[i][5][0, :, lanes[h]] for i, h in units}
    scores = {(i, h): _dot_nt(ins[i][1][0, 0, :, lanes[h]], ins[i][2][0, 0, :, lanes[h]]) for i, h in units}
    updates = {(i, h): [_dot_tn(vs[i, h][rows], ins[i][3][0, 0, rows, lanes[h]]) for rows in chunk_rows]
               for i, h in units}
    causal = [_chunk_masks(tm, d) for d in range(2)]
    intra = {(i, h): _dot(jnp.where(causal[directions[i]], scores[i, h], 0.0).astype(BF16), vs[i, h])
             for i, h in units}
    order = [list(range(nc)), list(range(nc - 1, -1, -1))]
    entering = {}
    for i, h in units:
        st = st_ref[i, h]
        for c in order[directions[i]]:
            entering[i, h, c] = st.astype(BF16)
            st = st * ins[i][4][0, 0, 0, c:c + 1, lanes[h]] + updates[i, h][c]
        st_ref[i, h] = st
    for i, h in units:
        for c in order[directions[i]]:
            rows = chunk_rows[c]
            inter = _dot_nt(ins[i][0][0, 0, rows, lanes[h]], entering[i, h, c])
            o_refs[i][0, rows, lanes[h]] = (intra[i, h][rows] + inter).astype(BF16)


def _scan(mixers, n_ctx):
    _, bsz, t, n = mixers[0][0].shape
    heads = n // LANES
    tm = ROW_TILE
    nb = t // tm
    ncb = n_ctx // tm

    def blk(d, s):
        return s if d == 0 else jnp.where(s < ncb, ncb - 1 - s, nb - 1 - (s - ncb))

    in_specs, args, directions = [], [], []
    for q_in, q_mid, k_mid, k_out, decay, v in mixers:
        for d in range(2):
            drow = lambda b, s, d=d: (d, b, blk(d, s), 0)
            in_specs += [pl.BlockSpec((1, 1, tm, n), drow)] * 4
            in_specs += [pl.BlockSpec((1, 1, 1, 8, n), lambda b, s, d=d: (d, b, blk(d, s), 0, 0)),
                         pl.BlockSpec((1, tm, n), lambda b, s, d=d: (b, blk(d, s), 0))]
            args += [q_in, q_mid, k_mid, k_out, decay, v]
            directions.append(d)
    outs = pl.pallas_call(
        functools.partial(_scan_kernel, heads=heads, directions=tuple(directions)),
        grid=(bsz, nb),
        in_specs=in_specs,
        out_specs=[pl.BlockSpec((1, tm, n), lambda b, s, d=d: (b, blk(d, s), 0)) for d in directions],
        out_shape=[jax.ShapeDtypeStruct((bsz, t, n), BF16)] * len(directions),
        scratch_shapes=[pltpu.VMEM((len(directions), heads, LANES, LANES), F32)],
        compiler_params=_params("arbitrary", "arbitrary"),
        name="rec_scan",
    )(*args)
    return [tuple(outs[2 * i:2 * i + 2]) for i in range(len(mixers))]


def _rec_out_kernel(og0_ref, og1_ref, oh0_ref, oh1_ref, gg_ref, hg_ref, gn_ref, hn_ref, wout_ref,
                    x_ref, mod_ref, g2_ref, wrt_ref, o_ref, h_ref, lg_ref):
    parts = []
    for fwd_ref, bwd_ref, gate_ref, n_ref in ((og0_ref, og1_ref, gg_ref, gn_ref),
                                              (oh0_ref, oh1_ref, hg_ref, hn_ref)):
        for h in range(fwd_ref.shape[2] // LANES):
            sl = slice(LANES * h, LANES * (h + 1))
            o = fwd_ref[0, :, sl].astype(F32) + bwd_ref[0, :, sl].astype(F32)
            parts.append(_rms(o) * n_ref[...] * _silu(gate_ref[0, :, sl].astype(F32)))
    a = jnp.concatenate(parts, axis=1).astype(BF16)
    mod = mod_ref[0, 0]
    y = x_ref[0] + mod[2:3] * _dot(a, wout_ref[...])
    o_ref[0] = y
    _route_rows(y, mod, g2_ref, wrt_ref, h_ref, lg_ref)


def _rec_readout(og, oh, gg, hg, gn, hn, w_out, xl, mod, g2, w_router, n_ctx):
    bsz, seq, d = xl.shape
    tm = ROW_TILE
    ncb = n_ctx // tm
    n = gg.shape[2]
    row = lambda b, i: (b, i + ncb, 0)
    const = lambda b, i: (0, 0)
    wout = w_out.astype(BF16)
    r_in, r_out, r_shape = _route_specs(bsz, seq, d, w_router.shape[1], tm)
    return pl.pallas_call(
        _rec_out_kernel,
        grid=(bsz, seq // tm),
        in_specs=[pl.BlockSpec((1, tm, n), row),
                  pl.BlockSpec((1, tm, n), row),
                  pl.BlockSpec((1, tm, n), row),
                  pl.BlockSpec((1, tm, n), row),
                  pl.BlockSpec((1, tm, n), row),
                  pl.BlockSpec((1, tm, n), row),
                  pl.BlockSpec((1, LANES), const),
                  pl.BlockSpec((1, LANES), const),
                  pl.BlockSpec(wout.shape, const),
                  pl.BlockSpec((1, tm, d), lambda b, i: (b, i, 0)),
                  _mod_spec(d, 1)] + r_in,
        out_specs=[pl.BlockSpec((1, tm, d), lambda b, i: (b, i, 0))] + r_out,
        out_shape=[jax.ShapeDtypeStruct((bsz, seq, d), F32)] + r_shape,
        compiler_params=_params("arbitrary", "arbitrary"),
        name="rec_readout",
    )(*og, *oh, gg, hg, gn.reshape(1, -1), hn.reshape(1, -1), wout, xl, mod, g2.reshape(1, d), w_router.T)


def kernel(x, c, ctx, c_ctx, ada_w, ada_b, norm1_g, norm2_g, att_w_in, mla_q_norm_g, mla_q_up, mla_kv_norm_g, mla_kv_up, da_lam_q1, da_lam_k1, da_lam_q2, da_lam_k2, da_subln_g, att_w_out, rec_w_in, gla_gk_up, gla_gk_bias, gla_norm_g, hg_lb_logits, hg_norm_g, rec_w_out, moe_router, moe_w_gate, moe_w_up, moe_w_down, final_norm_g):
    bsz, seq, d = x.shape
    n_ctx = ctx.shape[1]
    depth = ada_w.shape[0]
    assert depth == 2 and n_ctx == ROW_TILE and seq % ROW_TILE == 0 and bsz + 1 <= 16

    cvec = jnp.concatenate([c, c_ctx[None, :], jnp.zeros((16 - bsz - 1, d), F32)], axis=0)
    ada = _ada_vectors(cvec, ada_w, ada_b).reshape(depth, 16, 6, d)
    ada = jnp.pad(ada, ((0, 0), (0, 0), (0, 2), (0, 0)))
    mods = jnp.stack([jnp.broadcast_to(ada[:, bsz][:, None], (depth, bsz, 8, d)), ada[:, :bsz]], axis=2)
    xc, xl = ctx, x

    mod = mods[0]
    tabs_da = _rope_tables(n_ctx, seq, DA_QK, 0, DA_QK)
    tabs_m = _rope_tables(n_ctx, seq, MLA_ROPE, MLA_NOPE, LANES)
    proj = _attn_project(xc, xl, mod, norm1_g[0], att_w_in[0], mla_q_norm_g[0], mla_q_up[0],
                         mla_kv_norm_g[0], mla_kv_up[0], tabs_da, tabs_m)
    lamv = jnp.stack([da_lam_q1[0], da_lam_k1[0], da_lam_q2[0], da_lam_k2[0]])
    lam_init = 0.8 - 0.6 * math.exp(-0.3 * 0)
    sc = _attention(*proj, lamv, da_subln_g[0], att_w_out[0], xc, mod, norm2_g[0], moe_router[0], 0, 0, n_ctx,
                    lam_init)
    sl = _attention(*proj, lamv, da_subln_g[0], att_w_out[0], xl, mod, norm2_g[0], moe_router[0], 1,
                    n_ctx // ROW_TILE, n_ctx + seq, lam_init)
    xc, xl = _moe([sc, sl], [0, 1], mod, moe_w_gate, moe_w_up, moe_w_down, 0)

    mod = mods[1]
    rec = _rec_project(xc, xl, mod, norm1_g[1], rec_w_in[0], gla_gk_up[0], gla_gk_bias[0], hg_lb_logits, 1)
    og, oh = _scan([rec[0:6], rec[7:13]], n_ctx)
    sl = _rec_readout(og, oh, rec[6], rec[13], gla_norm_g[0], hg_norm_g[0], rec_w_out[0], xl, mod, norm2_g[1],
                      moe_router[1], n_ctx)
    (out,) = _moe([sl], [1], mod, moe_w_gate, moe_w_up, moe_w_down, 1, final_g=final_norm_g)
    return out
```
